```python
import jax, jax.numpy as jnp
from jax import lax
import numpy as np

D_MODEL = 2048
BATCH = 8
SEQ = 4096
DEPTH = 4

CHUNK = 64
Q_BLOCK = 128
N_MIXERS = 2
N_MLA_LAYERS = (DEPTH + 1) // 2
N_FOX_LAYERS = DEPTH // 2
RMS_EPS = 1e-6
NEG_INF = -1e30
MAX_POS_OFFSET = 8192

MLA_HEADS = D_MODEL // 128
MLA_Q_LORA = 512
MLA_KV_LORA = 512
MLA_NOPE = 128
MLA_ROPE = 64
MLA_V = 128
ROPE_THETA = 10000.0

FOX_HEAD_DIM = 128
FOX_HEADS = D_MODEL // FOX_HEAD_DIM
FOX_FORGET_BIAS = 3.0

MOE_GROUPS = 8
MOE_EXPERTS_PER_GROUP = 4
MOE_EXPERTS = MOE_GROUPS * MOE_EXPERTS_PER_GROUP
MOE_TOP_K = 2
MOE_D_FF = D_MODEL // 4

kernel_name = "hybrid_mla_fox_hmoe_streaming_trunk"


def rmsnorm(x, gain):
    xf = x.astype(jnp.float32)
    y = xf * lax.rsqrt(jnp.mean(xf * xf, axis=-1, keepdims=True) + RMS_EPS)
    return (y * gain.astype(jnp.float32)).astype(x.dtype)


def rope_tables(positions):
    inv_freq = ROPE_THETA ** (-jnp.arange(0, MLA_ROPE, 2, dtype=jnp.float32) / MLA_ROPE)
    ang = positions.astype(jnp.float32)[..., None] * inv_freq
    return jnp.cos(ang), jnp.sin(ang)


def apply_rope(x, cos, sin):
    xf = x.astype(jnp.float32)
    half = xf.shape[-1] // 2
    x1, x2 = xf[..., :half], xf[..., half:]
    c, s = cos[:, :, None, :], sin[:, :, None, :]
    return jnp.concatenate([x1 * c - x2 * s, x2 * c + x1 * s], axis=-1).astype(x.dtype)


def blocked_attention(q, k, v, scale, logit_bias):
    B, S, H, dq = q.shape
    dv = v.shape[-1]
    nb = S // Q_BLOCK
    qb = jnp.moveaxis(q.reshape(B, nb, Q_BLOCK, H, dq), 1, 0)

    def one_block(args):
        i, q_i = args
        qpos = i * Q_BLOCK + jnp.arange(Q_BLOCK)
        s = jnp.einsum('bqhd,bkhd->bhqk', q_i, k, preferred_element_type=jnp.float32) * scale
        s = s + logit_bias(i, qpos)
        p = jax.nn.softmax(s, axis=-1).astype(v.dtype)
        return jnp.einsum('bhqk,bkhd->bqhd', p, v)

    o = lax.map(one_block, (jnp.arange(nb), qb))
    return jnp.moveaxis(o, 0, 1).reshape(B, S, H, dv)


def mla_mixer(x, cos, sin, w_dq, q_norm, w_uq, w_dkv, kv_norm, w_ukv, w_o):
    B, S, _ = x.shape
    H = MLA_HEADS
    c_q = rmsnorm(x @ w_dq, q_norm)
    q = (c_q @ w_uq).reshape(B, S, H, MLA_NOPE + MLA_ROPE)
    q = jnp.concatenate([q[..., :MLA_NOPE], apply_rope(q[..., MLA_NOPE:], cos, sin)], axis=-1)
    kv_a = x @ w_dkv
    c_kv = rmsnorm(kv_a[..., :MLA_KV_LORA], kv_norm)
    k_pe = apply_rope(kv_a[..., None, MLA_KV_LORA:], cos, sin)
    kv = (c_kv @ w_ukv).reshape(B, S, H, MLA_NOPE + MLA_V)
    k = jnp.concatenate([kv[..., :MLA_NOPE], jnp.broadcast_to(k_pe, (B, S, H, MLA_ROPE))], axis=-1)
    v = kv[..., MLA_NOPE:]
    k_chunk = jnp.arange(S) // CHUNK

    def chunk_mask(i, qpos):
        allowed = k_chunk[None, :] <= (qpos // CHUNK)[:, None]
        return jnp.where(allowed, 0.0, NEG_INF).astype(jnp.float32)

    o = blocked_attention(q, k, v, (MLA_NOPE + MLA_ROPE) ** -0.5, chunk_mask)
    return o.reshape(B, S, H * MLA_V) @ w_o


def fox_mixer(x, w_qkv, q_norm, k_norm, w_f, b_f, w_og, w_o):
    B, S, _ = x.shape
    H, dh = FOX_HEADS, FOX_HEAD_DIM
    qkv = (x @ w_qkv).reshape(B, S, 3, H, dh)
    q = rmsnorm(qkv[:, :, 0], q_norm)
    k = rmsnorm(qkv[:, :, 1], k_norm)
    v = qkv[:, :, 2]
    log_f = jax.nn.log_sigmoid((x @ w_f + b_f).astype(jnp.float32))
    c = jnp.cumsum(log_f, axis=1).transpose(0, 2, 1)
    kpos = jnp.arange(S)

    def forget_bias(i, qpos):
        c_q = lax.dynamic_slice_in_dim(c, i * Q_BLOCK, Q_BLOCK, axis=2)
        allowed = kpos[None, :] <= qpos[:, None]
        return jnp.where(allowed, c_q[..., :, None] - c[..., None, :], NEG_INF)

    o = blocked_attention(q, k, v, dh ** -0.5, forget_bias)
    gate = jax.nn.sigmoid(x @ w_og)
    return (o.reshape(B, S, H * dh) * gate) @ w_o


def hier_moe(h, w_grp, b_grp, w_rt, b_rt, w_gate, w_up, w_down):
    B, S, D = h.shape
    t = h.reshape(B * S, D)
    T = t.shape[0]
    grp_logits = jnp.einsum('td,dg->tg', t, w_grp, preferred_element_type=jnp.float32) + b_grp.astype(jnp.float32)
    grp_prob = jax.nn.softmax(grp_logits, axis=-1)
    g_sel = jnp.argmax(grp_logits, axis=-1)
    g_w = jnp.take_along_axis(grp_prob, g_sel[:, None], axis=1)
    ex_logits = (jnp.einsum('td,de->te', t, w_rt, preferred_element_type=jnp.float32)
                 + b_rt.astype(jnp.float32)).reshape(T, MOE_GROUPS, MOE_EXPERTS_PER_GROUP)
    idx = jnp.broadcast_to(g_sel[:, None, None], (T, 1, MOE_EXPERTS_PER_GROUP))
    in_grp = jnp.take_along_axis(ex_logits, idx, axis=1)[:, 0]
    in_prob = jax.nn.softmax(in_grp, axis=-1)
    top_w, top_j = lax.top_k(in_prob, MOE_TOP_K)
    top_w = top_w / jnp.sum(top_w, axis=-1, keepdims=True) * g_w
    expert_id = g_sel[:, None] * MOE_EXPERTS_PER_GROUP + top_j
    gates = jnp.sum(jax.nn.one_hot(expert_id, MOE_EXPERTS, dtype=jnp.float32) * top_w[..., None], axis=1)

    def expert_step(acc, p):
        wg, wu, wd, gate_e = p
        hid = jax.nn.silu(t @ wg) * (t @ wu)
        out = jnp.einsum('tf,fd->td', hid, wd, preferred_element_type=jnp.float32)
        return acc + gate_e[:, None] * out, None

    acc, _ = lax.scan(expert_step, jnp.zeros((T, D), jnp.float32), (w_gate, w_up, w_down, gates.T))
    return acc.astype(h.dtype).reshape(B, S, D)


def setup_inputs(seed: int = 0) -> dict:
    key = jax.random.key(seed)
    ks = jax.random.split(key, 32)
    f32 = jnp.float32

    def nrm(k, shape, fan_in):
        return jax.random.normal(k, shape, f32) * (fan_in ** -0.5)

    def gain(k, shape):
        return 1.0 + 0.05 * jax.random.normal(k, shape, f32)

    D, H, dh = D_MODEL, FOX_HEADS, FOX_HEAD_DIM
    x = jax.random.normal(ks[0], (BATCH, SEQ, D), f32)
    positions = (jnp.arange(SEQ, dtype=jnp.int32)[None, :]
                 + jax.random.randint(ks[1], (BATCH, 1), 0, MAX_POS_OFFSET, dtype=jnp.int32))
    return {
        'x': x,
        'positions': positions,
        'attn_norm': gain(ks[2], (DEPTH, D)),
        'ffn_norm': gain(ks[3], (DEPTH, D)),
        'final_norm': gain(ks[4], (D,)),
        'mla_w_dq': nrm(ks[5], (N_MLA_LAYERS, D, MLA_Q_LORA), D),
        'mla_q_norm': gain(ks[6], (N_MLA_LAYERS, MLA_Q_LORA)),
        'mla_w_uq': nrm(ks[7], (N_MLA_LAYERS, MLA_Q_LORA, MLA_HEADS * (MLA_NOPE + MLA_ROPE)), MLA_Q_LORA),
        'mla_w_dkv': nrm(ks[8], (N_MLA_LAYERS, D, MLA_KV_LORA + MLA_ROPE), D),
        'mla_kv_norm': gain(ks[9], (N_MLA_LAYERS, MLA_KV_LORA)),
        'mla_w_ukv': nrm(ks[10], (N_MLA_LAYERS, MLA_KV_LORA, MLA_HEADS * (MLA_NOPE + MLA_V)), MLA_KV_LORA),
        'mla_w_o': nrm(ks[11], (N_MLA_LAYERS, MLA_HEADS * MLA_V, D), MLA_HEADS * MLA_V),
        'fox_w_qkv': nrm(ks[12], (N_FOX_LAYERS, D, 3 * H * dh), D),
        'fox_q_norm': gain(ks[13], (N_FOX_LAYERS, dh)),
        'fox_k_norm': gain(ks[14], (N_FOX_LAYERS, dh)),
        'fox_w_f': nrm(ks[15], (N_FOX_LAYERS, D, H), D),
        'fox_b_f': FOX_FORGET_BIAS + 0.5 * jax.random.normal(ks[16], (N_FOX_LAYERS, H), f32),
        'fox_w_og': nrm(ks[17], (N_FOX_LAYERS, D, H * dh), D),
        'fox_w_o': nrm(ks[18], (N_FOX_LAYERS, H * dh, D), H * dh),
        'moe_w_grp': nrm(ks[19], (DEPTH, D, MOE_GROUPS), D),
        'moe_b_grp': 0.01 * jax.random.normal(ks[20], (DEPTH, MOE_GROUPS), f32),
        'moe_w_rt': nrm(ks[21], (DEPTH, D, MOE_EXPERTS), D),
        'moe_b_rt': 0.01 * jax.random.normal(ks[22], (DEPTH, MOE_EXPERTS), f32),
        'moe_w_gate': nrm(ks[23], (DEPTH, MOE_EXPERTS, D, MOE_D_FF), D),
        'moe_w_up': nrm(ks[24], (DEPTH, MOE_EXPERTS, D, MOE_D_FF), D),
        'moe_w_down': nrm(ks[25], (DEPTH, MOE_EXPERTS, MOE_D_FF, D), MOE_D_FF),
    }


def reference(x, positions, attn_norm, ffn_norm, final_norm,
              mla_w_dq, mla_q_norm, mla_w_uq, mla_w_dkv, mla_kv_norm, mla_w_ukv, mla_w_o,
              fox_w_qkv, fox_q_norm, fox_k_norm, fox_w_f, fox_b_f, fox_w_og, fox_w_o,
              moe_w_grp, moe_b_grp, moe_w_rt, moe_b_rt, moe_w_gate, moe_w_up, moe_w_down):
    cos, sin = rope_tables(positions)
    h = x
    for i in range(DEPTH):
        a = rmsnorm(h, attn_norm[i])
        j = i // N_MIXERS
        if i % N_MIXERS == 0:
            mix = mla_mixer(a, cos, sin, mla_w_dq[j], mla_q_norm[j], mla_w_uq[j],
                            mla_w_dkv[j], mla_kv_norm[j], mla_w_ukv[j], mla_w_o[j])
        else:
            mix = fox_mixer(a, fox_w_qkv[j], fox_q_norm[j], fox_k_norm[j],
                            fox_w_f[j], fox_b_f[j], fox_w_og[j], fox_w_o[j])
        h = h + mix
        h = h + hier_moe(rmsnorm(h, ffn_norm[i]), moe_w_grp[i], moe_b_grp[i], moe_w_rt[i],
                         moe_b_rt[i], moe_w_gate[i], moe_w_up[i], moe_w_down[i])
    return rmsnorm(h, final_norm)
```

```python
import functools

import jax
import jax.numpy as jnp
from jax import lax
from jax.experimental import pallas as pl
from jax.experimental.pallas import tpu as pltpu

F32 = jnp.float32
BF16 = jnp.bfloat16
I32 = jnp.int32
U32 = jnp.uint32

RMS_EPS = 1e-6
NEG_INF = -1e30
CHUNK = 64
MLA_NOPE = 128
MLA_ROPE = 64
MLA_V = 128
MLA_QK_PAD = 256
ROPE_THETA = 10000.0
LOG2_E = 1.4426950408889634
FOX_HEAD_DIM = 128
MOE_GROUPS = 8
MOE_EXPERTS_PER_GROUP = 4
MOE_EXPERTS = MOE_GROUPS * MOE_EXPERTS_PER_GROUP

LANES = 128
SUBLANES = 8
V7X_VMEM_LIMIT_BYTES = 56 * 1024 * 1024

ROW_TILE = 1024
COL_TILE = 1024
ATTN_TILE = 512
ROUTE_TILE = 512
MOVE_TILE = 256
EXPERT_TILE = 256


def _params(*sem):
    return pltpu.CompilerParams(dimension_semantics=sem, vmem_limit_bytes=V7X_VMEM_LIMIT_BYTES)


def _tile(n, t):
    if n <= t:
        return n
    step = LANES if t % LANES == 0 else SUBLANES
    for c in range(t - t % step, 0, -step):
        if n % c == 0:
            return c
    raise ValueError(f"no aligned tile for {n} under {t}")


def _rms(x, gain):
    ms = jnp.mean(x * x, axis=-1, keepdims=True)
    return x * lax.rsqrt(ms + RMS_EPS) * gain


def _dot(a, b):
    return jnp.dot(a, b, preferred_element_type=F32)


def _nt_dot(a, b):
    return lax.dot_general(a, b, (((1,), (1,)), ((), ())), preferred_element_type=F32)


def _pack_halves(x):
    n = x.shape[1] // 2
    hi = lax.bitcast_convert_type(x[:, :n].astype(BF16).astype(F32), U32)
    lo = lax.bitcast_convert_type(x[:, n:].astype(BF16).astype(F32), U32)
    return hi | lax.shift_right_logical(lo, jnp.uint32(16))


def _unpack_halves(p):
    hi = lax.bitcast_convert_type(p & jnp.uint32(0xFFFF0000), F32)
    lo = lax.bitcast_convert_type(lax.shift_left(p, jnp.uint32(16)), F32)
    return hi, lo


def _split3(x):
    a = x.astype(BF16)
    r = x - a.astype(F32)
    b = r.astype(BF16)
    c = (r - b.astype(F32)).astype(BF16)
    return a, b, c


def _rope_lanes(seg, cos_t, sin_t):
    half = MLA_ROPE // 2
    lane = lax.broadcasted_iota(I32, seg.shape, 1)
    swapped = jnp.where(lane < half, pltpu.roll(seg, LANES - half, 1), pltpu.roll(seg, half, 1))
    return seg * cos_t + swapped * sin_t


def _rope_table_kernel(pos_ref, freq_ref, cos_ref, sin_ref):
    ang = pos_ref[...] * freq_ref[...]
    lane = lax.broadcasted_iota(I32, ang.shape, 1)
    half = MLA_ROPE // 2
    valid = lane < MLA_ROPE
    cos_ref[...] = jnp.where(valid, jnp.cos(ang), 0.0)
    sin_ref[...] = jnp.where(valid, jnp.where(lane < half, -jnp.sin(ang), jnp.sin(ang)), 0.0)


def _rope_tables(positions):
    t = positions.size
    tm = _tile(t, ROW_TILE)
    half = MLA_ROPE // 2
    inv_freq = ROPE_THETA ** (-jnp.arange(0, MLA_ROPE, 2, dtype=F32) / MLA_ROPE)
    freq_row = jnp.concatenate([inv_freq, inv_freq, jnp.zeros((LANES - 2 * half,), F32)])[None, :]
    pos = positions.reshape(t, 1).astype(F32)
    out = jax.ShapeDtypeStruct((t, LANES), F32)
    return pl.pallas_call(
        _rope_table_kernel,
        grid=(t // tm,),
        in_specs=[pl.BlockSpec((tm, 1), lambda i: (i, 0)), pl.BlockSpec((1, LANES), lambda i: (0, 0))],
        out_specs=[pl.BlockSpec((tm, LANES), lambda i: (i, 0))] * 2,
        out_shape=[out, out],
        compiler_params=_params("parallel"),
        name="rope_tables",
    )(pos, freq_row)


def _norm_kernel(h_ref, g_ref, a_ref):
    a_ref[...] = _rms(h_ref[...], g_ref[...]).astype(a_ref.dtype)


def _norm(h, gain, out_dtype):
    t, d = h.shape
    tm = _tile(t, ROUTE_TILE)
    return pl.pallas_call(
        _norm_kernel,
        grid=(t // tm,),
        in_specs=[pl.BlockSpec((tm, d), lambda i: (i, 0)), pl.BlockSpec((1, d), lambda i: (0, 0))],
        out_specs=pl.BlockSpec((tm, d), lambda i: (i, 0)),
        out_shape=jax.ShapeDtypeStruct((t, d), out_dtype),
        compiler_params=_params("parallel"),
        name="rmsnorm",
    )(h, gain.reshape(1, d))


def _mm_kernel(x_ref, w_ref, o_ref, *, act):
    y = _dot(x_ref[...], w_ref[...])
    if act == "sigmoid":
        y = jax.nn.sigmoid(y)
    o_ref[...] = y.astype(o_ref.dtype)


def _mm(x, w, out_dtype, act=None, name="mm"):
    m, k = x.shape
    n = w.shape[1]
    tm, tn = _tile(m, ROW_TILE), _tile(n, COL_TILE)
    return pl.pallas_call(
        functools.partial(_mm_kernel, act=act),
        grid=(m // tm, n // tn),
        in_specs=[pl.BlockSpec((tm, k), lambda i, j: (i, 0)), pl.BlockSpec((k, tn), lambda i, j: (0, j))],
        out_specs=pl.BlockSpec((tm, tn), lambda i, j: (i, j)),
        out_shape=jax.ShapeDtypeStruct((m, n), out_dtype),
        compiler_params=_params("parallel", "parallel"),
        name=name,
    )(x, w)


def _mm_nt_kernel(wt_ref, x_ref, o_ref):
    o_ref[...] = _nt_dot(wt_ref[...], x_ref[...]).astype(o_ref.dtype)


def _mm_nt(w_t, x, out_dtype, name):
    n, k = w_t.shape
    m = x.shape[0]
    tm, tn = _tile(m, ROW_TILE), _tile(n, COL_TILE)
    return pl.pallas_call(
        _mm_nt_kernel,
        grid=(m // tm, n // tn),
        in_specs=[pl.BlockSpec((tn, k), lambda i, j: (j, 0)), pl.BlockSpec((tm, k), lambda i, j: (i, 0))],
        out_specs=pl.BlockSpec((tn, tm), lambda i, j: (j, i)),
        out_shape=jax.ShapeDtypeStruct((n, m), out_dtype),
        compiler_params=_params("parallel", "parallel"),
        name=name,
    )(w_t, x)


def _mm_res_kernel(x_ref, w_ref, h_ref, o_ref):
    o_ref[...] = h_ref[...] + _dot(x_ref[...], w_ref[...])


def _mm_residual(x, w, h):
    m, k = x.shape
    n = w.shape[1]
    tm, tn = _tile(m, ROW_TILE), _tile(n, COL_TILE)
    return pl.pallas_call(
        _mm_res_kernel,
        grid=(m // tm, n // tn),
        in_specs=[pl.BlockSpec((tm, k), lambda i, j: (i, 0)), pl.BlockSpec((k, tn), lambda i, j: (0, j)),
                  pl.BlockSpec((tm, tn), lambda i, j: (i, j))],
        out_specs=pl.BlockSpec((tm, tn), lambda i, j: (i, j)),
        out_shape=jax.ShapeDtypeStruct((m, n), F32),
        compiler_params=_params("parallel", "parallel"),
        name="out_proj_residual",
    )(x, w, h)


def _mla_q_kernel(c_ref, g_ref, w_ref, cos_ref, sin_ref, q_ref, *, scale):
    cq = _rms(c_ref[...], g_ref[...]).astype(BF16)
    y = _dot(cq, w_ref[...]) * scale
    cos_t, sin_t = cos_ref[...], sin_ref[...]
    for hd in range(y.shape[1] // MLA_QK_PAD):
        base = hd * MLA_QK_PAD
        q_ref[:, base:base + MLA_NOPE] = y[:, base:base + MLA_NOPE].astype(q_ref.dtype)
        roped = _rope_lanes(y[:, base + MLA_NOPE:base + MLA_QK_PAD], cos_t, sin_t)
        q_ref[:, base + MLA_NOPE:base + MLA_QK_PAD] = roped.astype(q_ref.dtype)


def _mla_kv_kernel(c_ref, pe_ref, g_ref, wk_ref, wvt_ref, cos_ref, sin_ref, k_ref, vt_ref):
    ckv = _rms(c_ref[...], g_ref[...]).astype(BF16)
    kn = _dot(ckv, wk_ref[...])
    vt_ref[...] = _nt_dot(wvt_ref[...], ckv).astype(vt_ref.dtype)
    k_pe = _rope_lanes(pe_ref[...], cos_ref[...], sin_ref[...]).astype(k_ref.dtype)
    for hd in range(kn.shape[1] // MLA_NOPE):
        k_ref[:, hd * MLA_QK_PAD:hd * MLA_QK_PAD + MLA_NOPE] = (
            kn[:, hd * MLA_NOPE:(hd + 1) * MLA_NOPE].astype(k_ref.dtype))
        k_ref[:, hd * MLA_QK_PAD + MLA_NOPE:(hd + 1) * MLA_QK_PAD] = k_pe


def _mla_project(a, cos_t, sin_t, w_dq, q_norm, w_uq, w_dkv, kv_norm, w_ukv):
    t, d = a.shape
    q_lora = w_dq.shape[1]
    kv_lora = w_dkv.shape[1] - MLA_ROPE
    heads = w_uq.shape[1] // (MLA_NOPE + MLA_ROPE)
    assert q_lora % LANES == 0 and kv_lora % LANES == 0

    w_down = jnp.concatenate(
        [w_dq, w_dkv, jnp.zeros((d, LANES - MLA_ROPE), w_dkv.dtype)], axis=1).astype(BF16)
    s1 = _mm(a, w_down, F32, name="mla_down")

    w_q = w_uq.reshape(q_lora, heads, MLA_NOPE + MLA_ROPE)
    w_q = jnp.pad(w_q, ((0, 0), (0, 0), (0, MLA_QK_PAD - MLA_NOPE - MLA_ROPE)))
    w_q = w_q.reshape(q_lora, heads * MLA_QK_PAD).astype(BF16)
    w_kv = w_ukv.reshape(kv_lora, heads, MLA_NOPE + MLA_V)
    w_k = w_kv[:, :, :MLA_NOPE].reshape(kv_lora, heads * MLA_NOPE).astype(BF16)
    w_vt = w_kv[:, :, MLA_NOPE:].reshape(kv_lora, heads * MLA_V).T.astype(BF16)

    tm = _tile(t, ROW_TILE)
    scale = (MLA_NOPE + MLA_ROPE) ** -0.5 * LOG2_E
    tn = _tile(heads * MLA_QK_PAD, COL_TILE)
    row128 = pl.BlockSpec((tm, LANES), lambda i, j: (i, 0))
    q = pl.pallas_call(
        functools.partial(_mla_q_kernel, scale=scale),
        grid=(t // tm, heads * MLA_QK_PAD // tn),
        in_specs=[pl.BlockSpec((tm, q_lora), lambda i, j: (i, 0)),
                  pl.BlockSpec((1, q_lora), lambda i, j: (0, 0)),
                  pl.BlockSpec((q_lora, tn), lambda i, j: (0, j)), row128, row128],
        out_specs=pl.BlockSpec((tm, tn), lambda i, j: (i, j)),
        out_shape=jax.ShapeDtypeStruct((t, heads * MLA_QK_PAD), BF16),
        compiler_params=_params("parallel", "parallel"),
        name="mla_q",
    )(s1, q_norm.reshape(1, q_lora), w_q, cos_t, sin_t)

    hb = min(heads, COL_TILE // MLA_QK_PAD)
    kv_blk = q_lora // kv_lora
    assert q_lora % kv_lora == 0
    pe_blk = (q_lora + kv_lora) // LANES
    k, v_t = pl.pallas_call(
        _mla_kv_kernel,
        grid=(t // tm, heads // hb),
        in_specs=[pl.BlockSpec((tm, kv_lora), lambda i, j: (i, kv_blk)),
                  pl.BlockSpec((tm, LANES), lambda i, j: (i, pe_blk)),
                  pl.BlockSpec((1, kv_lora), lambda i, j: (0, 0)),
                  pl.BlockSpec((kv_lora, hb * MLA_NOPE), lambda i, j: (0, j)),
                  pl.BlockSpec((hb * MLA_V, kv_lora), lambda i, j: (j, 0)), row128, row128],
        out_specs=[pl.BlockSpec((tm, hb * MLA_QK_PAD), lambda i, j: (i, j)),
                   pl.BlockSpec((hb * MLA_V, tm), lambda i, j: (j, i))],
        out_shape=[jax.ShapeDtypeStruct((t, heads * MLA_QK_PAD), BF16),
                   jax.ShapeDtypeStruct((heads * MLA_V, t), BF16)],
        compiler_params=_params("parallel", "parallel"),
        name="mla_kv",
    )(s1, s1, kv_norm.reshape(1, kv_lora), w_k, w_vt, cos_t, sin_t)
    return q, k, v_t


def _softmax_step_t(s_t, v_t, m_sc, l_sc, acc_sc, col_shift=None):
    m_prev = m_sc[...]
    m_cur = jnp.max(s_t, axis=0, keepdims=True)
    if col_shift is not None:
        m_cur = m_cur + col_shift
    m_new = jnp.maximum(m_prev, m_cur)
    shift = m_new if col_shift is None else m_new - col_shift
    p_t = jnp.exp2(s_t - shift)
    alpha = jnp.exp2(m_prev - m_new)
    l_sc[...] = alpha * l_sc[...] + jnp.sum(p_t, axis=0, keepdims=True)
    acc_sc[...] = alpha * acc_sc[...] + _dot(v_t, p_t.astype(v_t.dtype))
    m_sc[...] = m_new


def _init_softmax(m_sc, l_sc, acc_sc):
    m_sc[...] = jnp.full(m_sc.shape, NEG_INF, F32)
    l_sc[...] = jnp.zeros(l_sc.shape, F32)
    acc_sc[...] = jnp.zeros(acc_sc.shape, F32)


def _sweep_below_diagonal(qi, step):
    def pair(kp, carry):
        step(2 * kp)
        step(2 * kp + 1)
        return carry

    lax.fori_loop(0, lax.shift_right_logical(qi, 1), pair, 0)

    @pl.when((qi & 1) == 1)
    def _():
        step(qi - 1)


def _mla_attn_kernel(q_ref, k_ref, vt_ref, o_ref, m_sc, l_sc, acc_sc, *, tq, chunk_shift):
    qi = pl.program_id(2)
    q = q_ref[...]
    _init_softmax(m_sc, l_sc, acc_sc)

    def scores(kt):
        k0 = pl.multiple_of(kt * tq, tq)
        return _nt_dot(k_ref[pl.ds(k0, tq), :], q), vt_ref[:, pl.ds(k0, tq)]

    def below_diag(kt):
        s_t, v_t = scores(kt)
        _softmax_step_t(s_t, v_t, m_sc, l_sc, acc_sc)

    _sweep_below_diagonal(qi, below_diag)

    s_t, v_t = scores(qi)
    key = lax.broadcasted_iota(I32, s_t.shape, 0)
    qry = lax.broadcasted_iota(I32, s_t.shape, 1)
    allowed = lax.shift_right_logical(key, chunk_shift) <= lax.shift_right_logical(qry, chunk_shift)
    _softmax_step_t(jnp.where(allowed, s_t, NEG_INF), v_t, m_sc, l_sc, acc_sc)
    o_ref[...] = (acc_sc[...] / l_sc[...]).T.astype(o_ref.dtype)


def _mla_attention(q, k, v_t, batch, seq, heads):
    tq = _tile(seq, ATTN_TILE)
    assert tq % CHUNK == 0 and CHUNK & (CHUNK - 1) == 0
    nq = seq // tq
    return pl.pallas_call(
        functools.partial(_mla_attn_kernel, tq=tq, chunk_shift=CHUNK.bit_length() - 1),
        grid=(batch, heads, nq),
        in_specs=[pl.BlockSpec((tq, MLA_QK_PAD), lambda b, h, i: (b * nq + i, h)),
                  pl.BlockSpec((seq, MLA_QK_PAD), lambda b, h, i: (b, h)),
                  pl.BlockSpec((MLA_V, seq), lambda b, h, i: (h, b))],
        out_specs=pl.BlockSpec((tq, MLA_V), lambda b, h, i: (b * nq + i, h)),
        out_shape=jax.ShapeDtypeStruct((batch * seq, heads * MLA_V), BF16),
        scratch_shapes=[pltpu.VMEM((1, tq), F32), pltpu.VMEM((1, tq), F32), pltpu.VMEM((MLA_V, tq), F32)],
        compiler_params=_params("parallel", "parallel", "arbitrary"),
        name="mla_attention",
    )(q, k, v_t)


def _fox_attn_kernel(q_ref, k_ref, vt_ref, c_ref, g_ref, o_ref, m_sc, l_sc, acc_sc, ck_sc, *, tq):
    qi = pl.program_id(2)
    q = q_ref[...]
    _init_softmax(m_sc, l_sc, acc_sc)
    seq = ck_sc.shape[0]

    @pl.when(qi == 0)
    def _():
        def fill(blk, carry):
            b0 = pl.multiple_of(blk * LANES, LANES)
            row = c_ref[0, 0, :, pl.ds(b0, LANES)] * LOG2_E
            ck_sc[pl.ds(b0, LANES), :] = jnp.broadcast_to(row, (LANES, LANES)).T
            return carry

        lax.fori_loop(0, seq // LANES, fill, 0)

    q0 = pl.multiple_of(qi * tq, tq)
    c_q = c_ref[0, 0, :, pl.ds(q0, tq)] * LOG2_E

    def scores(kt):
        k0 = pl.multiple_of(kt * tq, tq)
        c_k = ck_sc[pl.ds(k0, tq), :]
        s_t = _nt_dot(k_ref[pl.ds(k0, tq), :], q) - jnp.concatenate([c_k] * (tq // LANES), axis=1)
        return s_t, vt_ref[:, pl.ds(k0, tq)]

    def below_diag(kt):
        s_t, v_t = scores(kt)
        _softmax_step_t(s_t, v_t, m_sc, l_sc, acc_sc, col_shift=c_q)

    _sweep_below_diagonal(qi, below_diag)

    s_t, v_t = scores(qi)
    key = lax.broadcasted_iota(I32, s_t.shape, 0)
    qry = lax.broadcasted_iota(I32, s_t.shape, 1)
    _softmax_step_t(jnp.where(key <= qry, s_t, NEG_INF), v_t, m_sc, l_sc, acc_sc, col_shift=c_q)
    o_ref[...] = ((acc_sc[...] / l_sc[...]).T * g_ref[...].astype(F32)).astype(o_ref.dtype)


def _fox_attention(qk, v_t, c_rows, gate, batch, seq, heads):
    tq = _tile(seq, ATTN_TILE)
    assert tq % LANES == 0 or tq == seq
    nq = seq // tq
    dh = FOX_HEAD_DIM
    return pl.pallas_call(
        functools.partial(_fox_attn_kernel, tq=tq),
        grid=(batch, heads, nq),
        in_specs=[pl.BlockSpec((tq, dh), lambda b, h, i: (b * nq + i, h)),
                  pl.BlockSpec((seq, dh), lambda b, h, i: (b, heads + h)),
                  pl.BlockSpec((dh, seq), lambda b, h, i: (h, b)),
                  pl.BlockSpec((1, 1, 1, seq), lambda b, h, i: (b, h, 0, 0)),
                  pl.BlockSpec((tq, dh), lambda b, h, i: (b * nq + i, h))],
        out_specs=pl.BlockSpec((tq, dh), lambda b, h, i: (b * nq + i, h)),
        out_shape=jax.ShapeDtypeStruct((batch * seq, heads * dh), BF16),
        scratch_shapes=[pltpu.VMEM((1, tq), F32), pltpu.VMEM((1, tq), F32), pltpu.VMEM((dh, tq), F32),
                        pltpu.VMEM((seq, LANES), F32)],
        compiler_params=_params("parallel", "parallel", "arbitrary"),
        name="fox_attention",
    )(qk, qk, v_t, c_rows, gate)


def _fox_qk_kernel(x_ref, w_ref, g_ref, o_ref):
    y = _dot(x_ref[...], w_ref[...])
    g = g_ref[...]
    for hd in range(y.shape[1] // FOX_HEAD_DIM):
        sl = slice(hd * FOX_HEAD_DIM, (hd + 1) * FOX_HEAD_DIM)
        o_ref[:, sl] = _rms(y[:, sl], g[:, sl]).astype(o_ref.dtype)


def _fox_qk(a, w_qk, gain_row):
    t, d = a.shape
    n = w_qk.shape[1]
    tm, tn = _tile(t, ROW_TILE), _tile(n, COL_TILE)
    return pl.pallas_call(
        _fox_qk_kernel,
        grid=(t // tm, n // tn),
        in_specs=[pl.BlockSpec((tm, d), lambda i, j: (i, 0)), pl.BlockSpec((d, tn), lambda i, j: (0, j)),
                  pl.BlockSpec((1, tn), lambda i, j: (0, j))],
        out_specs=pl.BlockSpec((tm, tn), lambda i, j: (i, j)),
        out_shape=jax.ShapeDtypeStruct((t, n), BF16),
        compiler_params=_params("parallel", "parallel"),
        name="fox_qk",
    )(a, w_qk, gain_row)


def _fox_forget_kernel(h_ref, g_ref, w1_ref, w2_ref, w3_ref, b_ref, c_ref, carry_sc, *, tiles_per_seq):
    i = pl.program_id(0)

    @pl.when(i % tiles_per_seq == 0)
    def _():
        carry_sc[...] = jnp.zeros(carry_sc.shape, F32)

    a1, a2, a3 = _split3(_rms(h_ref[...], g_ref[...]))
    w1, w2 = w1_ref[...], w2_ref[...]
    z = (_dot(a1, w1) + (_dot(a1, w2) + _dot(a2, w1))
         + (_dot(a1, w3_ref[...]) + _dot(a2, w2) + _dot(a3, w1))) + b_ref[...]
    log_f = jnp.minimum(z, 0.0) - jnp.log(1.0 + jnp.exp(-jnp.abs(z)))

    tm = log_f.shape[0]
    row = lax.broadcasted_iota(I32, (tm, tm), 0)
    col = lax.broadcasted_iota(I32, (tm, tm), 1)
    tri = (col <= row).astype(BF16)
    f1, f2, f3 = _split3(log_f)
    c = (_dot(tri, f1) + _dot(tri, f2) + _dot(tri, f3)) + carry_sc[...]
    carry_sc[...] = c[tm - 1:tm, :]
    c_ref[0] = c.T


def _fox_forget_cumsum(h, gain, w_f, b_f, batch, seq):
    t, d = h.shape
    heads = w_f.shape[1]
    assert heads <= LANES
    tm = _tile(seq, ROUTE_TILE)
    w = jnp.pad(w_f, ((0, 0), (0, LANES - heads)))
    w1, w2, w3 = _split3(w)
    b = jnp.pad(b_f, (0, LANES - heads)).reshape(1, LANES)
    wspec = pl.BlockSpec((d, LANES), lambda i: (0, 0))
    tps = seq // tm
    return pl.pallas_call(
        functools.partial(_fox_forget_kernel, tiles_per_seq=tps),
        grid=(t // tm,),
        in_specs=[pl.BlockSpec((tm, d), lambda i: (i, 0)), pl.BlockSpec((1, d), lambda i: (0, 0)),
                  wspec, wspec, wspec, pl.BlockSpec((1, LANES), lambda i: (0, 0))],
        out_specs=pl.BlockSpec((1, LANES, tm), lambda i: (i // tps, 0, i % tps)),
        out_shape=jax.ShapeDtypeStruct((batch, LANES, seq), F32),
        scratch_shapes=[pltpu.VMEM((1, LANES), F32)],
        compiler_params=_params("arbitrary"),
        name="fox_forget_cumsum",
    )(h, gain.reshape(1, d), w1, w2, w3, b)


def _router_kernel(h_ref, g_ref, w1_ref, w2_ref, w3_ref, b_ref, t_ref, meta_ref, cnt_ref, carry_sc):
    i = pl.program_id(0)

    @pl.when(i == 0)
    def _():
        carry_sc[...] = jnp.zeros(carry_sc.shape, F32)

    t = _rms(h_ref[...], g_ref[...])
    t_ref[...] = _pack_halves(t)

    a1, a2, a3 = _split3(t)
    w1, w2 = w1_ref[...], w2_ref[...]
    logits = (_dot(a1, w1) + (_dot(a1, w2) + _dot(a2, w1))
              + (_dot(a1, w3_ref[...]) + _dot(a2, w2) + _dot(a3, w1))) + b_ref[...]
    tm = logits.shape[0]
    lane = lax.broadcasted_iota(I32, logits.shape, 1)
    lane_f = lane.astype(F32)
    first = lambda hit: jnp.min(jnp.where(hit, lane_f, float(LANES)), axis=1, keepdims=True).astype(I32)

    is_grp = lane < MOE_GROUPS
    gl = jnp.where(is_grp, logits, -jnp.inf)
    gmax = jnp.max(gl, axis=1, keepdims=True)
    g_sel = first(gl == gmax)
    gexp = jnp.where(is_grp, jnp.exp(logits - gmax), 0.0)
    g_w = 1.0 / jnp.sum(gexp, axis=1, keepdims=True)

    lo = MOE_GROUPS + MOE_EXPERTS_PER_GROUP * g_sel
    in_grp = jnp.logical_and(lane >= lo, lane < lo + MOE_EXPERTS_PER_GROUP)
    el = jnp.where(in_grp, logits, -jnp.inf)
    emax = jnp.max(el, axis=1, keepdims=True)
    eexp = jnp.where(in_grp, jnp.exp(logits - emax), 0.0)
    prob = eexp / jnp.sum(eexp, axis=1, keepdims=True)
    cand1 = jnp.where(in_grp, prob, -1.0)
    p1 = jnp.max(cand1, axis=1, keepdims=True)
    j1 = first(cand1 == p1)
    cand2 = jnp.where(lane == j1, -1.0, cand1)
    p2 = jnp.max(cand2, axis=1, keepdims=True)
    j2 = first(cand2 == p2)
    denom = p1 + p2
    wt1 = p1 / denom * g_w
    wt2 = p2 / denom * g_w
    e1 = j1 - MOE_GROUPS
    e2 = j2 - MOE_GROUPS

    hit1 = lane == e1
    hit2 = lane == e2
    row = lax.broadcasted_iota(I32, (tm, tm), 0)
    col = lax.broadcasted_iota(I32, (tm, tm), 1)
    before = (col < row).astype(BF16)
    pre1 = _dot(before, hit1.astype(BF16))
    pre2 = _dot(before, hit2.astype(BF16))
    carry = carry_sc[...]
    cnt1 = jnp.sum(hit1.astype(F32), axis=0, keepdims=True)
    cnt2 = jnp.sum(hit2.astype(F32), axis=0, keepdims=True)
    rank1 = jnp.sum(jnp.where(hit1, pre1 + carry, 0.0), axis=1, keepdims=True)
    rank2 = jnp.sum(jnp.where(hit2, pre2 + (carry + cnt1), 0.0), axis=1, keepdims=True)
    total = carry + cnt1 + cnt2
    carry_sc[...] = total
    cnt_ref[...] = jnp.broadcast_to(total, cnt_ref.shape)

    bits = lambda x: lax.bitcast_convert_type(jnp.broadcast_to(x, logits.shape), I32)
    meta = jnp.where(lane == 0, e1, 0)
    meta = jnp.where(lane == 1, e2, meta)
    meta = jnp.where(lane == 2, rank1.astype(I32), meta)
    meta = jnp.where(lane == 3, rank2.astype(I32), meta)
    meta = jnp.where(lane == 4, bits(wt1), meta)
    meta = jnp.where(lane == 5, bits(wt2), meta)
    meta_ref[...] = meta


def _router(h, gain, w_grp, b_grp, w_rt, b_rt):
    t, d = h.shape
    tm = _tile(t, ROUTE_TILE)
    n_used = MOE_GROUPS + MOE_EXPERTS
    w = jnp.pad(jnp.concatenate([w_grp, w_rt], axis=1), ((0, 0), (0, LANES - n_used)))
    w1, w2, w3 = _split3(w)
    b = jnp.pad(jnp.concatenate([b_grp, b_rt]), (0, LANES - n_used)).reshape(1, LANES)
    wspec = pl.BlockSpec((d, LANES), lambda i: (0, 0))
    return pl.pallas_call(
        _router_kernel,
        grid=(t // tm,),
        in_specs=[pl.BlockSpec((tm, d), lambda i: (i, 0)), pl.BlockSpec((1, d), lambda i: (0, 0)),
                  wspec, wspec, wspec, pl.BlockSpec((1, LANES), lambda i: (0, 0))],
        out_specs=[pl.BlockSpec((tm, d // 2), lambda i: (i, 0)),
                   pl.BlockSpec((tm, LANES), lambda i: (i, 0)),
                   pl.BlockSpec((SUBLANES, LANES), lambda i: (0, 0))],
        out_shape=[jax.ShapeDtypeStruct((t, d // 2), U32),
                   jax.ShapeDtypeStruct((t, LANES), I32),
                   jax.ShapeDtypeStruct((SUBLANES, LANES), F32)],
        scratch_shapes=[pltpu.VMEM((1, LANES), F32)],
        compiler_params=_params("arbitrary"),
        name="moe_router",
    )(h, gain.reshape(1, d), w1, w2, w3, b)


def _row_copy(src, src_row, dst, dst_row, sem):
    return pltpu.make_async_copy(src.at[pl.ds(src_row, 1)], dst.at[pl.ds(dst_row, 1)], sem)


_ROW_UNROLL = 8


def _dispatch_kernel(zs_ref, nv_ref, d1_ref, d2_ref, t_ref, xs_ref, zero_sc, sem, *, tm, tz):
    i = pl.program_id(0)

    @pl.when(i == 0)
    def _():
        zero_sc[...] = jnp.zeros(zero_sc.shape, zero_sc.dtype)
        fill = lambda row0: pltpu.make_async_copy(zero_sc, xs_ref.at[pl.ds(pl.multiple_of(row0, tz), tz)], sem)
        fills = [fill(zs_ref[e]) for e in range(MOE_EXPERTS)]
        for c in fills:
            c.start()
        for c in fills:
            c.wait()

        def tail(tile, carry):
            c = fill(tile * tz)
            c.start()
            c.wait()
            return carry

        lax.fori_loop(nv_ref[0], xs_ref.shape[0] // tz, tail, 0)

    def issue(blk, carry):
        for u in range(_ROW_UNROLL):
            r = blk * _ROW_UNROLL + u
            _row_copy(t_ref, r, xs_ref, d1_ref[0, 0, r], sem).start()
            _row_copy(t_ref, r, xs_ref, d2_ref[0, 0, r], sem).start()
        return carry

    lax.fori_loop(0, tm // _ROW_UNROLL, issue, 0)

    def drain(blk, carry):
        for u in range(2 * _ROW_UNROLL):
            _row_copy(t_ref, 0, xs_ref, 0, sem).wait()
        return carry

    lax.fori_loop(0, tm // _ROW_UNROLL, drain, 0)


def _dispatch(t_packed, dest1, dest2, zero_start, n_valid, n_rows):
    t, d2 = t_packed.shape
    tm = _tile(t, MOVE_TILE)
    tz = EXPERT_TILE
    dspec = pl.BlockSpec((1, 1, tm), lambda i, zs, nv: (i, 0, 0), memory_space=pltpu.SMEM)
    return pl.pallas_call(
        functools.partial(_dispatch_kernel, tm=tm, tz=tz),
        grid_spec=pltpu.PrefetchScalarGridSpec(
            num_scalar_prefetch=2,
            grid=(t // tm,),
            in_specs=[dspec, dspec, pl.BlockSpec((tm, d2), lambda i, zs, nv: (i, 0))],
            out_specs=pl.BlockSpec(memory_space=pl.ANY),
            scratch_shapes=[pltpu.VMEM((tz, d2), U32), pltpu.SemaphoreType.DMA(())],
        ),
        out_shape=jax.ShapeDtypeStruct((n_rows, d2), U32),
        compiler_params=_params("arbitrary"),
        name="moe_dispatch",
    )(zero_start, n_valid, dest1.reshape(t // tm, 1, tm), dest2.reshape(t // tm, 1, tm), t_packed)


def _experts_kernel(te_ref, nv_ref, x_ref, wg_ref, wu_ref, wd_ref, y_ref):
    live = pl.program_id(0) < nv_ref[0]

    @pl.when(jnp.logical_not(live))
    def _():
        y_ref[...] = jnp.zeros(y_ref.shape, y_ref.dtype)

    @pl.when(live)
    def _():
        hi, lo = _unpack_halves(x_ref[...])
        x = jnp.concatenate([hi, lo], axis=1).astype(BF16)
        g = _dot(x, wg_ref[0])
        u = _dot(x, wu_ref[0])
        hid = (g * jax.nn.sigmoid(g) * u).astype(BF16)
        y_ref[...] = _pack_halves(_dot(hid, wd_ref[0]))


def _experts(xs, tile_expert, n_valid, w_gate, w_up, w_down, n_tiles):
    tm = EXPERT_TILE
    d2 = xs.shape[1]
    _, d, f = w_gate.shape
    live = lambda i, nv: jnp.minimum(i, nv[0] - 1)
    return pl.pallas_call(
        _experts_kernel,
        grid_spec=pltpu.PrefetchScalarGridSpec(
            num_scalar_prefetch=2,
            grid=(n_tiles,),
            in_specs=[pl.BlockSpec((tm, d2), lambda i, te, nv: (live(i, nv), 0)),
                      pl.BlockSpec((1, d, f), lambda i, te, nv: (te[i], 0, 0)),
                      pl.BlockSpec((1, d, f), lambda i, te, nv: (te[i], 0, 0)),
                      pl.BlockSpec((1, f, d), lambda i, te, nv: (te[i], 0, 0))],
            out_specs=pl.BlockSpec((tm, d2), lambda i, te, nv: (i, 0)),
        ),
        out_shape=jax.ShapeDtypeStruct((n_tiles * tm, d2), U32),
        compiler_params=_params("arbitrary"),
        name="moe_experts",
    )(tile_expert, n_valid, xs, w_gate, w_up, w_down)


def _combine_kernel(d1_ref, d2_ref, meta_ref, h_ref, g_ref, ys_ref, *rest, tm, final):
    if final:
        o_ref, y1_sc, y2_sc, sem = rest
    else:
        hn_ref, a_ref, y1_sc, y2_sc, sem = rest

    def issue(blk, carry):
        for u in range(_ROW_UNROLL):
            r = blk * _ROW_UNROLL + u
            _row_copy(ys_ref, d1_ref[0, 0, r], y1_sc, r, sem).start()
            _row_copy(ys_ref, d2_ref[0, 0, r], y2_sc, r, sem).start()
        return carry

    lax.fori_loop(0, tm // _ROW_UNROLL, issue, 0)

    def drain(blk, carry):
        for u in range(2 * _ROW_UNROLL):
            _row_copy(ys_ref, 0, y1_sc, 0, sem).wait()
        return carry

    lax.fori_loop(0, tm // _ROW_UNROLL, drain, 0)

    meta = meta_ref[...]
    lane = lax.broadcasted_iota(I32, meta.shape, 1)
    wbits = lax.bitcast_convert_type(meta, F32)
    wt1 = jnp.sum(jnp.where(lane == 4, wbits, 0.0), axis=1, keepdims=True)
    wt2 = jnp.sum(jnp.where(lane == 5, wbits, 0.0), axis=1, keepdims=True)
    hi1, lo1 = _unpack_halves(y1_sc[...])
    hi2, lo2 = _unpack_halves(y2_sc[...])
    moe = jnp.concatenate([wt1 * hi1 + wt2 * hi2, wt1 * lo1 + wt2 * lo2], axis=1)
    h_new = h_ref[...] + moe
    if final:
        o_ref[...] = _rms(h_new, g_ref[...])
    else:
        hn_ref[...] = h_new
        a_ref[...] = _rms(h_new, g_ref[...]).astype(a_ref.dtype)


def _combine(ys, dest1, dest2, meta, h, gain, final):
    t, d = h.shape
    tm = _tile(t, MOVE_TILE)
    d2 = d // 2
    dspec = pl.BlockSpec((1, 1, tm), lambda i: (i, 0, 0), memory_space=pltpu.SMEM)
    row = pl.BlockSpec((tm, d), lambda i: (i, 0))
    if final:
        out_specs, out_shape = row, jax.ShapeDtypeStruct((t, d), F32)
    else:
        out_specs = [row, row]
        out_shape = [jax.ShapeDtypeStruct((t, d), F32), jax.ShapeDtypeStruct((t, d), BF16)]
    return pl.pallas_call(
        functools.partial(_combine_kernel, tm=tm, final=final),
        grid=(t // tm,),
        in_specs=[dspec, dspec, pl.BlockSpec((tm, LANES), lambda i: (i, 0)), row,
                  pl.BlockSpec((1, d), lambda i: (0, 0)), pl.BlockSpec(memory_space=pl.ANY)],
        out_specs=out_specs,
        out_shape=out_shape,
        scratch_shapes=[pltpu.VMEM((tm, d2), U32), pltpu.VMEM((tm, d2), U32), pltpu.SemaphoreType.DMA(())],
        compiler_params=_params("arbitrary"),
        name="moe_combine",
    )(dest1.reshape(t // tm, 1, tm), dest2.reshape(t // tm, 1, tm), meta, h, gain.reshape(1, d), ys)


def _hier_moe(h, ffn_gain, w_grp, b_grp, w_rt, b_rt, w_gate, w_up, w_down, next_gain, final):
    t, d = h.shape
    tmx = EXPERT_TILE
    t_packed, meta, cnt = _router(h, ffn_gain, w_grp, b_grp, w_rt, b_rt)

    counts = cnt[0, :MOE_EXPERTS].astype(I32)
    padded = (counts + tmx - 1) // tmx * tmx
    ends = jnp.cumsum(padded)
    offs = ends - padded
    dest1 = offs[meta[:, 0]] + meta[:, 2]
    dest2 = offs[meta[:, 1]] + meta[:, 3]
    n_tiles = (2 * t) // tmx + MOE_EXPERTS
    n_valid = (ends[-1] // tmx).reshape(1)
    tile_start = jnp.arange(n_tiles, dtype=I32) * tmx
    tile_expert = jnp.searchsorted(ends, jnp.minimum(tile_start, ends[-1] - 1), side="right").astype(I32)
    zero_start = offs + counts // tmx * tmx

    xs = _dispatch(t_packed, dest1, dest2, zero_start, n_valid, (n_tiles + 1) * tmx)
    ys = _experts(xs, tile_expert, n_valid, w_gate.astype(BF16), w_up.astype(BF16), w_down.astype(BF16),
                  n_tiles)
    return _combine(ys, dest1, dest2, meta, h, next_gain, final)


def kernel(x, positions, attn_norm, ffn_norm, final_norm, mla_w_dq, mla_q_norm, mla_w_uq, mla_w_dkv,
           mla_kv_norm, mla_w_ukv, mla_w_o, fox_w_qkv, fox_q_norm, fox_k_norm, fox_w_f, fox_b_f,
           fox_w_og, fox_w_o, moe_w_grp, moe_b_grp, moe_w_rt, moe_b_rt, moe_w_gate, moe_w_up, moe_w_down):
    batch, seq, d = x.shape
    depth = attn_norm.shape[0]
    t = batch * seq
    cos_t, sin_t = _rope_tables(positions)
    h = x.reshape(t, d)
    a = _norm(h, attn_norm[0], BF16)
    out = None
    for i in range(depth):
        j = i // 2
        if i % 2 == 0:
            heads = mla_w_uq.shape[2] // (MLA_NOPE + MLA_ROPE)
            q, k, v_t = _mla_project(a, cos_t, sin_t, mla_w_dq[j], mla_q_norm[j], mla_w_uq[j],
                                     mla_w_dkv[j], mla_kv_norm[j], mla_w_ukv[j])
            o = _mla_attention(q, k, v_t, batch, seq, heads)
            w_o = mla_w_o[j]
        else:
            dh = FOX_HEAD_DIM
            heads = fox_w_qkv.shape[2] // (3 * dh)
            w_qkv = fox_w_qkv[j].astype(BF16)
            gain_row = jnp.concatenate([jnp.tile(fox_q_norm[j] * (dh ** -0.5 * LOG2_E), heads),
                                        jnp.tile(fox_k_norm[j], heads)])[None, :]
            qk = _fox_qk(a, w_qkv[:, :2 * heads * dh], gain_row)
            v_t = _mm_nt(w_qkv[:, 2 * heads * dh:].T, a, BF16, name="fox_v")
            gate = _mm(a, fox_w_og[j].astype(BF16), BF16, act="sigmoid", name="fox_gate")
            c_t = _fox_forget_cumsum(h, attn_norm[i], fox_w_f[j], fox_b_f[j], batch, seq)
            c_rows = c_t[:, :heads, :].reshape(batch, heads, 1, seq)
            o = _fox_attention(qk, v_t, c_rows, gate, batch, seq, heads)
            w_o = fox_w_o[j]
        h = _mm_residual(o, w_o.astype(BF16), h)
        final = i == depth - 1
        next_gain = final_norm if final else attn_norm[i + 1]
        res = _hier_moe(h, ffn_norm[i], moe_w_grp[i], moe_b_grp[i], moe_w_rt[i], moe_b_rt[i],
                        moe_w_gate[i], moe_w_up[i], moe_w_down[i], next_gain, final)
        if final:
            out = res
        else:
            h, a = res
    return out.reshape(batch, seq, d)
```

```python
import functools

import jax
import jax.numpy as jnp
from jax import lax
from jax.experimental import pallas as pl
from jax.experimental.pallas import tpu as pltpu

F32 = jnp.float32
BF16 = jnp.bfloat16
I32 = jnp.int32
U32 = jnp.uint32

RMS_EPS = 1e-6
NEG_INF = -1e30
CHUNK = 64
MLA_NOPE = 128
MLA_ROPE = 64
MLA_V = 128
MLA_QK_PAD = 256
ROPE_THETA = 10000.0
LOG2_E = 1.4426950408889634
FOX_HEAD_DIM = 128
MOE_GROUPS = 8
MOE_EXPERTS_PER_GROUP = 4
MOE_EXPERTS = MOE_GROUPS * MOE_EXPERTS_PER_GROUP

LANES = 128
SUBLANES = 8
V7X_VMEM_LIMIT_BYTES = 56 * 1024 * 1024

ROW_TILE = 1024
COL_TILE = 1024
ATTN_TILE = 512
ROUTE_TILE = 512
MOVE_TILE = 256
EXPERT_TILE = 256


def _params(*sem):
    return pltpu.CompilerParams(dimension_semantics=sem, vmem_limit_bytes=V7X_VMEM_LIMIT_BYTES)


def _tile(n, t):
    if n <= t:
        return n
    step = LANES if t % LANES == 0 else SUBLANES
    for c in range(t - t % step, 0, -step):
        if n % c == 0:
            return c
    raise ValueError(f"no aligned tile for {n} under {t}")


def _rms(x, gain):
    ms = jnp.mean(x * x, axis=-1, keepdims=True)
    return x * lax.rsqrt(ms + RMS_EPS) * gain


def _dot(a, b):
    return jnp.dot(a, b, preferred_element_type=F32)


def _nt_dot(a, b):
    return lax.dot_general(a, b, (((1,), (1,)), ((), ())), preferred_element_type=F32)


def _pack_halves(x):
    n = x.shape[1] // 2
    hi = lax.bitcast_convert_type(x[:, :n].astype(BF16).astype(F32), U32)
    lo = lax.bitcast_convert_type(x[:, n:].astype(BF16).astype(F32), U32)
    return hi | lax.shift_right_logical(lo, jnp.uint32(16))


def _unpack_halves(p):
    hi = lax.bitcast_convert_type(p & jnp.uint32(0xFFFF0000), F32)
    lo = lax.bitcast_convert_type(lax.shift_left(p, jnp.uint32(16)), F32)
    return hi, lo


def _split(x, terms):
    out = []
    for _ in range(terms):
        hi = x.astype(BF16)
        out.append(hi)
        x = x - hi.astype(F32)
    return out


def _split_dot(x, w12_ref):
    m = x.shape[0]
    n = w12_ref.shape[1] // 2
    prod = _dot(jnp.concatenate(_split(x, 2), axis=0), w12_ref[...])
    return (prod[:m, :n] + (prod[:m, n:] + prod[m:, :n])) + prod[m:, n:]


def _rope_lanes(seg, cos_t, sin_t):
    half = MLA_ROPE // 2
    lane = lax.broadcasted_iota(I32, seg.shape, 1)
    swapped = jnp.where(lane < half, pltpu.roll(seg, LANES - half, 1), pltpu.roll(seg, half, 1))
    return seg * cos_t + swapped * sin_t


def _rope_table_kernel(pos_ref, freq_ref, cos_ref, sin_ref):
    ang = pos_ref[...] * freq_ref[...]
    lane = lax.broadcasted_iota(I32, ang.shape, 1)
    half = MLA_ROPE // 2
    valid = lane < MLA_ROPE
    cos_ref[...] = jnp.where(valid, jnp.cos(ang), 0.0)
    sin_ref[...] = jnp.where(valid, jnp.where(lane < half, -jnp.sin(ang), jnp.sin(ang)), 0.0)


def _rope_tables(positions):
    t = positions.size
    tm = _tile(t, ROW_TILE)
    half = MLA_ROPE // 2
    inv_freq = ROPE_THETA ** (-jnp.arange(0, MLA_ROPE, 2, dtype=F32) / MLA_ROPE)
    freq_row = jnp.concatenate([inv_freq, inv_freq, jnp.zeros((LANES - 2 * half,), F32)])[None, :]
    pos = positions.reshape(t, 1).astype(F32)
    out = jax.ShapeDtypeStruct((t, LANES), F32)
    return pl.pallas_call(
        _rope_table_kernel,
        grid=(t // tm,),
        in_specs=[pl.BlockSpec((tm, 1), lambda i: (i, 0)), pl.BlockSpec((1, LANES), lambda i: (0, 0))],
        out_specs=[pl.BlockSpec((tm, LANES), lambda i: (i, 0))] * 2,
        out_shape=[out, out],
        compiler_params=_params("parallel"),
        name="rope_tables",
    )(pos, freq_row)


def _norm_kernel(h_ref, g_ref, a_ref):
    a_ref[...] = _rms(h_ref[...], g_ref[...]).astype(a_ref.dtype)


def _norm(h, gain, out_dtype):
    t, d = h.shape
    tm = _tile(t, ROUTE_TILE)
    return pl.pallas_call(
        _norm_kernel,
        grid=(t // tm,),
        in_specs=[pl.BlockSpec((tm, d), lambda i: (i, 0)), pl.BlockSpec((1, d), lambda i: (0, 0))],
        out_specs=pl.BlockSpec((tm, d), lambda i: (i, 0)),
        out_shape=jax.ShapeDtypeStruct((t, d), out_dtype),
        compiler_params=_params("parallel"),
        name="rmsnorm",
    )(h, gain.reshape(1, d))


def _mm_kernel(x_ref, w_ref, o_ref, *, act):
    y = _dot(x_ref[...], w_ref[...])
    if act == "sigmoid":
        y = jax.nn.sigmoid(y)
    o_ref[...] = y.astype(o_ref.dtype)


def _mm(x, w, out_dtype, act=None, name="mm"):
    m, k = x.shape
    n = w.shape[1]
    tm, tn = _tile(m, ROW_TILE), _tile(n, COL_TILE)
    return pl.pallas_call(
        functools.partial(_mm_kernel, act=act),
        grid=(m // tm, n // tn),
        in_specs=[pl.BlockSpec((tm, k), lambda i, j: (i, 0)), pl.BlockSpec((k, tn), lambda i, j: (0, j))],
        out_specs=pl.BlockSpec((tm, tn), lambda i, j: (i, j)),
        out_shape=jax.ShapeDtypeStruct((m, n), out_dtype),
        compiler_params=_params("parallel", "parallel"),
        name=name,
    )(x, w)


def _mm_nt_kernel(wt_ref, x_ref, o_ref):
    o_ref[...] = _nt_dot(wt_ref[...], x_ref[...]).astype(o_ref.dtype)


def _mm_nt(w_t, x, out_dtype, name):
    n, k = w_t.shape
    m = x.shape[0]
    tm, tn = _tile(m, ROW_TILE), _tile(n, COL_TILE)
    return pl.pallas_call(
        _mm_nt_kernel,
        grid=(m // tm, n // tn),
        in_specs=[pl.BlockSpec((tn, k), lambda i, j: (j, 0)), pl.BlockSpec((tm, k), lambda i, j: (i, 0))],
        out_specs=pl.BlockSpec((tn, tm), lambda i, j: (j, i)),
        out_shape=jax.ShapeDtypeStruct((n, m), out_dtype),
        compiler_params=_params("parallel", "parallel"),
        name=name,
    )(w_t, x)


def _mm_res_kernel(x_ref, w_ref, h_ref, o_ref):
    o_ref[...] = h_ref[...] + _dot(x_ref[...], w_ref[...])


def _mm_residual(x, w, h):
    m, k = x.shape
    n = w.shape[1]
    tm, tn = _tile(m, ROW_TILE), _tile(n, COL_TILE)
    return pl.pallas_call(
        _mm_res_kernel,
        grid=(m // tm, n // tn),
        in_specs=[pl.BlockSpec((tm, k), lambda i, j: (i, 0)), pl.BlockSpec((k, tn), lambda i, j: (0, j)),
                  pl.BlockSpec((tm, tn), lambda i, j: (i, j))],
        out_specs=pl.BlockSpec((tm, tn), lambda i, j: (i, j)),
        out_shape=jax.ShapeDtypeStruct((m, n), F32),
        compiler_params=_params("parallel", "parallel"),
        name="out_proj_residual",
    )(x, w, h)


def _mla_q_kernel(c_ref, g_ref, w_ref, cos_ref, sin_ref, q_ref, *, scale):
    cq = _rms(c_ref[...], g_ref[...]).astype(BF16)
    y = _dot(cq, w_ref[...]) * scale
    cos_t, sin_t = cos_ref[...], sin_ref[...]
    for hd in range(y.shape[1] // MLA_QK_PAD):
        base = hd * MLA_QK_PAD
        q_ref[:, base:base + MLA_NOPE] = y[:, base:base + MLA_NOPE].astype(q_ref.dtype)
        roped = _rope_lanes(y[:, base + MLA_NOPE:base + MLA_QK_PAD], cos_t, sin_t)
        q_ref[:, base + MLA_NOPE:base + MLA_QK_PAD] = roped.astype(q_ref.dtype)


def _mla_kv_kernel(c_ref, pe_ref, g_ref, wk_ref, wvt_ref, cos_ref, sin_ref, k_ref, vt_ref):
    ckv = _rms(c_ref[...], g_ref[...]).astype(BF16)
    kn = _dot(ckv, wk_ref[...])
    vt_ref[...] = _nt_dot(wvt_ref[...], ckv).astype(vt_ref.dtype)
    k_pe = _rope_lanes(pe_ref[...], cos_ref[...], sin_ref[...]).astype(k_ref.dtype)
    for hd in range(kn.shape[1] // MLA_NOPE):
        k_ref[:, hd * MLA_QK_PAD:hd * MLA_QK_PAD + MLA_NOPE] = (
            kn[:, hd * MLA_NOPE:(hd + 1) * MLA_NOPE].astype(k_ref.dtype))
        k_ref[:, hd * MLA_QK_PAD + MLA_NOPE:(hd + 1) * MLA_QK_PAD] = k_pe


def _mla_project(a, cos_t, sin_t, w_dq, q_norm, w_uq, w_dkv, kv_norm, w_ukv):
    t, d = a.shape
    q_lora = w_dq.shape[1]
    kv_lora = w_dkv.shape[1] - MLA_ROPE
    heads = w_uq.shape[1] // (MLA_NOPE + MLA_ROPE)
    assert q_lora % LANES == 0 and kv_lora % LANES == 0

    w_down = jnp.concatenate(
        [w_dq, w_dkv, jnp.zeros((d, LANES - MLA_ROPE), w_dkv.dtype)], axis=1).astype(BF16)
    s1 = _mm(a, w_down, F32, name="mla_down")

    w_q = w_uq.reshape(q_lora, heads, MLA_NOPE + MLA_ROPE)
    w_q = jnp.pad(w_q, ((0, 0), (0, 0), (0, MLA_QK_PAD - MLA_NOPE - MLA_ROPE)))
    w_q = w_q.reshape(q_lora, heads * MLA_QK_PAD).astype(BF16)
    w_kv = w_ukv.reshape(kv_lora, heads, MLA_NOPE + MLA_V)
    w_k = w_kv[:, :, :MLA_NOPE].reshape(kv_lora, heads * MLA_NOPE).astype(BF16)
    w_vt = w_kv[:, :, MLA_NOPE:].reshape(kv_lora, heads * MLA_V).T.astype(BF16)

    tm = _tile(t, ROW_TILE)
    scale = (MLA_NOPE + MLA_ROPE) ** -0.5 * LOG2_E
    tn = _tile(heads * MLA_QK_PAD, COL_TILE)
    row128 = pl.BlockSpec((tm, LANES), lambda i, j: (i, 0))
    q = pl.pallas_call(
        functools.partial(_mla_q_kernel, scale=scale),
        grid=(t // tm, heads * MLA_QK_PAD // tn),
        in_specs=[pl.BlockSpec((tm, q_lora), lambda i, j: (i, 0)),
                  pl.BlockSpec((1, q_lora), lambda i, j: (0, 0)),
                  pl.BlockSpec((q_lora, tn), lambda i, j: (0, j)), row128, row128],
        out_specs=pl.BlockSpec((tm, tn), lambda i, j: (i, j)),
        out_shape=jax.ShapeDtypeStruct((t, heads * MLA_QK_PAD), BF16),
        compiler_params=_params("parallel", "parallel"),
        name="mla_q",
    )(s1, q_norm.reshape(1, q_lora), w_q, cos_t, sin_t)

    hb = min(heads, COL_TILE // MLA_QK_PAD)
    kv_blk = q_lora // kv_lora
    assert q_lora % kv_lora == 0
    pe_blk = (q_lora + kv_lora) // LANES
    k, v_t = pl.pallas_call(
        _mla_kv_kernel,
        grid=(t // tm, heads // hb),
        in_specs=[pl.BlockSpec((tm, kv_lora), lambda i, j: (i, kv_blk)),
                  pl.BlockSpec((tm, LANES), lambda i, j: (i, pe_blk)),
                  pl.BlockSpec((1, kv_lora), lambda i, j: (0, 0)),
                  pl.BlockSpec((kv_lora, hb * MLA_NOPE), lambda i, j: (0, j)),
                  pl.BlockSpec((hb * MLA_V, kv_lora), lambda i, j: (j, 0)), row128, row128],
        out_specs=[pl.BlockSpec((tm, hb * MLA_QK_PAD), lambda i, j: (i, j)),
                   pl.BlockSpec((hb * MLA_V, tm), lambda i, j: (j, i))],
        out_shape=[jax.ShapeDtypeStruct((t, heads * MLA_QK_PAD), BF16),
                   jax.ShapeDtypeStruct((heads * MLA_V, t), BF16)],
        compiler_params=_params("parallel", "parallel"),
        name="mla_kv",
    )(s1, s1, kv_norm.reshape(1, kv_lora), w_k, w_vt, cos_t, sin_t)
    return q, k, v_t


def _col_max(x):
    while x.shape[0] > SUBLANES and x.shape[0] % (2 * SUBLANES) == 0:
        half = x.shape[0] // 2
        x = jnp.maximum(x[:half], x[half:])
    return jnp.max(x, axis=0, keepdims=True)


def _softmax_update(s_t, m_cur, v_t, m_sc, l_sc, acc_sc, col_shift=None):
    m_prev = m_sc[...]
    if col_shift is not None:
        m_cur = m_cur + col_shift
    m_new = jnp.maximum(m_prev, m_cur)
    shift = m_new if col_shift is None else m_new - col_shift
    p_t = jnp.exp2(s_t - shift)
    alpha = jnp.exp2(m_prev - m_new)
    l_sc[...] = alpha * l_sc[...] + jnp.sum(p_t, axis=0, keepdims=True)
    acc_sc[...] = alpha * acc_sc[...] + _dot(v_t, p_t.astype(v_t.dtype))
    m_sc[...] = m_new


def _init_softmax(m_sc, l_sc, acc_sc):
    m_sc[...] = jnp.full(m_sc.shape, NEG_INF, F32)
    l_sc[...] = jnp.zeros(l_sc.shape, F32)
    acc_sc[...] = jnp.zeros(acc_sc.shape, F32)


def _pipelined_sweep(qi, scores, values, update, mask, bufs):
    (s_a, m_a), (s_b, m_b) = bufs

    def produce(t, s_buf, m_buf):
        s_t = scores(t)
        s_buf[...] = s_t
        m_buf[...] = _col_max(s_t)

    def consume(t, s_buf, m_buf):
        update(s_buf[...], m_buf[...], values(t))

    def consume_diagonal(s_buf):
        s_t = mask(s_buf[...])
        update(s_t, _col_max(s_t), values(qi))

    produce(0, s_a, m_a)

    def pair(p, carry):
        produce(2 * p + 1, s_b, m_b)
        consume(2 * p, s_a, m_a)
        produce(2 * p + 2, s_a, m_a)
        consume(2 * p + 1, s_b, m_b)
        return carry

    lax.fori_loop(0, lax.shift_right_logical(qi, 1), pair, 0)

    @pl.when((qi & 1) == 0)
    def _():
        consume_diagonal(s_a)

    @pl.when((qi & 1) == 1)
    def _():
        produce(qi, s_b, m_b)
        consume(qi - 1, s_a, m_a)
        consume_diagonal(s_b)


def _mla_attn_kernel(q_ref, k_ref, vt_ref, o_ref, m_sc, l_sc, acc_sc, sa_sc, ma_sc, sb_sc, mb_sc, *, tq,
                     chunk_shift):
    qi = pl.program_id(2)
    q = q_ref[...]
    _init_softmax(m_sc, l_sc, acc_sc)

    def scores(kt):
        return _nt_dot(k_ref[pl.ds(pl.multiple_of(kt * tq, tq), tq), :], q)

    def values(kt):
        return vt_ref[:, pl.ds(pl.multiple_of(kt * tq, tq), tq)]

    def update(s_t, m_cur, v_t):
        _softmax_update(s_t, m_cur, v_t, m_sc, l_sc, acc_sc)

    def mask(s_t):
        key = lax.broadcasted_iota(I32, s_t.shape, 0)
        qry = lax.broadcasted_iota(I32, s_t.shape, 1)
        allowed = lax.shift_right_logical(key, chunk_shift) <= lax.shift_right_logical(qry, chunk_shift)
        return jnp.where(allowed, s_t, NEG_INF)

    _pipelined_sweep(qi, scores, values, update, mask, ((sa_sc, ma_sc), (sb_sc, mb_sc)))
    o_ref[...] = (acc_sc[...] / l_sc[...]).T.astype(o_ref.dtype)


def _mla_attention(q, k, v_t, batch, seq, heads):
    tq = _tile(seq, ATTN_TILE)
    assert tq % CHUNK == 0 and CHUNK & (CHUNK - 1) == 0
    nq = seq // tq
    return pl.pallas_call(
        functools.partial(_mla_attn_kernel, tq=tq, chunk_shift=CHUNK.bit_length() - 1),
        grid=(batch, heads, nq),
        in_specs=[pl.BlockSpec((tq, MLA_QK_PAD), lambda b, h, i: (b * nq + i, h)),
                  pl.BlockSpec((seq, MLA_QK_PAD), lambda b, h, i: (b, h)),
                  pl.BlockSpec((MLA_V, seq), lambda b, h, i: (h, b))],
        out_specs=pl.BlockSpec((tq, MLA_V), lambda b, h, i: (b * nq + i, h)),
        out_shape=jax.ShapeDtypeStruct((batch * seq, heads * MLA_V), BF16),
        scratch_shapes=[pltpu.VMEM((1, tq), F32), pltpu.VMEM((1, tq), F32), pltpu.VMEM((MLA_V, tq), F32),
                        pltpu.VMEM((tq, tq), F32), pltpu.VMEM((1, tq), F32),
                        pltpu.VMEM((tq, tq), F32), pltpu.VMEM((1, tq), F32)],
        compiler_params=_params("parallel", "parallel", "arbitrary"),
        name="mla_attention",
    )(q, k, v_t)


def _fox_attn_kernel(q_ref, k_ref, vt_ref, c_ref, g_ref, o_ref, m_sc, l_sc, acc_sc, ck_sc, sa_sc, ma_sc,
                     sb_sc, mb_sc, *, tq):
    qi = pl.program_id(2)
    q = q_ref[...]
    _init_softmax(m_sc, l_sc, acc_sc)
    seq = ck_sc.shape[0]

    @pl.when(qi == 0)
    def _():
        def fill(blk, carry):
            b0 = pl.multiple_of(blk * LANES, LANES)
            row = c_ref[0, 0, :, pl.ds(b0, LANES)] * LOG2_E
            ck_sc[pl.ds(b0, LANES), :] = jnp.broadcast_to(row, (LANES, LANES)).T
            return carry

        lax.fori_loop(0, seq // LANES, fill, 0)

    q0 = pl.multiple_of(qi * tq, tq)
    c_q = c_ref[0, 0, :, pl.ds(q0, tq)] * LOG2_E

    def scores(kt):
        k0 = pl.multiple_of(kt * tq, tq)
        c_k = ck_sc[pl.ds(k0, tq), :]
        return _nt_dot(k_ref[pl.ds(k0, tq), :], q) - jnp.concatenate([c_k] * (tq // LANES), axis=1)

    def values(kt):
        return vt_ref[:, pl.ds(pl.multiple_of(kt * tq, tq), tq)]

    def update(s_t, m_cur, v_t):
        _softmax_update(s_t, m_cur, v_t, m_sc, l_sc, acc_sc, col_shift=c_q)

    def mask(s_t):
        key = lax.broadcasted_iota(I32, s_t.shape, 0)
        qry = lax.broadcasted_iota(I32, s_t.shape, 1)
        return jnp.where(key <= qry, s_t, NEG_INF)

    _pipelined_sweep(qi, scores, values, update, mask, ((sa_sc, ma_sc), (sb_sc, mb_sc)))
    o_ref[...] = ((acc_sc[...] / l_sc[...]).T * g_ref[...].astype(F32)).astype(o_ref.dtype)


def _fox_attention(qk, v_t, c_rows, gate, batch, seq, heads):
    tq = _tile(seq, ATTN_TILE)
    assert tq % LANES == 0 or tq == seq
    nq = seq // tq
    dh = FOX_HEAD_DIM
    return pl.pallas_call(
        functools.partial(_fox_attn_kernel, tq=tq),
        grid=(batch, heads, nq),
        in_specs=[pl.BlockSpec((tq, dh), lambda b, h, i: (b * nq + i, h)),
                  pl.BlockSpec((seq, dh), lambda b, h, i: (b, heads + h)),
                  pl.BlockSpec((dh, seq), lambda b, h, i: (h, b)),
                  pl.BlockSpec((1, 1, 1, seq), lambda b, h, i: (b, h, 0, 0)),
                  pl.BlockSpec((tq, dh), lambda b, h, i: (b * nq + i, h))],
        out_specs=pl.BlockSpec((tq, dh), lambda b, h, i: (b * nq + i, h)),
        out_shape=jax.ShapeDtypeStruct((batch * seq, heads * dh), BF16),
        scratch_shapes=[pltpu.VMEM((1, tq), F32), pltpu.VMEM((1, tq), F32), pltpu.VMEM((dh, tq), F32),
                        pltpu.VMEM((seq, LANES), F32),
                        pltpu.VMEM((tq, tq), F32), pltpu.VMEM((1, tq), F32),
                        pltpu.VMEM((tq, tq), F32), pltpu.VMEM((1, tq), F32)],
        compiler_params=_params("parallel", "parallel", "arbitrary"),
        name="fox_attention",
    )(qk, qk, v_t, c_rows, gate)


def _fox_qk_kernel(x_ref, w_ref, g_ref, o_ref):
    y = _dot(x_ref[...], w_ref[...])
    g = g_ref[...]
    for hd in range(y.shape[1] // FOX_HEAD_DIM):
        sl = slice(hd * FOX_HEAD_DIM, (hd + 1) * FOX_HEAD_DIM)
        o_ref[:, sl] = _rms(y[:, sl], g[:, sl]).astype(o_ref.dtype)


def _fox_qk(a, w_qk, gain_row):
    t, d = a.shape
    n = w_qk.shape[1]
    tm, tn = _tile(t, ROW_TILE), _tile(n, COL_TILE)
    return pl.pallas_call(
        _fox_qk_kernel,
        grid=(t // tm, n // tn),
        in_specs=[pl.BlockSpec((tm, d), lambda i, j: (i, 0)), pl.BlockSpec((d, tn), lambda i, j: (0, j)),
                  pl.BlockSpec((1, tn), lambda i, j: (0, j))],
        out_specs=pl.BlockSpec((tm, tn), lambda i, j: (i, j)),
        out_shape=jax.ShapeDtypeStruct((t, n), BF16),
        compiler_params=_params("parallel", "parallel"),
        name="fox_qk",
    )(a, w_qk, gain_row)


def _fox_forget_kernel(h_ref, g_ref, w_ref, b_ref, c_ref, carry_sc, *, tiles_per_seq):
    i = pl.program_id(0)

    @pl.when(i % tiles_per_seq == 0)
    def _():
        carry_sc[...] = jnp.zeros(carry_sc.shape, F32)

    z = _split_dot(_rms(h_ref[...], g_ref[...]), w_ref) + b_ref[...]
    log_f = jnp.minimum(z, 0.0) - jnp.log(1.0 + jnp.exp(-jnp.abs(z)))

    tm = log_f.shape[0]
    row = lax.broadcasted_iota(I32, (tm, tm), 0)
    col = lax.broadcasted_iota(I32, (tm, tm), 1)
    tri = (col <= row).astype(BF16)
    sums = _dot(tri, jnp.concatenate(_split(log_f, 3) + [jnp.zeros_like(log_f, BF16)], axis=1))
    c = (sums[:, :LANES] + (sums[:, LANES:2 * LANES] + sums[:, 2 * LANES:3 * LANES])) + carry_sc[...]
    carry_sc[...] = c[tm - 1:tm, :]
    c_ref[0] = c.T


def _fox_forget_cumsum(h, gain, w_f, b_f, batch, seq):
    t, d = h.shape
    heads = w_f.shape[1]
    assert heads <= LANES
    tm = _tile(seq, ROUTE_TILE)
    w12 = jnp.concatenate(_split(jnp.pad(w_f, ((0, 0), (0, LANES - heads))), 2), axis=1)
    b = jnp.pad(b_f, (0, LANES - heads)).reshape(1, LANES)
    tps = seq // tm
    return pl.pallas_call(
        functools.partial(_fox_forget_kernel, tiles_per_seq=tps),
        grid=(t // tm,),
        in_specs=[pl.BlockSpec((tm, d), lambda i: (i, 0)), pl.BlockSpec((1, d), lambda i: (0, 0)),
                  pl.BlockSpec((d, 2 * LANES), lambda i: (0, 0)), pl.BlockSpec((1, LANES), lambda i: (0, 0))],
        out_specs=pl.BlockSpec((1, LANES, tm), lambda i: (i // tps, 0, i % tps)),
        out_shape=jax.ShapeDtypeStruct((batch, LANES, seq), F32),
        scratch_shapes=[pltpu.VMEM((1, LANES), F32)],
        compiler_params=_params("arbitrary"),
        name="fox_forget_cumsum",
    )(h, gain.reshape(1, d), w12, b)


def _router_kernel(h_ref, g_ref, w_ref, b_ref, t_ref, meta_ref, metat_ref, cnt_ref, carry_sc):
    i = pl.program_id(0)

    @pl.when(i == 0)
    def _():
        carry_sc[...] = jnp.zeros(carry_sc.shape, F32)

    t = _rms(h_ref[...], g_ref[...])
    t_ref[...] = _pack_halves(t)

    logits = _split_dot(t, w_ref) + b_ref[...]
    tm = logits.shape[0]
    lane = lax.broadcasted_iota(I32, logits.shape, 1)
    lane_f = lane.astype(F32)
    first = lambda hit: jnp.min(jnp.where(hit, lane_f, float(LANES)), axis=1, keepdims=True).astype(I32)

    is_grp = lane < MOE_GROUPS
    gl = jnp.where(is_grp, logits, -jnp.inf)
    gmax = jnp.max(gl, axis=1, keepdims=True)
    g_sel = first(gl == gmax)
    gexp = jnp.where(is_grp, jnp.exp(logits - gmax), 0.0)
    g_w = 1.0 / jnp.sum(gexp, axis=1, keepdims=True)

    lo = MOE_GROUPS + MOE_EXPERTS_PER_GROUP * g_sel
    in_grp = jnp.logical_and(lane >= lo, lane < lo + MOE_EXPERTS_PER_GROUP)
    el = jnp.where(in_grp, logits, -jnp.inf)
    emax = jnp.max(el, axis=1, keepdims=True)
    eexp = jnp.where(in_grp, jnp.exp(logits - emax), 0.0)
    prob = eexp / jnp.sum(eexp, axis=1, keepdims=True)
    cand1 = jnp.where(in_grp, prob, -1.0)
    p1 = jnp.max(cand1, axis=1, keepdims=True)
    j1 = first(cand1 == p1)
    cand2 = jnp.where(lane == j1, -1.0, cand1)
    p2 = jnp.max(cand2, axis=1, keepdims=True)
    j2 = first(cand2 == p2)
    denom = p1 + p2
    wt1 = p1 / denom * g_w
    wt2 = p2 / denom * g_w
    e1 = j1 - MOE_GROUPS
    e2 = j2 - MOE_GROUPS

    hit1 = lane == e1
    hit2 = lane == e2
    row = lax.broadcasted_iota(I32, (tm, tm), 0)
    col = lax.broadcasted_iota(I32, (tm, tm), 1)
    before = (col < row).astype(BF16)
    pre = _dot(before, jnp.concatenate([hit1.astype(BF16), hit2.astype(BF16)], axis=1))
    pre1, pre2 = pre[:, :LANES], pre[:, LANES:]
    carry = carry_sc[...]
    cnt1 = jnp.sum(hit1.astype(F32), axis=0, keepdims=True)
    cnt2 = jnp.sum(hit2.astype(F32), axis=0, keepdims=True)
    rank1 = jnp.sum(jnp.where(hit1, pre1 + carry, 0.0), axis=1, keepdims=True)
    rank2 = jnp.sum(jnp.where(hit2, pre2 + (carry + cnt1), 0.0), axis=1, keepdims=True)
    total = carry + cnt1 + cnt2
    carry_sc[...] = total
    cnt_ref[...] = jnp.broadcast_to(total, cnt_ref.shape)

    bits = lambda x: lax.bitcast_convert_type(jnp.broadcast_to(x, logits.shape), I32)
    meta = jnp.where(lane == 0, e1, 0)
    meta = jnp.where(lane == 1, e2, meta)
    meta = jnp.where(lane == 2, rank1.astype(I32), meta)
    meta = jnp.where(lane == 3, rank2.astype(I32), meta)
    meta = jnp.where(lane == 4, bits(wt1), meta)
    meta = jnp.where(lane == 5, bits(wt2), meta)
    meta_ref[...] = meta
    metat_ref[...] = meta.T[:SUBLANES, :]


def _router(h, gain, w_grp, b_grp, w_rt, b_rt):
    t, d = h.shape
    tm = _tile(t, ROUTE_TILE)
    n_used = MOE_GROUPS + MOE_EXPERTS
    w = jnp.pad(jnp.concatenate([w_grp, w_rt], axis=1), ((0, 0), (0, LANES - n_used)))
    w12 = jnp.concatenate(_split(w, 2), axis=1)
    b = jnp.pad(jnp.concatenate([b_grp, b_rt]), (0, LANES - n_used)).reshape(1, LANES)
    return pl.pallas_call(
        _router_kernel,
        grid=(t // tm,),
        in_specs=[pl.BlockSpec((tm, d), lambda i: (i, 0)), pl.BlockSpec((1, d), lambda i: (0, 0)),
                  pl.BlockSpec((d, 2 * LANES), lambda i: (0, 0)), pl.BlockSpec((1, LANES), lambda i: (0, 0))],
        out_specs=[pl.BlockSpec((tm, d // 2), lambda i: (i, 0)),
                   pl.BlockSpec((tm, LANES), lambda i: (i, 0)),
                   pl.BlockSpec((SUBLANES, tm), lambda i: (0, i)),
                   pl.BlockSpec((SUBLANES, LANES), lambda i: (0, 0))],
        out_shape=[jax.ShapeDtypeStruct((t, d // 2), U32),
                   jax.ShapeDtypeStruct((t, LANES), I32),
                   jax.ShapeDtypeStruct((SUBLANES, t), I32),
                   jax.ShapeDtypeStruct((SUBLANES, LANES), F32)],
        scratch_shapes=[pltpu.VMEM((1, LANES), F32)],
        compiler_params=_params("arbitrary"),
        name="moe_router",
    )(h, gain.reshape(1, d), w12, b)


def _dest_kernel(offs_ref, mt_ref, dest_ref):
    mt = mt_ref[...]
    experts = mt[0:2, :]
    base = jnp.zeros(experts.shape, I32)
    for e in range(MOE_EXPERTS):
        base = jnp.where(experts == e, offs_ref[e], base)
    row = lax.broadcasted_iota(I32, mt.shape, 0)
    dest_ref[...] = jnp.where(row < 2, jnp.concatenate([base + mt[2:4, :], mt[2:SUBLANES, :]], axis=0), 0)


def _dest_rows_all(meta_t, offs):
    rows, t = meta_t.shape
    return pl.pallas_call(
        _dest_kernel,
        grid_spec=pltpu.PrefetchScalarGridSpec(
            num_scalar_prefetch=1,
            grid=(1,),
            in_specs=[pl.BlockSpec((rows, t), lambda i, offs: (0, 0))],
            out_specs=pl.BlockSpec((rows, t), lambda i, offs: (0, 0)),
        ),
        out_shape=jax.ShapeDtypeStruct((rows, t), I32),
        compiler_params=_params("arbitrary"),
        name="moe_dest",
    )(offs, meta_t)


def _row_copy(src, src_row, dst, dst_row, sem):
    return pltpu.make_async_copy(src.at[pl.ds(src_row, 1)], dst.at[pl.ds(dst_row, 1)], sem)


_ROW_UNROLL = 8


def _dispatch_kernel(zs_ref, nv_ref, dest_ref, t_ref, xs_ref, zero_sc, sem, *, tm, tz):
    i = pl.program_id(0)

    @pl.when(i == 0)
    def _():
        zero_sc[...] = jnp.zeros(zero_sc.shape, zero_sc.dtype)
        fill = lambda row0: pltpu.make_async_copy(zero_sc, xs_ref.at[pl.ds(pl.multiple_of(row0, tz), tz)], sem)
        fills = [fill(zs_ref[e]) for e in range(MOE_EXPERTS)]
        for c in fills:
            c.start()
        for c in fills:
            c.wait()

        def tail(tile, carry):
            c = fill(tile * tz)
            c.start()
            c.wait()
            return carry

        lax.fori_loop(nv_ref[0], xs_ref.shape[0] // tz, tail, 0)

    def issue(blk, carry):
        for u in range(_ROW_UNROLL):
            r = blk * _ROW_UNROLL + u
            _row_copy(t_ref, r, xs_ref, dest_ref[0, r], sem).start()
            _row_copy(t_ref, r, xs_ref, dest_ref[1, r], sem).start()
        return carry

    lax.fori_loop(0, tm // _ROW_UNROLL, issue, 0)

    def drain(blk, carry):
        for u in range(2 * _ROW_UNROLL):
            _row_copy(t_ref, 0, xs_ref, 0, sem).wait()
        return carry

    lax.fori_loop(0, tm // _ROW_UNROLL, drain, 0)


def _dispatch(t_packed, dest_t, zero_start, n_valid, n_rows):
    t, d2 = t_packed.shape
    tm = _tile(t, MOVE_TILE)
    tz = EXPERT_TILE
    mspec = pl.BlockSpec((SUBLANES, tm), lambda i, *_: (0, i), memory_space=pltpu.SMEM)
    return pl.pallas_call(
        functools.partial(_dispatch_kernel, tm=tm, tz=tz),
        grid_spec=pltpu.PrefetchScalarGridSpec(
            num_scalar_prefetch=2,
            grid=(t // tm,),
            in_specs=[mspec, pl.BlockSpec((tm, d2), lambda i, *_: (i, 0))],
            out_specs=pl.BlockSpec(memory_space=pl.ANY),
            scratch_shapes=[pltpu.VMEM((tz, d2), U32), pltpu.SemaphoreType.DMA(())],
        ),
        out_shape=jax.ShapeDtypeStruct((n_rows, d2), U32),
        compiler_params=_params("arbitrary"),
        name="moe_dispatch",
    )(zero_start, n_valid, dest_t, t_packed)


def _experts_kernel(te_ref, nv_ref, x_ref, wg_ref, wu_ref, wd_ref, y_ref, wg_sc, wu_sc, wd_sc):
    i = pl.program_id(0)
    live = i < nv_ref[0]

    @pl.when(jnp.logical_or(i == 0, te_ref[i] != te_ref[jnp.maximum(i - 1, 0)]))
    def _():
        wg_sc[...] = wg_ref[0].astype(BF16)
        wu_sc[...] = wu_ref[0].astype(BF16)
        wd_sc[...] = wd_ref[0].astype(BF16)

    @pl.when(jnp.logical_not(live))
    def _():
        y_ref[...] = jnp.zeros(y_ref.shape, y_ref.dtype)

    @pl.when(live)
    def _():
        hi, lo = _unpack_halves(x_ref[...])
        x = jnp.concatenate([hi, lo], axis=1).astype(BF16)
        g = _dot(x, wg_sc[...])
        u = _dot(x, wu_sc[...])
        hid = (g * jax.nn.sigmoid(g) * u).astype(BF16)
        y_ref[...] = _pack_halves(_dot(hid, wd_sc[...]))


def _experts(xs, tile_expert, n_valid, w_gate, w_up, w_down, n_tiles):
    tm = EXPERT_TILE
    d2 = xs.shape[1]
    _, d, f = w_gate.shape
    live = lambda i, nv: jnp.minimum(i, nv[0] - 1)
    return pl.pallas_call(
        _experts_kernel,
        grid_spec=pltpu.PrefetchScalarGridSpec(
            num_scalar_prefetch=2,
            grid=(n_tiles,),
            in_specs=[pl.BlockSpec((tm, d2), lambda i, te, nv: (live(i, nv), 0)),
                      pl.BlockSpec((1, d, f), lambda i, te, nv: (te[i], 0, 0)),
                      pl.BlockSpec((1, d, f), lambda i, te, nv: (te[i], 0, 0)),
                      pl.BlockSpec((1, f, d), lambda i, te, nv: (te[i], 0, 0))],
            out_specs=pl.BlockSpec((tm, d2), lambda i, te, nv: (i, 0)),
            scratch_shapes=[pltpu.VMEM((d, f), BF16), pltpu.VMEM((d, f), BF16), pltpu.VMEM((f, d), BF16)],
        ),
        out_shape=jax.ShapeDtypeStruct((n_tiles * tm, d2), U32),
        compiler_params=_params("arbitrary"),
        name="moe_experts",
    )(tile_expert, n_valid, xs, w_gate, w_up, w_down)


def _combine_kernel(dest_ref, meta_ref, h_ref, g_ref, ys_ref, *rest, tm, final):
    if final:
        o_ref, y1_sc, y2_sc, sem = rest
    else:
        hn_ref, a_ref, y1_sc, y2_sc, sem = rest

    def issue(blk, carry):
        for u in range(_ROW_UNROLL):
            r = blk * _ROW_UNROLL + u
            _row_copy(ys_ref, dest_ref[0, r], y1_sc, r, sem).start()
            _row_copy(ys_ref, dest_ref[1, r], y2_sc, r, sem).start()
        return carry

    lax.fori_loop(0, tm // _ROW_UNROLL, issue, 0)

    def drain(blk, carry):
        for u in range(2 * _ROW_UNROLL):
            _row_copy(ys_ref, 0, y1_sc, 0, sem).wait()
        return carry

    lax.fori_loop(0, tm // _ROW_UNROLL, drain, 0)

    meta = meta_ref[...]
    lane = lax.broadcasted_iota(I32, meta.shape, 1)
    wbits = lax.bitcast_convert_type(meta, F32)
    wt1 = jnp.sum(jnp.where(lane == 4, wbits, 0.0), axis=1, keepdims=True)
    wt2 = jnp.sum(jnp.where(lane == 5, wbits, 0.0), axis=1, keepdims=True)
    hi1, lo1 = _unpack_halves(y1_sc[...])
    hi2, lo2 = _unpack_halves(y2_sc[...])
    moe = jnp.concatenate([wt1 * hi1 + wt2 * hi2, wt1 * lo1 + wt2 * lo2], axis=1)
    h_new = h_ref[...] + moe
    if final:
        o_ref[...] = _rms(h_new, g_ref[...])
    else:
        hn_ref[...] = h_new
        a_ref[...] = _rms(h_new, g_ref[...]).astype(a_ref.dtype)


def _combine(ys, dest_t, meta, h, gain, final):
    t, d = h.shape
    tm = _tile(t, MOVE_TILE)
    d2 = d // 2
    mspec = pl.BlockSpec((SUBLANES, tm), lambda i: (0, i), memory_space=pltpu.SMEM)
    row = pl.BlockSpec((tm, d), lambda i: (i, 0))
    if final:
        out_specs, out_shape = row, jax.ShapeDtypeStruct((t, d), F32)
    else:
        out_specs = [row, row]
        out_shape = [jax.ShapeDtypeStruct((t, d), F32), jax.ShapeDtypeStruct((t, d), BF16)]
    return pl.pallas_call(
        functools.partial(_combine_kernel, tm=tm, final=final),
        grid=(t // tm,),
        in_specs=[mspec, pl.BlockSpec((tm, LANES), lambda i: (i, 0)), row,
                  pl.BlockSpec((1, d), lambda i: (0, 0)), pl.BlockSpec(memory_space=pl.ANY)],
        out_specs=out_specs,
        out_shape=out_shape,
        scratch_shapes=[pltpu.VMEM((tm, d2), U32), pltpu.VMEM((tm, d2), U32), pltpu.SemaphoreType.DMA(())],
        compiler_params=_params("arbitrary"),
        name="moe_combine",
    )(dest_t, meta, h, gain.reshape(1, d), ys)


def _hier_moe(h, ffn_gain, w_grp, b_grp, w_rt, b_rt, w_gate, w_up, w_down, next_gain, final):
    t, d = h.shape
    tmx = EXPERT_TILE
    t_packed, meta, meta_t, cnt = _router(h, ffn_gain, w_grp, b_grp, w_rt, b_rt)

    counts = cnt[0, :MOE_EXPERTS].astype(I32)
    padded = (counts + tmx - 1) // tmx * tmx
    ends = jnp.cumsum(padded)
    offs = ends - padded
    n_tiles = (2 * t) // tmx + MOE_EXPERTS
    n_valid = (ends[-1] // tmx).reshape(1)
    tile_start = jnp.minimum(jnp.arange(n_tiles, dtype=I32) * tmx, ends[-1] - 1)
    tile_expert = jnp.sum((ends[None, :] <= tile_start[:, None]).astype(I32), axis=1)
    zero_start = offs + counts // tmx * tmx

    dest_t = _dest_rows_all(meta_t, offs)
    xs = _dispatch(t_packed, dest_t, zero_start, n_valid, (n_tiles + 1) * tmx)
    ys = _experts(xs, tile_expert, n_valid, w_gate, w_up, w_down, n_tiles)
    return _combine(ys, dest_t, meta, h, next_gain, final)


def kernel(x, positions, attn_norm, ffn_norm, final_norm, mla_w_dq, mla_q_norm, mla_w_uq, mla_w_dkv,
           mla_kv_norm, mla_w_ukv, mla_w_o, fox_w_qkv, fox_q_norm, fox_k_norm, fox_w_f, fox_b_f,
           fox_w_og, fox_w_o, moe_w_grp, moe_b_grp, moe_w_rt, moe_b_rt, moe_w_gate, moe_w_up, moe_w_down):
    batch, seq, d = x.shape
    depth = attn_norm.shape[0]
    t = batch * seq
    cos_t, sin_t = _rope_tables(positions)
    h = x.reshape(t, d)
    a = _norm(h, attn_norm[0], BF16)
    out = None
    for i in range(depth):
        j = i // 2
        if i % 2 == 0:
            heads = mla_w_uq.shape[2] // (MLA_NOPE + MLA_ROPE)
            q, k, v_t = _mla_project(a, cos_t, sin_t, mla_w_dq[j], mla_q_norm[j], mla_w_uq[j],
                                     mla_w_dkv[j], mla_kv_norm[j], mla_w_ukv[j])
            o = _mla_attention(q, k, v_t, batch, seq, heads)
            w_o = mla_w_o[j]
        else:
            dh = FOX_HEAD_DIM
            heads = fox_w_qkv.shape[2] // (3 * dh)
            w_qkv = fox_w_qkv[j].astype(BF16)
            gain_row = jnp.concatenate([jnp.tile(fox_q_norm[j] * (dh ** -0.5 * LOG2_E), heads),
                                        jnp.tile(fox_k_norm[j], heads)])[None, :]
            qk = _fox_qk(a, w_qkv[:, :2 * heads * dh], gain_row)
            v_t = _mm_nt(w_qkv[:, 2 * heads * dh:].T, a, BF16, name="fox_v")
            gate = _mm(a, fox_w_og[j].astype(BF16), BF16, act="sigmoid", name="fox_gate")
            c_t = _fox_forget_cumsum(h, attn_norm[i], fox_w_f[j], fox_b_f[j], batch, seq)
            c_rows = c_t[:, :heads, :].reshape(batch, heads, 1, seq)
            o = _fox_attention(qk, v_t, c_rows, gate, batch, seq, heads)
            w_o = fox_w_o[j]
        h = _mm_residual(o, w_o.astype(BF16), h)
        final = i == depth - 1
        next_gain = final_norm if final else attn_norm[i + 1]
        res = _hier_moe(h, ffn_norm[i], moe_w_grp[i], moe_b_grp[i], moe_w_rt[i], moe_b_rt[i],
                        moe_w_gate[i], moe_w_up[i], moe_w_down[i], next_gain, final)
        if final:
            out = res
        else:
            h, a = res
    return out.reshape(batch, seq, d)
```

```python
import functools

import jax
import jax.numpy as jnp
from jax import lax
from jax.experimental import pallas as pl
from jax.experimental.pallas import tpu as pltpu

F32 = jnp.float32
BF16 = jnp.bfloat16
I32 = jnp.int32
U32 = jnp.uint32

RMS_EPS = 1e-6
NEG_INF = -1e30
CHUNK = 64
MLA_NOPE = 128
MLA_ROPE = 64
MLA_V = 128
MLA_QK_PAD = 256
ROPE_THETA = 10000.0
LOG2_E = 1.4426950408889634
FOX_HEAD_DIM = 128
MOE_GROUPS = 8
MOE_EXPERTS_PER_GROUP = 4
MOE_EXPERTS = MOE_GROUPS * MOE_EXPERTS_PER_GROUP

LANES = 128
SUBLANES = 8
V7X_VMEM_LIMIT_BYTES = 56 * 1024 * 1024

ROW_TILE = 1024
COL_TILE = 1024
ATTN_TILE = 512
ROUTE_TILE = 512
MOVE_TILE = 256
EXPERT_TILE = 256


def _params(*sem):
    return pltpu.CompilerParams(dimension_semantics=sem, vmem_limit_bytes=V7X_VMEM_LIMIT_BYTES)


def _tile(n, t):
    if n <= t:
        return n
    step = LANES if t % LANES == 0 else SUBLANES
    for c in range(t - t % step, 0, -step):
        if n % c == 0:
            return c
    raise ValueError(f"no aligned tile for {n} under {t}")


def _rms(x, gain):
    ms = jnp.mean(x * x, axis=-1, keepdims=True)
    return x * lax.rsqrt(ms + RMS_EPS) * gain


def _dot(a, b):
    return jnp.dot(a, b, preferred_element_type=F32)


def _nt_dot(a, b):
    return lax.dot_general(a, b, (((1,), (1,)), ((), ())), preferred_element_type=F32)


def _pack_halves(x):
    n = x.shape[1] // 2
    hi = lax.bitcast_convert_type(x[:, :n].astype(BF16).astype(F32), U32)
    lo = lax.bitcast_convert_type(x[:, n:].astype(BF16).astype(F32), U32)
    return hi | lax.shift_right_logical(lo, jnp.uint32(16))


def _unpack_halves(p):
    hi = lax.bitcast_convert_type(p & jnp.uint32(0xFFFF0000), F32)
    lo = lax.bitcast_convert_type(lax.shift_left(p, jnp.uint32(16)), F32)
    return hi, lo


def _to_slabs(ref, x):
    m, width = x.shape
    c = width // LANES
    for j in range(c):
        ref[pl.ds(j, m, stride=c), :] = x[:, j * LANES:(j + 1) * LANES]


def _from_slabs(ref, m):
    c = ref.shape[0] // m
    return jnp.concatenate([ref[pl.ds(j, m, stride=c), :] for j in range(c)], axis=1)


def _split(x, terms):
    out = []
    for _ in range(terms):
        hi = x.astype(BF16)
        out.append(hi)
        x = x - hi.astype(F32)
    return out


def _split_dot(x, w12_ref):
    m = x.shape[0]
    n = w12_ref.shape[1] // 2
    prod = _dot(jnp.concatenate(_split(x, 2), axis=0), w12_ref[...])
    return (prod[:m, :n] + (prod[:m, n:] + prod[m:, :n])) + prod[m:, n:]


def _rope_lanes(seg, cos_t, sin_t):
    half = MLA_ROPE // 2
    lane = lax.broadcasted_iota(I32, seg.shape, 1)
    swapped = jnp.where(lane < half, pltpu.roll(seg, LANES - half, 1), pltpu.roll(seg, half, 1))
    return seg * cos_t + swapped * sin_t


def _rope_table_kernel(pos_ref, freq_ref, cos_ref, sin_ref):
    ang = pos_ref[...] * freq_ref[...]
    lane = lax.broadcasted_iota(I32, ang.shape, 1)
    half = MLA_ROPE // 2
    valid = lane < MLA_ROPE
    cos_ref[...] = jnp.where(valid, jnp.cos(ang), 0.0)
    sin_ref[...] = jnp.where(valid, jnp.where(lane < half, -jnp.sin(ang), jnp.sin(ang)), 0.0)


def _rope_tables(positions):
    t = positions.size
    tm = _tile(t, ROW_TILE)
    half = MLA_ROPE // 2
    inv_freq = ROPE_THETA ** (-jnp.arange(0, MLA_ROPE, 2, dtype=F32) / MLA_ROPE)
    freq_row = jnp.concatenate([inv_freq, inv_freq, jnp.zeros((LANES - 2 * half,), F32)])[None, :]
    pos = positions.reshape(t, 1).astype(F32)
    out = jax.ShapeDtypeStruct((t, LANES), F32)
    return pl.pallas_call(
        _rope_table_kernel,
        grid=(t // tm,),
        in_specs=[pl.BlockSpec((tm, 1), lambda i: (i, 0)), pl.BlockSpec((1, LANES), lambda i: (0, 0))],
        out_specs=[pl.BlockSpec((tm, LANES), lambda i: (i, 0))] * 2,
        out_shape=[out, out],
        compiler_params=_params("parallel"),
        name="rope_tables",
    )(pos, freq_row)


def _norm_kernel(h_ref, g_ref, a_ref):
    a_ref[...] = _rms(h_ref[...], g_ref[...]).astype(a_ref.dtype)


def _norm(h, gain, out_dtype):
    t, d = h.shape
    tm = _tile(t, ROUTE_TILE)
    return pl.pallas_call(
        _norm_kernel,
        grid=(t // tm,),
        in_specs=[pl.BlockSpec((tm, d), lambda i: (i, 0)), pl.BlockSpec((1, d), lambda i: (0, 0))],
        out_specs=pl.BlockSpec((tm, d), lambda i: (i, 0)),
        out_shape=jax.ShapeDtypeStruct((t, d), out_dtype),
        compiler_params=_params("parallel"),
        name="rmsnorm",
    )(h, gain.reshape(1, d))


def _mm_kernel(x_ref, w_ref, o_ref, *, act):
    y = _dot(x_ref[...], w_ref[...])
    if act == "sigmoid":
        y = jax.nn.sigmoid(y)
    o_ref[...] = y.astype(o_ref.dtype)


def _mm(x, w, out_dtype, act=None, name="mm"):
    m, k = x.shape
    n = w.shape[1]
    tm, tn = _tile(m, ROW_TILE), _tile(n, COL_TILE)
    return pl.pallas_call(
        functools.partial(_mm_kernel, act=act),
        grid=(m // tm, n // tn),
        in_specs=[pl.BlockSpec((tm, k), lambda i, j: (i, 0)), pl.BlockSpec((k, tn), lambda i, j: (0, j))],
        out_specs=pl.BlockSpec((tm, tn), lambda i, j: (i, j)),
        out_shape=jax.ShapeDtypeStruct((m, n), out_dtype),
        compiler_params=_params("parallel", "parallel"),
        name=name,
    )(x, w)


def _mm_nt_kernel(wt_ref, x_ref, o_ref):
    o_ref[...] = _nt_dot(wt_ref[...], x_ref[...]).astype(o_ref.dtype)


def _mm_nt(w_t, x, out_dtype, name):
    n, k = w_t.shape
    m = x.shape[0]
    tm, tn = _tile(m, ROW_TILE), _tile(n, COL_TILE)
    return pl.pallas_call(
        _mm_nt_kernel,
        grid=(m // tm, n // tn),
        in_specs=[pl.BlockSpec((tn, k), lambda i, j: (j, 0)), pl.BlockSpec((tm, k), lambda i, j: (i, 0))],
        out_specs=pl.BlockSpec((tn, tm), lambda i, j: (j, i)),
        out_shape=jax.ShapeDtypeStruct((n, m), out_dtype),
        compiler_params=_params("parallel", "parallel"),
        name=name,
    )(w_t, x)


def _mm_res_kernel(x_ref, w_ref, h_ref, o_ref):
    o_ref[...] = h_ref[...] + _dot(x_ref[...], w_ref[...])


def _mm_residual(x, w, h):
    m, k = x.shape
    n = w.shape[1]
    tm, tn = _tile(m, ROW_TILE), _tile(n, COL_TILE)
    return pl.pallas_call(
        _mm_res_kernel,
        grid=(m // tm, n // tn),
        in_specs=[pl.BlockSpec((tm, k), lambda i, j: (i, 0)), pl.BlockSpec((k, tn), lambda i, j: (0, j)),
                  pl.BlockSpec((tm, tn), lambda i, j: (i, j))],
        out_specs=pl.BlockSpec((tm, tn), lambda i, j: (i, j)),
        out_shape=jax.ShapeDtypeStruct((m, n), F32),
        compiler_params=_params("parallel", "parallel"),
        name="out_proj_residual",
    )(x, w, h)


def _mla_q_kernel(c_ref, g_ref, w_ref, cos_ref, sin_ref, q_ref, *, scale):
    cq = _rms(c_ref[...], g_ref[...]).astype(BF16)
    y = _dot(cq, w_ref[...]) * scale
    cos_t, sin_t = cos_ref[...], sin_ref[...]
    for hd in range(y.shape[1] // MLA_QK_PAD):
        base = hd * MLA_QK_PAD
        q_ref[:, base:base + MLA_NOPE] = y[:, base:base + MLA_NOPE].astype(q_ref.dtype)
        roped = _rope_lanes(y[:, base + MLA_NOPE:base + MLA_QK_PAD], cos_t, sin_t)
        q_ref[:, base + MLA_NOPE:base + MLA_QK_PAD] = roped.astype(q_ref.dtype)


def _mla_kv_kernel(c_ref, pe_ref, g_ref, wk_ref, wvt_ref, cos_ref, sin_ref, k_ref, vt_ref):
    ckv = _rms(c_ref[...], g_ref[...]).astype(BF16)
    kn = _dot(ckv, wk_ref[...])
    vt_ref[...] = _nt_dot(wvt_ref[...], ckv).astype(vt_ref.dtype)
    k_pe = _rope_lanes(pe_ref[...], cos_ref[...], sin_ref[...]).astype(k_ref.dtype)
    for hd in range(kn.shape[1] // MLA_NOPE):
        k_ref[:, hd * MLA_QK_PAD:hd * MLA_QK_PAD + MLA_NOPE] = (
            kn[:, hd * MLA_NOPE:(hd + 1) * MLA_NOPE].astype(k_ref.dtype))
        k_ref[:, hd * MLA_QK_PAD + MLA_NOPE:(hd + 1) * MLA_QK_PAD] = k_pe


def _mla_project(a, cos_t, sin_t, w_dq, q_norm, w_uq, w_dkv, kv_norm, w_ukv):
    t, d = a.shape
    q_lora = w_dq.shape[1]
    kv_lora = w_dkv.shape[1] - MLA_ROPE
    heads = w_uq.shape[1] // (MLA_NOPE + MLA_ROPE)
    assert q_lora % LANES == 0 and kv_lora % LANES == 0

    w_down = jnp.concatenate(
        [w_dq, w_dkv, jnp.zeros((d, LANES - MLA_ROPE), w_dkv.dtype)], axis=1).astype(BF16)
    s1 = _mm(a, w_down, F32, name="mla_down")

    w_q = w_uq.reshape(q_lora, heads, MLA_NOPE + MLA_ROPE)
    w_q = jnp.pad(w_q, ((0, 0), (0, 0), (0, MLA_QK_PAD - MLA_NOPE - MLA_ROPE)))
    w_q = w_q.reshape(q_lora, heads * MLA_QK_PAD).astype(BF16)
    w_kv = w_ukv.reshape(kv_lora, heads, MLA_NOPE + MLA_V)
    w_k = w_kv[:, :, :MLA_NOPE].reshape(kv_lora, heads * MLA_NOPE).astype(BF16)
    w_vt = w_kv[:, :, MLA_NOPE:].reshape(kv_lora, heads * MLA_V).T.astype(BF16)

    tm = _tile(t, ROW_TILE)
    scale = (MLA_NOPE + MLA_ROPE) ** -0.5 * LOG2_E
    tn = _tile(heads * MLA_QK_PAD, COL_TILE)
    row128 = pl.BlockSpec((tm, LANES), lambda i, j: (i, 0))
    q = pl.pallas_call(
        functools.partial(_mla_q_kernel, scale=scale),
        grid=(t // tm, heads * MLA_QK_PAD // tn),
        in_specs=[pl.BlockSpec((tm, q_lora), lambda i, j: (i, 0)),
                  pl.BlockSpec((1, q_lora), lambda i, j: (0, 0)),
                  pl.BlockSpec((q_lora, tn), lambda i, j: (0, j)), row128, row128],
        out_specs=pl.BlockSpec((tm, tn), lambda i, j: (i, j)),
        out_shape=jax.ShapeDtypeStruct((t, heads * MLA_QK_PAD), BF16),
        compiler_params=_params("parallel", "parallel"),
        name="mla_q",
    )(s1, q_norm.reshape(1, q_lora), w_q, cos_t, sin_t)

    hb = min(heads, COL_TILE // MLA_QK_PAD)
    kv_blk = q_lora // kv_lora
    assert q_lora % kv_lora == 0
    pe_blk = (q_lora + kv_lora) // LANES
    k, v_t = pl.pallas_call(
        _mla_kv_kernel,
        grid=(t // tm, heads // hb),
        in_specs=[pl.BlockSpec((tm, kv_lora), lambda i, j: (i, kv_blk)),
                  pl.BlockSpec((tm, LANES), lambda i, j: (i, pe_blk)),
                  pl.BlockSpec((1, kv_lora), lambda i, j: (0, 0)),
                  pl.BlockSpec((kv_lora, hb * MLA_NOPE), lambda i, j: (0, j)),
                  pl.BlockSpec((hb * MLA_V, kv_lora), lambda i, j: (j, 0)), row128, row128],
        out_specs=[pl.BlockSpec((tm, hb * MLA_QK_PAD), lambda i, j: (i, j)),
                   pl.BlockSpec((hb * MLA_V, tm), lambda i, j: (j, i))],
        out_shape=[jax.ShapeDtypeStruct((t, heads * MLA_QK_PAD), BF16),
                   jax.ShapeDtypeStruct((heads * MLA_V, t), BF16)],
        compiler_params=_params("parallel", "parallel"),
        name="mla_kv",
    )(s1, s1, kv_norm.reshape(1, kv_lora), w_k, w_vt, cos_t, sin_t)
    return q, k, v_t


def _col_max(x):
    while x.shape[0] > SUBLANES and x.shape[0] % (2 * SUBLANES) == 0:
        half = x.shape[0] // 2
        x = jnp.maximum(x[:half], x[half:])
    return jnp.max(x, axis=0, keepdims=True)


def _softmax_update(s_t, m_cur, v_t, m_sc, l_sc, acc_sc, col_shift=None):
    m_prev = m_sc[...]
    if col_shift is not None:
        m_cur = m_cur + col_shift
    m_new = jnp.maximum(m_prev, m_cur)
    shift = m_new if col_shift is None else m_new - col_shift
    p_t = jnp.exp2(s_t - shift)
    alpha = jnp.exp2(m_prev - m_new)
    l_sc[...] = alpha * l_sc[...] + jnp.sum(p_t, axis=0, keepdims=True)
    acc_sc[...] = alpha * acc_sc[...] + _dot(v_t, p_t.astype(v_t.dtype))
    m_sc[...] = m_new


def _init_softmax(m_sc, l_sc, acc_sc):
    m_sc[...] = jnp.full(m_sc.shape, NEG_INF, F32)
    l_sc[...] = jnp.zeros(l_sc.shape, F32)
    acc_sc[...] = jnp.zeros(acc_sc.shape, F32)


def _pipelined_sweep(qi, scores, values, update, mask, bufs):
    (s_a, m_a), (s_b, m_b) = bufs

    def produce(t, s_buf, m_buf):
        s_t = scores(t)
        s_buf[...] = s_t
        m_buf[...] = _col_max(s_t)

    def consume(t, s_buf, m_buf):
        update(s_buf[...], m_buf[...], values(t))

    def consume_diagonal(s_buf):
        s_t = mask(s_buf[...])
        update(s_t, _col_max(s_t), values(qi))

    produce(0, s_a, m_a)

    def pair(p, carry):
        produce(2 * p + 1, s_b, m_b)
        consume(2 * p, s_a, m_a)
        produce(2 * p + 2, s_a, m_a)
        consume(2 * p + 1, s_b, m_b)
        return carry

    lax.fori_loop(0, lax.shift_right_logical(qi, 1), pair, 0)

    @pl.when((qi & 1) == 0)
    def _():
        consume_diagonal(s_a)

    @pl.when((qi & 1) == 1)
    def _():
        produce(qi, s_b, m_b)
        consume(qi - 1, s_a, m_a)
        consume_diagonal(s_b)


def _mla_attn_kernel(q_ref, k_ref, vt_ref, o_ref, m_sc, l_sc, acc_sc, sa_sc, ma_sc, sb_sc, mb_sc, qt_sc, *,
                     tq, chunk_shift):
    qi = pl.program_id(2)
    qt_sc[...] = q_ref[...].T
    _init_softmax(m_sc, l_sc, acc_sc)

    def scores(kt):
        return _dot(k_ref[pl.ds(pl.multiple_of(kt * tq, tq), tq), :], qt_sc[...])

    def values(kt):
        return vt_ref[:, pl.ds(pl.multiple_of(kt * tq, tq), tq)]

    def update(s_t, m_cur, v_t):
        _softmax_update(s_t, m_cur, v_t, m_sc, l_sc, acc_sc)

    def mask(s_t):
        key = lax.broadcasted_iota(I32, s_t.shape, 0)
        qry = lax.broadcasted_iota(I32, s_t.shape, 1)
        allowed = lax.shift_right_logical(key, chunk_shift) <= lax.shift_right_logical(qry, chunk_shift)
        return jnp.where(allowed, s_t, NEG_INF)

    _pipelined_sweep(qi, scores, values, update, mask, ((sa_sc, ma_sc), (sb_sc, mb_sc)))
    o_ref[...] = (acc_sc[...] / l_sc[...]).T.astype(o_ref.dtype)


def _mla_attention(q, k, v_t, batch, seq, heads):
    tq = _tile(seq, ATTN_TILE)
    assert tq % CHUNK == 0 and CHUNK & (CHUNK - 1) == 0
    nq = seq // tq
    return pl.pallas_call(
        functools.partial(_mla_attn_kernel, tq=tq, chunk_shift=CHUNK.bit_length() - 1),
        grid=(batch, heads, nq),
        in_specs=[pl.BlockSpec((tq, MLA_QK_PAD), lambda b, h, i: (b * nq + i, h)),
                  pl.BlockSpec((seq, MLA_QK_PAD), lambda b, h, i: (b, h)),
                  pl.BlockSpec((MLA_V, seq), lambda b, h, i: (h, b))],
        out_specs=pl.BlockSpec((tq, MLA_V), lambda b, h, i: (b * nq + i, h)),
        out_shape=jax.ShapeDtypeStruct((batch * seq, heads * MLA_V), BF16),
        scratch_shapes=[pltpu.VMEM((1, tq), F32), pltpu.VMEM((1, tq), F32), pltpu.VMEM((MLA_V, tq), F32),
                        pltpu.VMEM((tq, tq), F32), pltpu.VMEM((1, tq), F32),
                        pltpu.VMEM((tq, tq), F32), pltpu.VMEM((1, tq), F32),
                        pltpu.VMEM((MLA_QK_PAD, tq), BF16)],
        compiler_params=_params("parallel", "parallel", "arbitrary"),
        name="mla_attention",
    )(q, k, v_t)


def _fox_attn_kernel(q_ref, k_ref, vt_ref, c_ref, g_ref, o_ref, m_sc, l_sc, acc_sc, ck_sc, sa_sc, ma_sc,
                     sb_sc, mb_sc, qt_sc, *, tq):
    qi = pl.program_id(2)
    qt_sc[...] = q_ref[...].T
    _init_softmax(m_sc, l_sc, acc_sc)
    seq = ck_sc.shape[0]

    @pl.when(qi == 0)
    def _():
        def fill(blk, carry):
            b0 = pl.multiple_of(blk * LANES, LANES)
            row = c_ref[0, 0, :, pl.ds(b0, LANES)] * LOG2_E
            ck_sc[pl.ds(b0, LANES), :] = jnp.broadcast_to(row, (LANES, LANES)).T
            return carry

        lax.fori_loop(0, seq // LANES, fill, 0)

    q0 = pl.multiple_of(qi * tq, tq)
    c_q = c_ref[0, 0, :, pl.ds(q0, tq)] * LOG2_E

    def scores(kt):
        k0 = pl.multiple_of(kt * tq, tq)
        c_k = ck_sc[pl.ds(k0, tq), :]
        return _dot(k_ref[pl.ds(k0, tq), :], qt_sc[...]) - jnp.concatenate([c_k] * (tq // LANES), axis=1)

    def values(kt):
        return vt_ref[:, pl.ds(pl.multiple_of(kt * tq, tq), tq)]

    def update(s_t, m_cur, v_t):
        _softmax_update(s_t, m_cur, v_t, m_sc, l_sc, acc_sc, col_shift=c_q)

    def mask(s_t):
        key = lax.broadcasted_iota(I32, s_t.shape, 0)
        qry = lax.broadcasted_iota(I32, s_t.shape, 1)
        return jnp.where(key <= qry, s_t, NEG_INF)

    _pipelined_sweep(qi, scores, values, update, mask, ((sa_sc, ma_sc), (sb_sc, mb_sc)))
    o_ref[...] = ((acc_sc[...] / l_sc[...]).T * g_ref[...].astype(F32)).astype(o_ref.dtype)


def _fox_attention(qk, v_t, c_rows, gate, batch, seq, heads):
    tq = _tile(seq, ATTN_TILE)
    assert tq % LANES == 0 or tq == seq
    nq = seq // tq
    dh = FOX_HEAD_DIM
    return pl.pallas_call(
        functools.partial(_fox_attn_kernel, tq=tq),
        grid=(batch, heads, nq),
        in_specs=[pl.BlockSpec((tq, dh), lambda b, h, i: (b * nq + i, h)),
                  pl.BlockSpec((seq, dh), lambda b, h, i: (b, heads + h)),
                  pl.BlockSpec((dh, seq), lambda b, h, i: (h, b)),
                  pl.BlockSpec((1, 1, 1, seq), lambda b, h, i: (b, h, 0, 0)),
                  pl.BlockSpec((tq, dh), lambda b, h, i: (b * nq + i, h))],
        out_specs=pl.BlockSpec((tq, dh), lambda b, h, i: (b * nq + i, h)),
        out_shape=jax.ShapeDtypeStruct((batch * seq, heads * dh), BF16),
        scratch_shapes=[pltpu.VMEM((1, tq), F32), pltpu.VMEM((1, tq), F32), pltpu.VMEM((dh, tq), F32),
                        pltpu.VMEM((seq, LANES), F32),
                        pltpu.VMEM((tq, tq), F32), pltpu.VMEM((1, tq), F32),
                        pltpu.VMEM((tq, tq), F32), pltpu.VMEM((1, tq), F32),
                        pltpu.VMEM((dh, tq), BF16)],
        compiler_params=_params("parallel", "parallel", "arbitrary"),
        name="fox_attention",
    )(qk, qk, v_t, c_rows, gate)


def _fox_qk_kernel(x_ref, w_ref, g_ref, o_ref):
    y = _dot(x_ref[...], w_ref[...])
    g = g_ref[...]
    for hd in range(y.shape[1] // FOX_HEAD_DIM):
        sl = slice(hd * FOX_HEAD_DIM, (hd + 1) * FOX_HEAD_DIM)
        o_ref[:, sl] = _rms(y[:, sl], g[:, sl]).astype(o_ref.dtype)


def _fox_qk(a, w_qk, gain_row):
    t, d = a.shape
    n = w_qk.shape[1]
    tm, tn = _tile(t, ROW_TILE), _tile(n, COL_TILE)
    return pl.pallas_call(
        _fox_qk_kernel,
        grid=(t // tm, n // tn),
        in_specs=[pl.BlockSpec((tm, d), lambda i, j: (i, 0)), pl.BlockSpec((d, tn), lambda i, j: (0, j)),
                  pl.BlockSpec((1, tn), lambda i, j: (0, j))],
        out_specs=pl.BlockSpec((tm, tn), lambda i, j: (i, j)),
        out_shape=jax.ShapeDtypeStruct((t, n), BF16),
        compiler_params=_params("parallel", "parallel"),
        name="fox_qk",
    )(a, w_qk, gain_row)


def _fox_forget_kernel(h_ref, g_ref, w_ref, b_ref, c_ref, carry_sc, *, tiles_per_seq):
    i = pl.program_id(0)

    @pl.when(i % tiles_per_seq == 0)
    def _():
        carry_sc[...] = jnp.zeros(carry_sc.shape, F32)

    z = _split_dot(_rms(h_ref[...], g_ref[...]), w_ref) + b_ref[...]
    log_f = jnp.minimum(z, 0.0) - jnp.log(1.0 + jnp.exp(-jnp.abs(z)))

    tm = log_f.shape[0]
    row = lax.broadcasted_iota(I32, (tm, tm), 0)
    col = lax.broadcasted_iota(I32, (tm, tm), 1)
    tri = (col <= row).astype(BF16)
    sums = _dot(tri, jnp.concatenate(_split(log_f, 3) + [jnp.zeros_like(log_f, BF16)], axis=1))
    c = (sums[:, :LANES] + (sums[:, LANES:2 * LANES] + sums[:, 2 * LANES:3 * LANES])) + carry_sc[...]
    carry_sc[...] = c[tm - 1:tm, :]
    c_ref[0] = c.T


def _fox_forget_cumsum(h, gain, w_f, b_f, batch, seq):
    t, d = h.shape
    heads = w_f.shape[1]
    assert heads <= LANES
    tm = _tile(seq, ROUTE_TILE)
    w12 = jnp.concatenate(_split(jnp.pad(w_f, ((0, 0), (0, LANES - heads))), 2), axis=1)
    b = jnp.pad(b_f, (0, LANES - heads)).reshape(1, LANES)
    tps = seq // tm
    return pl.pallas_call(
        functools.partial(_fox_forget_kernel, tiles_per_seq=tps),
        grid=(t // tm,),
        in_specs=[pl.BlockSpec((tm, d), lambda i: (i, 0)), pl.BlockSpec((1, d), lambda i: (0, 0)),
                  pl.BlockSpec((d, 2 * LANES), lambda i: (0, 0)), pl.BlockSpec((1, LANES), lambda i: (0, 0))],
        out_specs=pl.BlockSpec((1, LANES, tm), lambda i: (i // tps, 0, i % tps)),
        out_shape=jax.ShapeDtypeStruct((batch, LANES, seq), F32),
        scratch_shapes=[pltpu.VMEM((1, LANES), F32)],
        compiler_params=_params("arbitrary"),
        name="fox_forget_cumsum",
    )(h, gain.reshape(1, d), w12, b)


def _router_kernel(h_ref, g_ref, w_ref, b_ref, t_ref, meta_ref, metat_ref, cnt_ref, carry_sc):
    i = pl.program_id(0)

    @pl.when(i == 0)
    def _():
        carry_sc[...] = jnp.zeros(carry_sc.shape, F32)

    t = _rms(h_ref[...], g_ref[...])
    _to_slabs(t_ref, _pack_halves(t))

    logits = _split_dot(t, w_ref) + b_ref[...]
    tm = logits.shape[0]
    lane = lax.broadcasted_iota(I32, logits.shape, 1)
    lane_f = lane.astype(F32)
    first = lambda hit: jnp.min(jnp.where(hit, lane_f, float(LANES)), axis=1, keepdims=True).astype(I32)

    is_grp = lane < MOE_GROUPS
    gl = jnp.where(is_grp, logits, -jnp.inf)
    gmax = jnp.max(gl, axis=1, keepdims=True)
    g_sel = first(gl == gmax)
    gexp = jnp.where(is_grp, jnp.exp(logits - gmax), 0.0)
    g_w = 1.0 / jnp.sum(gexp, axis=1, keepdims=True)

    lo = MOE_GROUPS + MOE_EXPERTS_PER_GROUP * g_sel
    in_grp = jnp.logical_and(lane >= lo, lane < lo + MOE_EXPERTS_PER_GROUP)
    el = jnp.where(in_grp, logits, -jnp.inf)
    emax = jnp.max(el, axis=1, keepdims=True)
    eexp = jnp.where(in_grp, jnp.exp(logits - emax), 0.0)
    prob = eexp / jnp.sum(eexp, axis=1, keepdims=True)
    cand1 = jnp.where(in_grp, prob, -1.0)
    p1 = jnp.max(cand1, axis=1, keepdims=True)
    j1 = first(cand1 == p1)
    cand2 = jnp.where(lane == j1, -1.0, cand1)
    p2 = jnp.max(cand2, axis=1, keepdims=True)
    j2 = first(cand2 == p2)
    denom = p1 + p2
    wt1 = p1 / denom * g_w
    wt2 = p2 / denom * g_w
    e1 = j1 - MOE_GROUPS
    e2 = j2 - MOE_GROUPS

    hit1 = lane == e1
    hit2 = lane == e2
    row = lax.broadcasted_iota(I32, (tm, tm), 0)
    col = lax.broadcasted_iota(I32, (tm, tm), 1)
    before = (col < row).astype(BF16)
    pre = _dot(before, jnp.concatenate([hit1.astype(BF16), hit2.astype(BF16)], axis=1))
    pre1, pre2 = pre[:, :LANES], pre[:, LANES:]
    carry = carry_sc[...]
    cnt1 = jnp.sum(hit1.astype(F32), axis=0, keepdims=True)
    cnt2 = jnp.sum(hit2.astype(F32), axis=0, keepdims=True)
    rank1 = jnp.sum(jnp.where(hit1, pre1 + carry, 0.0), axis=1, keepdims=True)
    rank2 = jnp.sum(jnp.where(hit2, pre2 + (carry + cnt1), 0.0), axis=1, keepdims=True)
    total = carry + cnt1 + cnt2
    carry_sc[...] = total
    cnt_ref[...] = jnp.broadcast_to(total, cnt_ref.shape)

    bits = lambda x: lax.bitcast_convert_type(jnp.broadcast_to(x, logits.shape), I32)
    meta = jnp.where(lane == 0, e1, 0)
    meta = jnp.where(lane == 1, e2, meta)
    meta = jnp.where(lane == 2, rank1.astype(I32), meta)
    meta = jnp.where(lane == 3, rank2.astype(I32), meta)
    meta = jnp.where(lane == 4, bits(wt1), meta)
    meta = jnp.where(lane == 5, bits(wt2), meta)
    meta_ref[...] = meta
    metat_ref[...] = meta.T[:SUBLANES, :]


def _router(h, gain, w_grp, b_grp, w_rt, b_rt):
    t, d = h.shape
    tm = _tile(t, ROUTE_TILE)
    slab = d // 2 // LANES
    n_used = MOE_GROUPS + MOE_EXPERTS
    w = jnp.pad(jnp.concatenate([w_grp, w_rt], axis=1), ((0, 0), (0, LANES - n_used)))
    w12 = jnp.concatenate(_split(w, 2), axis=1)
    b = jnp.pad(jnp.concatenate([b_grp, b_rt]), (0, LANES - n_used)).reshape(1, LANES)
    return pl.pallas_call(
        _router_kernel,
        grid=(t // tm,),
        in_specs=[pl.BlockSpec((tm, d), lambda i: (i, 0)), pl.BlockSpec((1, d), lambda i: (0, 0)),
                  pl.BlockSpec((d, 2 * LANES), lambda i: (0, 0)), pl.BlockSpec((1, LANES), lambda i: (0, 0))],
        out_specs=[pl.BlockSpec((tm * slab, LANES), lambda i: (i, 0)),
                   pl.BlockSpec((tm, LANES), lambda i: (i, 0)),
                   pl.BlockSpec((SUBLANES, tm), lambda i: (0, i)),
                   pl.BlockSpec((SUBLANES, LANES), lambda i: (0, 0))],
        out_shape=[jax.ShapeDtypeStruct((t * slab, LANES), U32),
                   jax.ShapeDtypeStruct((t, LANES), I32),
                   jax.ShapeDtypeStruct((SUBLANES, t), I32),
                   jax.ShapeDtypeStruct((SUBLANES, LANES), F32)],
        scratch_shapes=[pltpu.VMEM((1, LANES), F32)],
        compiler_params=_params("arbitrary"),
        name="moe_router",
    )(h, gain.reshape(1, d), w12, b)


def _dest_kernel(offs_ref, mt_ref, dest_ref):
    mt = mt_ref[...]
    experts = mt[0:2, :]
    base = jnp.zeros(experts.shape, I32)
    for e in range(MOE_EXPERTS):
        base = jnp.where(experts == e, offs_ref[e], base)
    row = lax.broadcasted_iota(I32, mt.shape, 0)
    dest_ref[...] = jnp.where(row < 2, jnp.concatenate([base + mt[2:4, :], mt[2:SUBLANES, :]], axis=0), 0)


def _dest_rows_all(meta_t, offs):
    rows, t = meta_t.shape
    return pl.pallas_call(
        _dest_kernel,
        grid_spec=pltpu.PrefetchScalarGridSpec(
            num_scalar_prefetch=1,
            grid=(1,),
            in_specs=[pl.BlockSpec((rows, t), lambda i, offs: (0, 0))],
            out_specs=pl.BlockSpec((rows, t), lambda i, offs: (0, 0)),
        ),
        out_shape=jax.ShapeDtypeStruct((rows, t), I32),
        compiler_params=_params("arbitrary"),
        name="moe_dest",
    )(offs, meta_t)


def _slab_copy(src, src_tok, dst, dst_tok, sem, slab):
    rows = lambda tok: pl.ds(pl.multiple_of(tok * slab, slab), slab)
    return pltpu.make_async_copy(src.at[rows(src_tok)], dst.at[rows(dst_tok)], sem)


_ROW_UNROLL = 8


def _dispatch_kernel(zs_ref, nv_ref, dest_ref, t_ref, xs_ref, zero_sc, sem, *, tm, tz, slab):
    i = pl.program_id(0)

    @pl.when(i == 0)
    def _():
        zero_sc[...] = jnp.zeros(zero_sc.shape, zero_sc.dtype)
        fill = lambda tok0: pltpu.make_async_copy(
            zero_sc, xs_ref.at[pl.ds(pl.multiple_of(tok0 * slab, tz * slab), tz * slab)], sem)
        fills = [fill(zs_ref[e]) for e in range(MOE_EXPERTS)]
        for c in fills:
            c.start()
        for c in fills:
            c.wait()

        def tail(tile, carry):
            c = fill(tile * tz)
            c.start()
            c.wait()
            return carry

        lax.fori_loop(nv_ref[0], xs_ref.shape[0] // (tz * slab), tail, 0)

    def issue(blk, carry):
        for u in range(_ROW_UNROLL):
            r = blk * _ROW_UNROLL + u
            _slab_copy(t_ref, r, xs_ref, dest_ref[0, r], sem, slab).start()
            _slab_copy(t_ref, r, xs_ref, dest_ref[1, r], sem, slab).start()
        return carry

    lax.fori_loop(0, tm // _ROW_UNROLL, issue, 0)

    def drain(blk, carry):
        for u in range(2 * _ROW_UNROLL):
            _slab_copy(t_ref, 0, xs_ref, 0, sem, slab).wait()
        return carry

    lax.fori_loop(0, tm // _ROW_UNROLL, drain, 0)


def _dispatch(t_slabs, dest_t, zero_start, n_valid, n_rows):
    t = dest_t.shape[1]
    slab = t_slabs.shape[0] // t
    tm = _tile(t, MOVE_TILE)
    tz = EXPERT_TILE
    mspec = pl.BlockSpec((SUBLANES, tm), lambda i, *_: (0, i), memory_space=pltpu.SMEM)
    return pl.pallas_call(
        functools.partial(_dispatch_kernel, tm=tm, tz=tz, slab=slab),
        grid_spec=pltpu.PrefetchScalarGridSpec(
            num_scalar_prefetch=2,
            grid=(t // tm,),
            in_specs=[mspec, pl.BlockSpec((tm * slab, LANES), lambda i, *_: (i, 0))],
            out_specs=pl.BlockSpec(memory_space=pl.ANY),
            scratch_shapes=[pltpu.VMEM((tz * slab, LANES), U32), pltpu.SemaphoreType.DMA(())],
        ),
        out_shape=jax.ShapeDtypeStruct((n_rows * slab, LANES), U32),
        compiler_params=_params("arbitrary"),
        name="moe_dispatch",
    )(zero_start, n_valid, dest_t, t_slabs)


def _experts_kernel(te_ref, nv_ref, x_ref, wg_ref, wu_ref, wd_ref, y_ref, wg_sc, wu_sc, wd_sc, *, tm):
    i = pl.program_id(0)
    live = i < nv_ref[0]

    @pl.when(jnp.logical_or(i == 0, te_ref[i] != te_ref[jnp.maximum(i - 1, 0)]))
    def _():
        wg_sc[...] = wg_ref[0, 0].astype(BF16)
        wu_sc[...] = wu_ref[0, 0].astype(BF16)
        wd_sc[...] = wd_ref[0, 0].astype(BF16)

    @pl.when(jnp.logical_not(live))
    def _():
        y_ref[...] = jnp.zeros(y_ref.shape, y_ref.dtype)

    @pl.when(live)
    def _():
        hi, lo = _unpack_halves(_from_slabs(x_ref, tm))
        x = jnp.concatenate([hi, lo], axis=1).astype(BF16)
        g = _dot(x, wg_sc[...])
        u = _dot(x, wu_sc[...])
        hid = (g * jax.nn.sigmoid(g) * u).astype(BF16)
        _to_slabs(y_ref, _pack_halves(_dot(hid, wd_sc[...])))


def _experts(xs, tile_expert, n_valid, w_gate, w_up, w_down, layer, n_tiles):
    tm = EXPERT_TILE
    _, _, d, f = w_gate.shape
    rows = tm * (d // 2 // LANES)
    live = lambda i, nv: jnp.minimum(i, nv[0] - 1)
    return pl.pallas_call(
        functools.partial(_experts_kernel, tm=tm),
        grid_spec=pltpu.PrefetchScalarGridSpec(
            num_scalar_prefetch=2,
            grid=(n_tiles,),
            in_specs=[pl.BlockSpec((rows, LANES), lambda i, te, nv: (live(i, nv), 0)),
                      pl.BlockSpec((1, 1, d, f), lambda i, te, nv: (layer, te[i], 0, 0)),
                      pl.BlockSpec((1, 1, d, f), lambda i, te, nv: (layer, te[i], 0, 0)),
                      pl.BlockSpec((1, 1, f, d), lambda i, te, nv: (layer, te[i], 0, 0))],
            out_specs=pl.BlockSpec((rows, LANES), lambda i, te, nv: (i, 0)),
            scratch_shapes=[pltpu.VMEM((d, f), BF16), pltpu.VMEM((d, f), BF16), pltpu.VMEM((f, d), BF16)],
        ),
        out_shape=jax.ShapeDtypeStruct((n_tiles * rows, LANES), U32),
        compiler_params=_params("arbitrary"),
        name="moe_experts",
    )(tile_expert, n_valid, xs, w_gate, w_up, w_down)


def _combine_kernel(dest_ref, dnext_ref, meta_ref, h_ref, g_ref, ys_ref, *rest, tm, slab, final):
    if final:
        o_ref, y_sc, sem = rest
    else:
        hn_ref, a_ref, y_sc, sem = rest
    i = pl.program_id(0)
    n = pl.num_programs(0)

    def gather(dref, slot):
        def issue(blk, carry):
            for u in range(_ROW_UNROLL):
                r = blk * _ROW_UNROLL + u
                _slab_copy(ys_ref, dref[0, r], y_sc.at[slot, 0], r, sem.at[slot], slab).start()
                _slab_copy(ys_ref, dref[1, r], y_sc.at[slot, 1], r, sem.at[slot], slab).start()
            return carry

        lax.fori_loop(0, tm // _ROW_UNROLL, issue, 0)

    def finish(slot):
        def drain(blk, carry):
            for u in range(2 * _ROW_UNROLL):
                _slab_copy(ys_ref, 0, y_sc.at[slot, 0], 0, sem.at[slot], slab).wait()
            return carry

        lax.fori_loop(0, tm // _ROW_UNROLL, drain, 0)

        meta = meta_ref[...]
        lane = lax.broadcasted_iota(I32, meta.shape, 1)
        wbits = lax.bitcast_convert_type(meta, F32)
        wt1 = jnp.sum(jnp.where(lane == 4, wbits, 0.0), axis=1, keepdims=True)
        wt2 = jnp.sum(jnp.where(lane == 5, wbits, 0.0), axis=1, keepdims=True)
        hi1, lo1 = _unpack_halves(_from_slabs(y_sc.at[slot, 0], tm))
        hi2, lo2 = _unpack_halves(_from_slabs(y_sc.at[slot, 1], tm))
        moe = jnp.concatenate([wt1 * hi1 + wt2 * hi2, wt1 * lo1 + wt2 * lo2], axis=1)
        h_new = h_ref[...] + moe
        if final:
            o_ref[...] = _rms(h_new, g_ref[...])
        else:
            hn_ref[...] = h_new
            a_ref[...] = _rms(h_new, g_ref[...]).astype(a_ref.dtype)

    @pl.when(i == 0)
    def _():
        gather(dest_ref, 0)

    for slot in (0, 1):
        @pl.when((i & 1) == slot)
        def _():
            @pl.when(i + 1 < n)
            def _():
                gather(dnext_ref, 1 - slot)

            finish(slot)


def _combine(ys, dest_t, meta, h, gain, final):
    t, d = h.shape
    tm = _tile(t, MOVE_TILE)
    slab = d // 2 // LANES
    n = t // tm
    mspec = pl.BlockSpec((SUBLANES, tm), lambda i: (0, i), memory_space=pltpu.SMEM)
    mnext = pl.BlockSpec((SUBLANES, tm), lambda i: (0, jnp.minimum(i + 1, n - 1)), memory_space=pltpu.SMEM)
    row = pl.BlockSpec((tm, d), lambda i: (i, 0))
    if final:
        out_specs, out_shape = row, jax.ShapeDtypeStruct((t, d), F32)
    else:
        out_specs = [row, row]
        out_shape = [jax.ShapeDtypeStruct((t, d), F32), jax.ShapeDtypeStruct((t, d), BF16)]
    return pl.pallas_call(
        functools.partial(_combine_kernel, tm=tm, slab=slab, final=final),
        grid=(n,),
        in_specs=[mspec, mnext, pl.BlockSpec((tm, LANES), lambda i: (i, 0)), row,
                  pl.BlockSpec((1, d), lambda i: (0, 0)), pl.BlockSpec(memory_space=pl.ANY)],
        out_specs=out_specs,
        out_shape=out_shape,
        scratch_shapes=[pltpu.VMEM((2, 2, tm * slab, LANES), U32), pltpu.SemaphoreType.DMA((2,))],
        compiler_params=_params("arbitrary"),
        name="moe_combine",
    )(dest_t, dest_t, meta, h, gain.reshape(1, d), ys)


def _hier_moe(h, ffn_gain, w_grp, b_grp, w_rt, b_rt, w_gate, w_up, w_down, layer, next_gain, final):
    t, d = h.shape
    tmx = EXPERT_TILE
    t_slabs, meta, meta_t, cnt = _router(h, ffn_gain, w_grp, b_grp, w_rt, b_rt)

    counts = cnt[0, :MOE_EXPERTS].astype(I32)
    padded = (counts + tmx - 1) // tmx * tmx
    ends = jnp.cumsum(padded)
    offs = ends - padded
    n_tiles = (2 * t) // tmx + MOE_EXPERTS
    n_valid = (ends[-1] // tmx).reshape(1)
    tile_start = jnp.minimum(jnp.arange(n_tiles, dtype=I32) * tmx, ends[-1] - 1)
    tile_expert = jnp.sum((ends[None, :] <= tile_start[:, None]).astype(I32), axis=1)
    zero_start = offs + counts // tmx * tmx

    dest_t = _dest_rows_all(meta_t, offs)
    xs = _dispatch(t_slabs, dest_t, zero_start, n_valid, (n_tiles + 1) * tmx)
    ys = _experts(xs, tile_expert, n_valid, w_gate, w_up, w_down, layer, n_tiles)
    return _combine(ys, dest_t, meta, h, next_gain, final)


def kernel(x, positions, attn_norm, ffn_norm, final_norm, mla_w_dq, mla_q_norm, mla_w_uq, mla_w_dkv,
           mla_kv_norm, mla_w_ukv, mla_w_o, fox_w_qkv, fox_q_norm, fox_k_norm, fox_w_f, fox_b_f,
           fox_w_og, fox_w_o, moe_w_grp, moe_b_grp, moe_w_rt, moe_b_rt, moe_w_gate, moe_w_up, moe_w_down):
    batch, seq, d = x.shape
    depth = attn_norm.shape[0]
    t = batch * seq
    cos_t, sin_t = _rope_tables(positions)
    h = x.reshape(t, d)
    a = _norm(h, attn_norm[0], BF16)
    out = None
    for i in range(depth):
        j = i // 2
        if i % 2 == 0:
            heads = mla_w_uq.shape[2] // (MLA_NOPE + MLA_ROPE)
            q, k, v_t = _mla_project(a, cos_t, sin_t, mla_w_dq[j], mla_q_norm[j], mla_w_uq[j],
                                     mla_w_dkv[j], mla_kv_norm[j], mla_w_ukv[j])
            o = _mla_attention(q, k, v_t, batch, seq, heads)
            w_o = mla_w_o[j]
        else:
            dh = FOX_HEAD_DIM
            heads = fox_w_qkv.shape[2] // (3 * dh)
            w_qkv = fox_w_qkv[j].astype(BF16)
            gain_row = jnp.concatenate([jnp.tile(fox_q_norm[j] * (dh ** -0.5 * LOG2_E), heads),
                                        jnp.tile(fox_k_norm[j], heads)])[None, :]
            qk = _fox_qk(a, w_qkv[:, :2 * heads * dh], gain_row)
            v_t = _mm_nt(w_qkv[:, 2 * heads * dh:].T, a, BF16, name="fox_v")
            gate = _mm(a, fox_w_og[j].astype(BF16), BF16, act="sigmoid", name="fox_gate")
            c_t = _fox_forget_cumsum(h, attn_norm[i], fox_w_f[j], fox_b_f[j], batch, seq)
            c_rows = c_t[:, :heads, :].reshape(batch, heads, 1, seq)
            o = _fox_attention(qk, v_t, c_rows, gate, batch, seq, heads)
            w_o = fox_w_o[j]
        h = _mm_residual(o, w_o.astype(BF16), h)
        final = i == depth - 1
        next_gain = final_norm if final else attn_norm[i + 1]
        res = _hier_moe(h, ffn_norm[i], moe_w_grp[i], moe_b_grp[i], moe_w_rt[i], moe_b_rt[i],
                        moe_w_gate, moe_w_up, moe_w_down, i, next_gain, final)
        if final:
            out = res
        else:
            h, a = res
    return out.reshape(batch, seq, d)
```

```python
import functools

import jax
import jax.numpy as jnp
from jax import lax
from jax.experimental import pallas as pl
from jax.experimental.pallas import tpu as pltpu

F32 = jnp.float32
BF16 = jnp.bfloat16
I32 = jnp.int32
U32 = jnp.uint32

RMS_EPS = 1e-6
NEG_INF = -1e30
CHUNK = 64
MLA_NOPE = 128
MLA_ROPE = 64
MLA_V = 128
MLA_QK_PAD = 256
ROPE_THETA = 10000.0
LOG2_E = 1.4426950408889634
FOX_HEAD_DIM = 128
MOE_GROUPS = 8
MOE_EXPERTS_PER_GROUP = 4
MOE_EXPERTS = MOE_GROUPS * MOE_EXPERTS_PER_GROUP

LANES = 128
SUBLANES = 8
V7X_VMEM_LIMIT_BYTES = 56 * 1024 * 1024

ROW_TILE = 1024
COL_TILE = 1024
ATTN_TILE = 512
ROUTE_TILE = 512
MOVE_TILE = 256
EXPERT_TILE = 256


def _params(*sem):
    return pltpu.CompilerParams(dimension_semantics=sem, vmem_limit_bytes=V7X_VMEM_LIMIT_BYTES)


def _tile(n, t):
    if n <= t:
        return n
    step = LANES if t % LANES == 0 else SUBLANES
    for c in range(t - t % step, 0, -step):
        if n % c == 0:
            return c
    raise ValueError(f"no aligned tile for {n} under {t}")


def _rms(x, gain):
    ms = jnp.mean(x * x, axis=-1, keepdims=True)
    return x * lax.rsqrt(ms + RMS_EPS) * gain


def _dot(a, b):
    return jnp.dot(a, b, preferred_element_type=F32)


def _nt_dot(a, b):
    return lax.dot_general(a, b, (((1,), (1,)), ((), ())), preferred_element_type=F32)


def _pack_halves(x):
    n = x.shape[1] // 2
    hi = lax.bitcast_convert_type(x[:, :n].astype(BF16).astype(F32), U32)
    lo = lax.bitcast_convert_type(x[:, n:].astype(BF16).astype(F32), U32)
    return hi | lax.shift_right_logical(lo, jnp.uint32(16))


def _unpack_halves(p):
    hi = lax.bitcast_convert_type(p & jnp.uint32(0xFFFF0000), F32)
    lo = lax.bitcast_convert_type(lax.shift_left(p, jnp.uint32(16)), F32)
    return hi, lo


def _to_slabs(ref, x):
    m, width = x.shape
    c = width // LANES
    for j in range(c):
        ref[pl.ds(j, m, stride=c), :] = x[:, j * LANES:(j + 1) * LANES]


def _from_slabs(ref, m):
    c = ref.shape[0] // m
    return jnp.concatenate([ref[pl.ds(j, m, stride=c), :] for j in range(c)], axis=1)


def _split(x, terms):
    out = []
    for _ in range(terms):
        hi = x.astype(BF16)
        out.append(hi)
        x = x - hi.astype(F32)
    return out


def _split_dot(x, w12_ref):
    m = x.shape[0]
    n = w12_ref.shape[1] // 2
    prod = _dot(jnp.concatenate(_split(x, 2), axis=0), w12_ref[...])
    return (prod[:m, :n] + (prod[:m, n:] + prod[m:, :n])) + prod[m:, n:]


def _rope_lanes(seg, cos_t, sin_t):
    half = MLA_ROPE // 2
    lane = lax.broadcasted_iota(I32, seg.shape, 1)
    swapped = jnp.where(lane < half, pltpu.roll(seg, LANES - half, 1), pltpu.roll(seg, half, 1))
    return seg * cos_t + swapped * sin_t


def _rope_table_kernel(pos_ref, freq_ref, cos_ref, sin_ref):
    ang = pos_ref[...] * freq_ref[...]
    lane = lax.broadcasted_iota(I32, ang.shape, 1)
    half = MLA_ROPE // 2
    valid = lane < MLA_ROPE
    cos_ref[...] = jnp.where(valid, jnp.cos(ang), 0.0)
    sin_ref[...] = jnp.where(valid, jnp.where(lane < half, -jnp.sin(ang), jnp.sin(ang)), 0.0)


def _rope_tables(positions):
    t = positions.size
    tm = _tile(t, ROW_TILE)
    half = MLA_ROPE // 2
    inv_freq = ROPE_THETA ** (-jnp.arange(0, MLA_ROPE, 2, dtype=F32) / MLA_ROPE)
    freq_row = jnp.concatenate([inv_freq, inv_freq, jnp.zeros((LANES - 2 * half,), F32)])[None, :]
    pos = positions.reshape(t, 1).astype(F32)
    out = jax.ShapeDtypeStruct((t, LANES), F32)
    return pl.pallas_call(
        _rope_table_kernel,
        grid=(t // tm,),
        in_specs=[pl.BlockSpec((tm, 1), lambda i: (i, 0)), pl.BlockSpec((1, LANES), lambda i: (0, 0))],
        out_specs=[pl.BlockSpec((tm, LANES), lambda i: (i, 0))] * 2,
        out_shape=[out, out],
        compiler_params=_params("parallel"),
        name="rope_tables",
    )(pos, freq_row)


def _norm_kernel(h_ref, g_ref, a_ref):
    a_ref[...] = _rms(h_ref[...], g_ref[...]).astype(a_ref.dtype)


def _norm(h, gain, out_dtype):
    t, d = h.shape
    tm = _tile(t, ROUTE_TILE)
    return pl.pallas_call(
        _norm_kernel,
        grid=(t // tm,),
        in_specs=[pl.BlockSpec((tm, d), lambda i: (i, 0)), pl.BlockSpec((1, d), lambda i: (0, 0))],
        out_specs=pl.BlockSpec((tm, d), lambda i: (i, 0)),
        out_shape=jax.ShapeDtypeStruct((t, d), out_dtype),
        compiler_params=_params("parallel"),
        name="rmsnorm",
    )(h, gain.reshape(1, d))


def _mm_kernel(x_ref, w_ref, o_ref, *, act):
    y = _dot(x_ref[...], w_ref[...])
    if act == "sigmoid":
        y = jax.nn.sigmoid(y)
    o_ref[...] = y.astype(o_ref.dtype)


def _mm(x, w, out_dtype, act=None, name="mm", col_tile=COL_TILE):
    m, k = x.shape
    n = w.shape[1]
    tm, tn = _tile(m, ROW_TILE), _tile(n, col_tile)
    return pl.pallas_call(
        functools.partial(_mm_kernel, act=act),
        grid=(m // tm, n // tn),
        in_specs=[pl.BlockSpec((tm, k), lambda i, j: (i, 0)), pl.BlockSpec((k, tn), lambda i, j: (0, j))],
        out_specs=pl.BlockSpec((tm, tn), lambda i, j: (i, j)),
        out_shape=jax.ShapeDtypeStruct((m, n), out_dtype),
        compiler_params=_params("parallel", "parallel"),
        name=name,
    )(x, w)


def _mm_nt_kernel(wt_ref, x_ref, o_ref):
    o_ref[...] = _nt_dot(wt_ref[...], x_ref[...]).astype(o_ref.dtype)


def _mm_nt(w_t, x, out_dtype, name):
    n, k = w_t.shape
    m = x.shape[0]
    tm, tn = _tile(m, ROW_TILE), _tile(n, COL_TILE)
    return pl.pallas_call(
        _mm_nt_kernel,
        grid=(m // tm, n // tn),
        in_specs=[pl.BlockSpec((tn, k), lambda i, j: (j, 0)), pl.BlockSpec((tm, k), lambda i, j: (i, 0))],
        out_specs=pl.BlockSpec((tn, tm), lambda i, j: (j, i)),
        out_shape=jax.ShapeDtypeStruct((n, m), out_dtype),
        compiler_params=_params("parallel", "parallel"),
        name=name,
    )(w_t, x)


def _mm_res_kernel(x_ref, w_ref, h_ref, o_ref):
    o_ref[...] = h_ref[...] + _dot(x_ref[...], w_ref[...])


def _mm_residual(x, w, h):
    m, k = x.shape
    n = w.shape[1]
    tm, tn = _tile(m, ROW_TILE), _tile(n, COL_TILE)
    return pl.pallas_call(
        _mm_res_kernel,
        grid=(m // tm, n // tn),
        in_specs=[pl.BlockSpec((tm, k), lambda i, j: (i, 0)), pl.BlockSpec((k, tn), lambda i, j: (0, j)),
                  pl.BlockSpec((tm, tn), lambda i, j: (i, j))],
        out_specs=pl.BlockSpec((tm, tn), lambda i, j: (i, j)),
        out_shape=jax.ShapeDtypeStruct((m, n), F32),
        compiler_params=_params("parallel", "parallel"),
        name="out_proj_residual",
    )(x, w, h)


def _mla_q_kernel(c_ref, g_ref, w_ref, cos_ref, sin_ref, q_ref, *, scale):
    cq = _rms(c_ref[...], g_ref[...]).astype(BF16)
    y = _dot(cq, w_ref[...]) * scale
    cos_t, sin_t = cos_ref[...], sin_ref[...]
    for hd in range(y.shape[1] // MLA_QK_PAD):
        base = hd * MLA_QK_PAD
        q_ref[:, base:base + MLA_NOPE] = y[:, base:base + MLA_NOPE].astype(q_ref.dtype)
        roped = _rope_lanes(y[:, base + MLA_NOPE:base + MLA_QK_PAD], cos_t, sin_t)
        q_ref[:, base + MLA_NOPE:base + MLA_QK_PAD] = roped.astype(q_ref.dtype)


def _mla_kv_kernel(c_ref, pe_ref, g_ref, wk_ref, wvt_ref, cos_ref, sin_ref, k_ref, vt_ref):
    ckv = _rms(c_ref[...], g_ref[...]).astype(BF16)
    kn = _dot(ckv, wk_ref[...])
    vt_ref[...] = _nt_dot(wvt_ref[...], ckv).astype(vt_ref.dtype)
    k_pe = _rope_lanes(pe_ref[...], cos_ref[...], sin_ref[...]).astype(k_ref.dtype)
    for hd in range(kn.shape[1] // MLA_NOPE):
        k_ref[:, hd * MLA_QK_PAD:hd * MLA_QK_PAD + MLA_NOPE] = (
            kn[:, hd * MLA_NOPE:(hd + 1) * MLA_NOPE].astype(k_ref.dtype))
        k_ref[:, hd * MLA_QK_PAD + MLA_NOPE:(hd + 1) * MLA_QK_PAD] = k_pe


def _mla_project(a, cos_t, sin_t, w_dq, q_norm, w_uq, w_dkv, kv_norm, w_ukv):
    t, d = a.shape
    q_lora = w_dq.shape[1]
    kv_lora = w_dkv.shape[1] - MLA_ROPE
    heads = w_uq.shape[1] // (MLA_NOPE + MLA_ROPE)
    assert q_lora % LANES == 0 and kv_lora % LANES == 0

    w_down = jnp.concatenate(
        [w_dq, w_dkv, jnp.zeros((d, LANES - MLA_ROPE), w_dkv.dtype)], axis=1).astype(BF16)
    s1 = _mm(a, w_down, F32, name="mla_down", col_tile=w_down.shape[1])

    w_q = w_uq.reshape(q_lora, heads, MLA_NOPE + MLA_ROPE)
    w_q = jnp.pad(w_q, ((0, 0), (0, 0), (0, MLA_QK_PAD - MLA_NOPE - MLA_ROPE)))
    w_q = w_q.reshape(q_lora, heads * MLA_QK_PAD).astype(BF16)
    w_kv = w_ukv.reshape(kv_lora, heads, MLA_NOPE + MLA_V)
    w_k = w_kv[:, :, :MLA_NOPE].reshape(kv_lora, heads * MLA_NOPE).astype(BF16)
    w_vt = w_kv[:, :, MLA_NOPE:].reshape(kv_lora, heads * MLA_V).T.astype(BF16)

    tm = _tile(t, ROW_TILE)
    scale = (MLA_NOPE + MLA_ROPE) ** -0.5 * LOG2_E
    tn = _tile(heads * MLA_QK_PAD, COL_TILE)
    row128 = pl.BlockSpec((tm, LANES), lambda i, j: (i, 0))
    q = pl.pallas_call(
        functools.partial(_mla_q_kernel, scale=scale),
        grid=(t // tm, heads * MLA_QK_PAD // tn),
        in_specs=[pl.BlockSpec((tm, q_lora), lambda i, j: (i, 0)),
                  pl.BlockSpec((1, q_lora), lambda i, j: (0, 0)),
                  pl.BlockSpec((q_lora, tn), lambda i, j: (0, j)), row128, row128],
        out_specs=pl.BlockSpec((tm, tn), lambda i, j: (i, j)),
        out_shape=jax.ShapeDtypeStruct((t, heads * MLA_QK_PAD), BF16),
        compiler_params=_params("parallel", "parallel"),
        name="mla_q",
    )(s1, q_norm.reshape(1, q_lora), w_q, cos_t, sin_t)

    hb = min(heads, COL_TILE // MLA_QK_PAD)
    kv_blk = q_lora // kv_lora
    assert q_lora % kv_lora == 0
    pe_blk = (q_lora + kv_lora) // LANES
    k, v_t = pl.pallas_call(
        _mla_kv_kernel,
        grid=(t // tm, heads // hb),
        in_specs=[pl.BlockSpec((tm, kv_lora), lambda i, j: (i, kv_blk)),
                  pl.BlockSpec((tm, LANES), lambda i, j: (i, pe_blk)),
                  pl.BlockSpec((1, kv_lora), lambda i, j: (0, 0)),
                  pl.BlockSpec((kv_lora, hb * MLA_NOPE), lambda i, j: (0, j)),
                  pl.BlockSpec((hb * MLA_V, kv_lora), lambda i, j: (j, 0)), row128, row128],
        out_specs=[pl.BlockSpec((tm, hb * MLA_QK_PAD), lambda i, j: (i, j)),
                   pl.BlockSpec((hb * MLA_V, tm), lambda i, j: (j, i))],
        out_shape=[jax.ShapeDtypeStruct((t, heads * MLA_QK_PAD), BF16),
                   jax.ShapeDtypeStruct((heads * MLA_V, t), BF16)],
        compiler_params=_params("parallel", "parallel"),
        name="mla_kv",
    )(s1, s1, kv_norm.reshape(1, kv_lora), w_k, w_vt, cos_t, sin_t)
    return q, k, v_t


def _col_max(x):
    while x.shape[0] > SUBLANES and x.shape[0] % (2 * SUBLANES) == 0:
        half = x.shape[0] // 2
        x = jnp.maximum(x[:half], x[half:])
    return jnp.max(x, axis=0, keepdims=True)


def _softmax_update(s_t, m_cur, v_t, m_sc, l_sc, acc_sc, col_shift=None):
    m_prev = m_sc[...]
    if col_shift is not None:
        m_cur = m_cur + col_shift
    m_new = jnp.maximum(m_prev, m_cur)
    shift = m_new if col_shift is None else m_new - col_shift
    p_t = jnp.exp2(s_t - shift)
    alpha = jnp.exp2(m_prev - m_new)
    l_sc[...] = alpha * l_sc[...] + jnp.sum(p_t, axis=0, keepdims=True)
    acc_sc[...] = alpha * acc_sc[...] + _dot(v_t, p_t.astype(v_t.dtype))
    m_sc[...] = m_new


def _init_softmax(m_sc, l_sc, acc_sc):
    m_sc[...] = jnp.full(m_sc.shape, NEG_INF, F32)
    l_sc[...] = jnp.zeros(l_sc.shape, F32)
    acc_sc[...] = jnp.zeros(acc_sc.shape, F32)


def _pipelined_sweep(qi, scores, values, update, mask, bufs):
    (s_a, m_a), (s_b, m_b) = bufs

    def produce(t, s_buf, m_buf):
        s_t = scores(t)
        s_buf[...] = s_t
        m_buf[...] = _col_max(s_t)

    def consume(t, s_buf, m_buf):
        update(s_buf[...], m_buf[...], values(t))

    def consume_diagonal(s_buf):
        s_t = mask(s_buf[...])
        update(s_t, _col_max(s_t), values(qi))

    produce(0, s_a, m_a)

    def pair(p, carry):
        produce(2 * p + 1, s_b, m_b)
        consume(2 * p, s_a, m_a)
        produce(2 * p + 2, s_a, m_a)
        consume(2 * p + 1, s_b, m_b)
        return carry

    lax.fori_loop(0, lax.shift_right_logical(qi, 1), pair, 0)

    @pl.when((qi & 1) == 0)
    def _():
        consume_diagonal(s_a)

    @pl.when((qi & 1) == 1)
    def _():
        produce(qi, s_b, m_b)
        consume(qi - 1, s_a, m_a)
        consume_diagonal(s_b)


def _mla_attn_kernel(q_ref, k_ref, vt_ref, o_ref, m_sc, l_sc, acc_sc, sa_sc, ma_sc, sb_sc, mb_sc, qt_sc, *,
                     tq, chunk_shift):
    qi = pl.program_id(2)
    qt_sc[...] = q_ref[...].T
    _init_softmax(m_sc, l_sc, acc_sc)

    def scores(kt):
        return _dot(k_ref[pl.ds(pl.multiple_of(kt * tq, tq), tq), :], qt_sc[...])

    def values(kt):
        return vt_ref[:, pl.ds(pl.multiple_of(kt * tq, tq), tq)]

    def update(s_t, m_cur, v_t):
        _softmax_update(s_t, m_cur, v_t, m_sc, l_sc, acc_sc)

    def mask(s_t):
        key = lax.broadcasted_iota(I32, s_t.shape, 0)
        qry = lax.broadcasted_iota(I32, s_t.shape, 1)
        allowed = lax.shift_right_logical(key, chunk_shift) <= lax.shift_right_logical(qry, chunk_shift)
        return jnp.where(allowed, s_t, NEG_INF)

    _pipelined_sweep(qi, scores, values, update, mask, ((sa_sc, ma_sc), (sb_sc, mb_sc)))
    o_ref[...] = (acc_sc[...] / l_sc[...]).T.astype(o_ref.dtype)


def _mla_attention(q, k, v_t, batch, seq, heads):
    tq = _tile(seq, ATTN_TILE)
    assert tq % CHUNK == 0 and CHUNK & (CHUNK - 1) == 0
    nq = seq // tq
    return pl.pallas_call(
        functools.partial(_mla_attn_kernel, tq=tq, chunk_shift=CHUNK.bit_length() - 1),
        grid=(batch, heads, nq),
        in_specs=[pl.BlockSpec((tq, MLA_QK_PAD), lambda b, h, i: (b * nq + i, h)),
                  pl.BlockSpec((seq, MLA_QK_PAD), lambda b, h, i: (b, h)),
                  pl.BlockSpec((MLA_V, seq), lambda b, h, i: (h, b))],
        out_specs=pl.BlockSpec((tq, MLA_V), lambda b, h, i: (b * nq + i, h)),
        out_shape=jax.ShapeDtypeStruct((batch * seq, heads * MLA_V), BF16),
        scratch_shapes=[pltpu.VMEM((1, tq), F32), pltpu.VMEM((1, tq), F32), pltpu.VMEM((MLA_V, tq), F32),
                        pltpu.VMEM((tq, tq), F32), pltpu.VMEM((1, tq), F32),
                        pltpu.VMEM((tq, tq), F32), pltpu.VMEM((1, tq), F32),
                        pltpu.VMEM((MLA_QK_PAD, tq), BF16)],
        compiler_params=_params("parallel", "parallel", "arbitrary"),
        name="mla_attention",
    )(q, k, v_t)


_FORGET_TERMS = 3


def _fox_attn_kernel(q_ref, k_ref, vt_ref, ctok_ref, crow_ref, g_ref, o_ref, m_sc, l_sc, acc_sc, ka_sc,
                     sa_sc, ma_sc, sb_sc, mb_sc, qt_sc, *, tq):
    head = pl.program_id(1)
    qi = pl.program_id(2)
    dh = k_ref.shape[1]
    _init_softmax(m_sc, l_sc, acc_sc)

    qt_sc[:dh, :] = q_ref[...].T
    row = lax.broadcasted_iota(I32, (LANES, tq), 0)
    qt_sc[dh:, :] = jnp.where(row < _FORGET_TERMS, -1.0, 0.0).astype(qt_sc.dtype)

    @pl.when(qi == 0)
    def _():
        ka_sc[:, :dh] = k_ref[...]
        terms = _split(ctok_ref[...] * LOG2_E, _FORGET_TERMS)
        src = lax.broadcasted_iota(I32, (LANES, LANES), 0)
        dst = lax.broadcasted_iota(I32, (LANES, LANES), 1)
        aug = None
        for j, term in enumerate(terms):
            pick = jnp.logical_and(src == head, dst == j).astype(BF16)
            part = _dot(term, pick)
            aug = part if aug is None else aug + part
        ka_sc[:, dh:] = aug.astype(ka_sc.dtype)

    q0 = pl.multiple_of(qi * tq, tq)
    c_q = crow_ref[0, 0, :, pl.ds(q0, tq)] * LOG2_E

    def scores(kt):
        return _dot(ka_sc[pl.ds(pl.multiple_of(kt * tq, tq), tq), :], qt_sc[...])

    def values(kt):
        return vt_ref[:, pl.ds(pl.multiple_of(kt * tq, tq), tq)]

    def update(s_t, m_cur, v_t):
        _softmax_update(s_t, m_cur, v_t, m_sc, l_sc, acc_sc, col_shift=c_q)

    def mask(s_t):
        key = lax.broadcasted_iota(I32, s_t.shape, 0)
        qry = lax.broadcasted_iota(I32, s_t.shape, 1)
        return jnp.where(key <= qry, s_t, NEG_INF)

    _pipelined_sweep(qi, scores, values, update, mask, ((sa_sc, ma_sc), (sb_sc, mb_sc)))
    o_ref[...] = ((acc_sc[...] / l_sc[...]).T * g_ref[...].astype(F32)).astype(o_ref.dtype)


def _fox_attention(qk, v_t, c_tok, c_rows, gate, batch, seq, heads):
    tq = _tile(seq, ATTN_TILE)
    assert tq % LANES == 0 or tq == seq
    nq = seq // tq
    dh = FOX_HEAD_DIM
    return pl.pallas_call(
        functools.partial(_fox_attn_kernel, tq=tq),
        grid=(batch, heads, nq),
        in_specs=[pl.BlockSpec((tq, dh), lambda b, h, i: (b * nq + i, h)),
                  pl.BlockSpec((seq, dh), lambda b, h, i: (b, heads + h)),
                  pl.BlockSpec((dh, seq), lambda b, h, i: (h, b)),
                  pl.BlockSpec((seq, LANES), lambda b, h, i: (b, 0)),
                  pl.BlockSpec((1, 1, 1, seq), lambda b, h, i: (b, h, 0, 0)),
                  pl.BlockSpec((tq, dh), lambda b, h, i: (b * nq + i, h))],
        out_specs=pl.BlockSpec((tq, dh), lambda b, h, i: (b * nq + i, h)),
        out_shape=jax.ShapeDtypeStruct((batch * seq, heads * dh), BF16),
        scratch_shapes=[pltpu.VMEM((1, tq), F32), pltpu.VMEM((1, tq), F32), pltpu.VMEM((dh, tq), F32),
                        pltpu.VMEM((seq, dh + LANES), BF16),
                        pltpu.VMEM((tq, tq), F32), pltpu.VMEM((1, tq), F32),
                        pltpu.VMEM((tq, tq), F32), pltpu.VMEM((1, tq), F32),
                        pltpu.VMEM((dh + LANES, tq), BF16)],
        compiler_params=_params("parallel", "parallel", "arbitrary"),
        name="fox_attention",
    )(qk, qk, v_t, c_tok, c_rows, gate)


def _fox_qk_kernel(x_ref, w_ref, g_ref, o_ref):
    y = _dot(x_ref[...], w_ref[...])
    g = g_ref[...]
    for hd in range(y.shape[1] // FOX_HEAD_DIM):
        sl = slice(hd * FOX_HEAD_DIM, (hd + 1) * FOX_HEAD_DIM)
        o_ref[:, sl] = _rms(y[:, sl], g[:, sl]).astype(o_ref.dtype)


def _fox_qk(a, w_qk, gain_row):
    t, d = a.shape
    n = w_qk.shape[1]
    tm, tn = _tile(t, ROW_TILE), _tile(n, COL_TILE)
    return pl.pallas_call(
        _fox_qk_kernel,
        grid=(t // tm, n // tn),
        in_specs=[pl.BlockSpec((tm, d), lambda i, j: (i, 0)), pl.BlockSpec((d, tn), lambda i, j: (0, j)),
                  pl.BlockSpec((1, tn), lambda i, j: (0, j))],
        out_specs=pl.BlockSpec((tm, tn), lambda i, j: (i, j)),
        out_shape=jax.ShapeDtypeStruct((t, n), BF16),
        compiler_params=_params("parallel", "parallel"),
        name="fox_qk",
    )(a, w_qk, gain_row)


def _fox_forget_kernel(h_ref, g_ref, w_ref, b_ref, c_ref, ctok_ref, carry_sc, *, tiles_per_seq):
    i = pl.program_id(0)

    @pl.when(i % tiles_per_seq == 0)
    def _():
        carry_sc[...] = jnp.zeros(carry_sc.shape, F32)

    z = _split_dot(_rms(h_ref[...], g_ref[...]), w_ref) + b_ref[...]
    log_f = jnp.minimum(z, 0.0) - jnp.log(1.0 + jnp.exp(-jnp.abs(z)))

    tm = log_f.shape[0]
    row = lax.broadcasted_iota(I32, (tm, tm), 0)
    col = lax.broadcasted_iota(I32, (tm, tm), 1)
    tri = (col <= row).astype(BF16)
    sums = _dot(tri, jnp.concatenate(_split(log_f, 3) + [jnp.zeros_like(log_f, BF16)], axis=1))
    c = (sums[:, :LANES] + (sums[:, LANES:2 * LANES] + sums[:, 2 * LANES:3 * LANES])) + carry_sc[...]
    carry_sc[...] = c[tm - 1:tm, :]
    c_ref[0] = c.T
    ctok_ref[...] = c


def _fox_forget_cumsum(h, gain, w_f, b_f, batch, seq):
    t, d = h.shape
    heads = w_f.shape[1]
    assert heads <= LANES
    tm = _tile(seq, ROUTE_TILE)
    w12 = jnp.concatenate(_split(jnp.pad(w_f, ((0, 0), (0, LANES - heads))), 2), axis=1)
    b = jnp.pad(b_f, (0, LANES - heads)).reshape(1, LANES)
    tps = seq // tm
    return pl.pallas_call(
        functools.partial(_fox_forget_kernel, tiles_per_seq=tps),
        grid=(t // tm,),
        in_specs=[pl.BlockSpec((tm, d), lambda i: (i, 0)), pl.BlockSpec((1, d), lambda i: (0, 0)),
                  pl.BlockSpec((d, 2 * LANES), lambda i: (0, 0)), pl.BlockSpec((1, LANES), lambda i: (0, 0))],
        out_specs=[pl.BlockSpec((1, LANES, tm), lambda i: (i // tps, 0, i % tps)),
                   pl.BlockSpec((tm, LANES), lambda i: (i, 0))],
        out_shape=[jax.ShapeDtypeStruct((batch, LANES, seq), F32), jax.ShapeDtypeStruct((t, LANES), F32)],
        scratch_shapes=[pltpu.VMEM((1, LANES), F32)],
        compiler_params=_params("arbitrary"),
        name="fox_forget_cumsum",
    )(h, gain.reshape(1, d), w12, b)


def _router_kernel(h_ref, g_ref, w_ref, b_ref, t_ref, meta_ref, metat_ref, cnt_ref, carry_sc):
    i = pl.program_id(0)

    @pl.when(i == 0)
    def _():
        carry_sc[...] = jnp.zeros(carry_sc.shape, F32)

    t = _rms(h_ref[...], g_ref[...])
    _to_slabs(t_ref, _pack_halves(t))

    logits = _split_dot(t, w_ref) + b_ref[...]
    tm = logits.shape[0]
    lane = lax.broadcasted_iota(I32, logits.shape, 1)
    lane_f = lane.astype(F32)
    first = lambda hit: jnp.min(jnp.where(hit, lane_f, float(LANES)), axis=1, keepdims=True).astype(I32)

    is_grp = lane < MOE_GROUPS
    gl = jnp.where(is_grp, logits, -jnp.inf)
    gmax = jnp.max(gl, axis=1, keepdims=True)
    g_sel = first(gl == gmax)
    gexp = jnp.where(is_grp, jnp.exp(logits - gmax), 0.0)
    g_w = 1.0 / jnp.sum(gexp, axis=1, keepdims=True)

    lo = MOE_GROUPS + MOE_EXPERTS_PER_GROUP * g_sel
    in_grp = jnp.logical_and(lane >= lo, lane < lo + MOE_EXPERTS_PER_GROUP)
    el = jnp.where(in_grp, logits, -jnp.inf)
    emax = jnp.max(el, axis=1, keepdims=True)
    eexp = jnp.where(in_grp, jnp.exp(logits - emax), 0.0)
    prob = eexp / jnp.sum(eexp, axis=1, keepdims=True)
    cand1 = jnp.where(in_grp, prob, -1.0)
    p1 = jnp.max(cand1, axis=1, keepdims=True)
    j1 = first(cand1 == p1)
    cand2 = jnp.where(lane == j1, -1.0, cand1)
    p2 = jnp.max(cand2, axis=1, keepdims=True)
    j2 = first(cand2 == p2)
    denom = p1 + p2
    wt1 = p1 / denom * g_w
    wt2 = p2 / denom * g_w
    e1 = j1 - MOE_GROUPS
    e2 = j2 - MOE_GROUPS

    hit1 = lane == e1
    hit2 = lane == e2
    row = lax.broadcasted_iota(I32, (tm, tm), 0)
    col = lax.broadcasted_iota(I32, (tm, tm), 1)
    before = (col < row).astype(BF16)
    pre = _dot(before, jnp.concatenate([hit1.astype(BF16), hit2.astype(BF16)], axis=1))
    pre1, pre2 = pre[:, :LANES], pre[:, LANES:]
    carry = carry_sc[...]
    cnt1 = jnp.sum(hit1.astype(F32), axis=0, keepdims=True)
    cnt2 = jnp.sum(hit2.astype(F32), axis=0, keepdims=True)
    rank1 = jnp.sum(jnp.where(hit1, pre1 + carry, 0.0), axis=1, keepdims=True)
    rank2 = jnp.sum(jnp.where(hit2, pre2 + (carry + cnt1), 0.0), axis=1, keepdims=True)
    total = carry + cnt1 + cnt2
    carry_sc[...] = total
    cnt_ref[...] = jnp.broadcast_to(total, cnt_ref.shape)

    bits = lambda x: lax.bitcast_convert_type(jnp.broadcast_to(x, logits.shape), I32)
    meta = jnp.where(lane == 0, e1, 0)
    meta = jnp.where(lane == 1, e2, meta)
    meta = jnp.where(lane == 2, rank1.astype(I32), meta)
    meta = jnp.where(lane == 3, rank2.astype(I32), meta)
    meta = jnp.where(lane == 4, bits(wt1), meta)
    meta = jnp.where(lane == 5, bits(wt2), meta)
    meta_ref[...] = meta
    metat_ref[...] = meta.T[:SUBLANES, :]


def _router(h, gain, w_grp, b_grp, w_rt, b_rt):
    t, d = h.shape
    tm = _tile(t, ROUTE_TILE)
    slab = d // 2 // LANES
    n_used = MOE_GROUPS + MOE_EXPERTS
    w = jnp.pad(jnp.concatenate([w_grp, w_rt], axis=1), ((0, 0), (0, LANES - n_used)))
    w12 = jnp.concatenate(_split(w, 2), axis=1)
    b = jnp.pad(jnp.concatenate([b_grp, b_rt]), (0, LANES - n_used)).reshape(1, LANES)
    return pl.pallas_call(
        _router_kernel,
        grid=(t // tm,),
        in_specs=[pl.BlockSpec((tm, d), lambda i: (i, 0)), pl.BlockSpec((1, d), lambda i: (0, 0)),
                  pl.BlockSpec((d, 2 * LANES), lambda i: (0, 0)), pl.BlockSpec((1, LANES), lambda i: (0, 0))],
        out_specs=[pl.BlockSpec((tm * slab, LANES), lambda i: (i, 0)),
                   pl.BlockSpec((tm, LANES), lambda i: (i, 0)),
                   pl.BlockSpec((SUBLANES, tm), lambda i: (0, i)),
                   pl.BlockSpec((SUBLANES, LANES), lambda i: (0, 0))],
        out_shape=[jax.ShapeDtypeStruct((t * slab, LANES), U32),
                   jax.ShapeDtypeStruct((t, LANES), I32),
                   jax.ShapeDtypeStruct((SUBLANES, t), I32),
                   jax.ShapeDtypeStruct((SUBLANES, LANES), F32)],
        scratch_shapes=[pltpu.VMEM((1, LANES), F32)],
        compiler_params=_params("arbitrary"),
        name="moe_router",
    )(h, gain.reshape(1, d), w12, b)


def _dest_kernel(offs_ref, mt_ref, dest_ref):
    mt = mt_ref[...]
    experts = mt[0:2, :]
    base = jnp.zeros(experts.shape, I32)
    for e in range(MOE_EXPERTS):
        base = jnp.where(experts == e, offs_ref[e], base)
    row = lax.broadcasted_iota(I32, mt.shape, 0)
    dest_ref[...] = jnp.where(row < 2, jnp.concatenate([base + mt[2:4, :], mt[2:SUBLANES, :]], axis=0), 0)


def _dest_rows_all(meta_t, offs):
    rows, t = meta_t.shape
    return pl.pallas_call(
        _dest_kernel,
        grid_spec=pltpu.PrefetchScalarGridSpec(
            num_scalar_prefetch=1,
            grid=(1,),
            in_specs=[pl.BlockSpec((rows, t), lambda i, offs: (0, 0))],
            out_specs=pl.BlockSpec((rows, t), lambda i, offs: (0, 0)),
        ),
        out_shape=jax.ShapeDtypeStruct((rows, t), I32),
        compiler_params=_params("arbitrary"),
        name="moe_dest",
    )(offs, meta_t)


def _slab_copy(src, src_tok, dst, dst_tok, sem, slab):
    rows = lambda tok: pl.ds(pl.multiple_of(tok * slab, slab), slab)
    return pltpu.make_async_copy(src.at[rows(src_tok)], dst.at[rows(dst_tok)], sem)


_ROW_UNROLL = 8


def _dispatch_kernel(zs_ref, nv_ref, dest_ref, t_ref, xs_ref, zero_sc, sem, *, tm, tz, slab):
    i = pl.program_id(0)

    @pl.when(i == 0)
    def _():
        zero_sc[...] = jnp.zeros(zero_sc.shape, zero_sc.dtype)
        fill = lambda tok0: pltpu.make_async_copy(
            zero_sc, xs_ref.at[pl.ds(pl.multiple_of(tok0 * slab, tz * slab), tz * slab)], sem)
        fills = [fill(zs_ref[e]) for e in range(MOE_EXPERTS)]
        for c in fills:
            c.start()
        for c in fills:
            c.wait()

        def tail(tile, carry):
            c = fill(tile * tz)
            c.start()
            c.wait()
            return carry

        lax.fori_loop(nv_ref[0], xs_ref.shape[0] // (tz * slab), tail, 0)

    def issue(blk, carry):
        for u in range(_ROW_UNROLL):
            r = blk * _ROW_UNROLL + u
            _slab_copy(t_ref, r, xs_ref, dest_ref[0, r], sem, slab).start(priority=0)
            _slab_copy(t_ref, r, xs_ref, dest_ref[1, r], sem, slab).start(priority=1)
        return carry

    lax.fori_loop(0, tm // _ROW_UNROLL, issue, 0)

    def drain(blk, carry):
        for u in range(2 * _ROW_UNROLL):
            _slab_copy(t_ref, 0, xs_ref, 0, sem, slab).wait()
        return carry

    lax.fori_loop(0, tm // _ROW_UNROLL, drain, 0)


def _dispatch(t_slabs, dest_t, zero_start, n_valid, n_rows):
    t = dest_t.shape[1]
    slab = t_slabs.shape[0] // t
    tm = _tile(t, MOVE_TILE)
    tz = EXPERT_TILE
    mspec = pl.BlockSpec((SUBLANES, tm), lambda i, *_: (0, i), memory_space=pltpu.SMEM)
    return pl.pallas_call(
        functools.partial(_dispatch_kernel, tm=tm, tz=tz, slab=slab),
        grid_spec=pltpu.PrefetchScalarGridSpec(
            num_scalar_prefetch=2,
            grid=(t // tm,),
            in_specs=[mspec, pl.BlockSpec((tm * slab, LANES), lambda i, *_: (i, 0))],
            out_specs=pl.BlockSpec(memory_space=pl.ANY),
            scratch_shapes=[pltpu.VMEM((tz * slab, LANES), U32), pltpu.SemaphoreType.DMA(())],
        ),
        out_shape=jax.ShapeDtypeStruct((n_rows * slab, LANES), U32),
        compiler_params=_params("arbitrary"),
        name="moe_dispatch",
    )(zero_start, n_valid, dest_t, t_slabs)


def _experts_kernel(te_ref, nv_ref, x_ref, wg_ref, wu_ref, wd_ref, y_ref, wg_sc, wu_sc, wd_sc, *, tm):
    i = pl.program_id(0)
    live = i < nv_ref[0]

    @pl.when(jnp.logical_or(i == 0, te_ref[i] != te_ref[jnp.maximum(i - 1, 0)]))
    def _():
        wg_sc[...] = wg_ref[0, 0].astype(BF16)
        wu_sc[...] = wu_ref[0, 0].astype(BF16)
        wd_sc[...] = wd_ref[0, 0].astype(BF16)

    @pl.when(jnp.logical_not(live))
    def _():
        y_ref[...] = jnp.zeros(y_ref.shape, y_ref.dtype)

    @pl.when(live)
    def _():
        hi, lo = _unpack_halves(_from_slabs(x_ref, tm))
        x = jnp.concatenate([hi, lo], axis=1).astype(BF16)
        g = _dot(x, wg_sc[...])
        u = _dot(x, wu_sc[...])
        hid = (g * jax.nn.sigmoid(g) * u).astype(BF16)
        _to_slabs(y_ref, _pack_halves(_dot(hid, wd_sc[...])))


def _experts(xs, tile_expert, n_valid, w_gate, w_up, w_down, layer, n_tiles):
    tm = EXPERT_TILE
    _, _, d, f = w_gate.shape
    rows = tm * (d // 2 // LANES)
    live = lambda i, nv: jnp.minimum(i, nv[0] - 1)
    return pl.pallas_call(
        functools.partial(_experts_kernel, tm=tm),
        grid_spec=pltpu.PrefetchScalarGridSpec(
            num_scalar_prefetch=2,
            grid=(n_tiles,),
            in_specs=[pl.BlockSpec((rows, LANES), lambda i, te, nv: (live(i, nv), 0)),
                      pl.BlockSpec((1, 1, d, f), lambda i, te, nv: (layer, te[i], 0, 0)),
                      pl.BlockSpec((1, 1, d, f), lambda i, te, nv: (layer, te[i], 0, 0)),
                      pl.BlockSpec((1, 1, f, d), lambda i, te, nv: (layer, te[i], 0, 0))],
            out_specs=pl.BlockSpec((rows, LANES), lambda i, te, nv: (i, 0)),
            scratch_shapes=[pltpu.VMEM((d, f), BF16), pltpu.VMEM((d, f), BF16), pltpu.VMEM((f, d), BF16)],
        ),
        out_shape=jax.ShapeDtypeStruct((n_tiles * rows, LANES), U32),
        compiler_params=_params("arbitrary"),
        name="moe_experts",
    )(tile_expert, n_valid, xs, w_gate, w_up, w_down)


def _combine_kernel(dest_ref, dnext_ref, meta_ref, h_ref, g_ref, ys_ref, *rest, tm, slab, final):
    if final:
        o_ref, y_sc, sem = rest
    else:
        hn_ref, a_ref, y_sc, sem = rest
    i = pl.program_id(0)
    n = pl.num_programs(0)

    def gather(dref, slot):
        def issue(blk, carry):
            for u in range(_ROW_UNROLL):
                r = blk * _ROW_UNROLL + u
                _slab_copy(ys_ref, dref[0, r], y_sc.at[slot, 0], r, sem.at[slot], slab).start(priority=0)
                _slab_copy(ys_ref, dref[1, r], y_sc.at[slot, 1], r, sem.at[slot], slab).start(priority=1)
            return carry

        lax.fori_loop(0, tm // _ROW_UNROLL, issue, 0)

    def finish(slot):
        def drain(blk, carry):
            for u in range(2 * _ROW_UNROLL):
                _slab_copy(ys_ref, 0, y_sc.at[slot, 0], 0, sem.at[slot], slab).wait()
            return carry

        lax.fori_loop(0, tm // _ROW_UNROLL, drain, 0)

        meta = meta_ref[...]
        lane = lax.broadcasted_iota(I32, meta.shape, 1)
        wbits = lax.bitcast_convert_type(meta, F32)
        wt1 = jnp.sum(jnp.where(lane == 4, wbits, 0.0), axis=1, keepdims=True)
        wt2 = jnp.sum(jnp.where(lane == 5, wbits, 0.0), axis=1, keepdims=True)
        hi1, lo1 = _unpack_halves(_from_slabs(y_sc.at[slot, 0], tm))
        hi2, lo2 = _unpack_halves(_from_slabs(y_sc.at[slot, 1], tm))
        moe = jnp.concatenate([wt1 * hi1 + wt2 * hi2, wt1 * lo1 + wt2 * lo2], axis=1)
        h_new = h_ref[...] + moe
        if final:
            o_ref[...] = _rms(h_new, g_ref[...])
        else:
            hn_ref[...] = h_new
            a_ref[...] = _rms(h_new, g_ref[...]).astype(a_ref.dtype)

    @pl.when(i == 0)
    def _():
        gather(dest_ref, 0)

    for slot in (0, 1):
        @pl.when((i & 1) == slot)
        def _():
            @pl.when(i + 1 < n)
            def _():
                gather(dnext_ref, 1 - slot)

            finish(slot)


def _combine(ys, dest_t, meta, h, gain, final):
    t, d = h.shape
    tm = _tile(t, MOVE_TILE)
    slab = d // 2 // LANES
    n = t // tm
    mspec = pl.BlockSpec((SUBLANES, tm), lambda i: (0, i), memory_space=pltpu.SMEM)
    mnext = pl.BlockSpec((SUBLANES, tm), lambda i: (0, jnp.minimum(i + 1, n - 1)), memory_space=pltpu.SMEM)
    row = pl.BlockSpec((tm, d), lambda i: (i, 0))
    if final:
        out_specs, out_shape = row, jax.ShapeDtypeStruct((t, d), F32)
    else:
        out_specs = [row, row]
        out_shape = [jax.ShapeDtypeStruct((t, d), F32), jax.ShapeDtypeStruct((t, d), BF16)]
    return pl.pallas_call(
        functools.partial(_combine_kernel, tm=tm, slab=slab, final=final),
        grid=(n,),
        in_specs=[mspec, mnext, pl.BlockSpec((tm, LANES), lambda i: (i, 0)), row,
                  pl.BlockSpec((1, d), lambda i: (0, 0)), pl.BlockSpec(memory_space=pl.ANY)],
        out_specs=out_specs,
        out_shape=out_shape,
        scratch_shapes=[pltpu.VMEM((2, 2, tm * slab, LANES), U32), pltpu.SemaphoreType.DMA((2,))],
        compiler_params=_params("arbitrary"),
        name="moe_combine",
    )(dest_t, dest_t, meta, h, gain.reshape(1, d), ys)


def _hier_moe(h, ffn_gain, w_grp, b_grp, w_rt, b_rt, w_gate, w_up, w_down, layer, next_gain, final):
    t, d = h.shape
    tmx = EXPERT_TILE
    t_slabs, meta, meta_t, cnt = _router(h, ffn_gain, w_grp, b_grp, w_rt, b_rt)

    counts = cnt[0, :MOE_EXPERTS].astype(I32)
    padded = (counts + tmx - 1) // tmx * tmx
    ends = jnp.cumsum(padded)
    offs = ends - padded
    n_tiles = (2 * t) // tmx + MOE_EXPERTS
    n_valid = (ends[-1] // tmx).reshape(1)
    tile_start = jnp.minimum(jnp.arange(n_tiles, dtype=I32) * tmx, ends[-1] - 1)
    tile_expert = jnp.sum((ends[None, :] <= tile_start[:, None]).astype(I32), axis=1)
    zero_start = offs + counts // tmx * tmx

    dest_t = _dest_rows_all(meta_t, offs)
    xs = _dispatch(t_slabs, dest_t, zero_start, n_valid, (n_tiles + 1) * tmx)
    ys = _experts(xs, tile_expert, n_valid, w_gate, w_up, w_down, layer, n_tiles)
    return _combine(ys, dest_t, meta, h, next_gain, final)


def kernel(x, positions, attn_norm, ffn_norm, final_norm, mla_w_dq, mla_q_norm, mla_w_uq, mla_w_dkv,
           mla_kv_norm, mla_w_ukv, mla_w_o, fox_w_qkv, fox_q_norm, fox_k_norm, fox_w_f, fox_b_f,
           fox_w_og, fox_w_o, moe_w_grp, moe_b_grp, moe_w_rt, moe_b_rt, moe_w_gate, moe_w_up, moe_w_down):
    batch, seq, d = x.shape
    depth = attn_norm.shape[0]
    t = batch * seq
    cos_t, sin_t = _rope_tables(positions)
    h = x.reshape(t, d)
    a = _norm(h, attn_norm[0], BF16)
    out = None
    for i in range(depth):
        j = i // 2
        if i % 2 == 0:
            heads = mla_w_uq.shape[2] // (MLA_NOPE + MLA_ROPE)
            q, k, v_t = _mla_project(a, cos_t, sin_t, mla_w_dq[j], mla_q_norm[j], mla_w_uq[j],
                                     mla_w_dkv[j], mla_kv_norm[j], mla_w_ukv[j])
            o = _mla_attention(q, k, v_t, batch, seq, heads)
            w_o = mla_w_o[j]
        else:
            dh = FOX_HEAD_DIM
            heads = fox_w_qkv.shape[2] // (3 * dh)
            w_qkv = fox_w_qkv[j].astype(BF16)
            gain_row = jnp.concatenate([jnp.tile(fox_q_norm[j] * (dh ** -0.5 * LOG2_E), heads),
                                        jnp.tile(fox_k_norm[j], heads)])[None, :]
            qk = _fox_qk(a, w_qkv[:, :2 * heads * dh], gain_row)
            v_t = _mm_nt(w_qkv[:, 2 * heads * dh:].T, a, BF16, name="fox_v")
            gate = _mm(a, fox_w_og[j].astype(BF16), BF16, act="sigmoid", name="fox_gate")
            c_t, c_tok = _fox_forget_cumsum(h, attn_norm[i], fox_w_f[j], fox_b_f[j], batch, seq)
            c_rows = c_t[:, :heads, :].reshape(batch, heads, 1, seq)
            o = _fox_attention(qk, v_t, c_tok, c_rows, gate, batch, seq, heads)
            w_o = fox_w_o[j]
        h = _mm_residual(o, w_o.astype(BF16), h)
        final = i == depth - 1
        next_gain = final_norm if final else attn_norm[i + 1]
        res = _hier_moe(h, ffn_norm[i], moe_w_grp[i], moe_b_grp[i], moe_w_rt[i], moe_b_rt[i],
                        moe_w_gate, moe_w_up, moe_w_down, i, next_gain, final)
        if final:
            out = res
        else:
            h, a = res
    return out.reshape(batch, seq, d)
```

```python
import functools

import jax
import jax.numpy as jnp
from jax import lax
from jax.experimental import pallas as pl
from jax.experimental.pallas import tpu as pltpu

F32 = jnp.float32
BF16 = jnp.bfloat16
I32 = jnp.int32
U32 = jnp.uint32

RMS_EPS = 1e-6
NEG_INF = -1e30
CHUNK = 64
MLA_NOPE = 128
MLA_ROPE = 64
MLA_V = 128
MLA_QK_PAD = 256
ROPE_THETA = 10000.0
LOG2_E = 1.4426950408889634
FOX_HEAD_DIM = 128
MOE_GROUPS = 8
MOE_EXPERTS_PER_GROUP = 4
MOE_EXPERTS = MOE_GROUPS * MOE_EXPERTS_PER_GROUP

LANES = 128
SUBLANES = 8
V7X_VMEM_LIMIT_BYTES = 56 * 1024 * 1024

ROW_TILE = 1024
COL_TILE = 1024
ATTN_TILE = 512
ROUTE_TILE = 512
MOVE_TILE = 256
EXPERT_TILE = 256


def _params(*sem):
    return pltpu.CompilerParams(dimension_semantics=sem, vmem_limit_bytes=V7X_VMEM_LIMIT_BYTES)


def _tile(n, t):
    if n <= t:
        return n
    step = LANES if t % LANES == 0 else SUBLANES
    for c in range(t - t % step, 0, -step):
        if n % c == 0:
            return c
    raise ValueError(f"no aligned tile for {n} under {t}")


def _rms(x, gain):
    ms = jnp.mean(x * x, axis=-1, keepdims=True)
    return x * lax.rsqrt(ms + RMS_EPS) * gain


def _dot(a, b):
    return jnp.dot(a, b, preferred_element_type=F32)


def _nt_dot(a, b):
    return lax.dot_general(a, b, (((1,), (1,)), ((), ())), preferred_element_type=F32)


def _pack_halves(x):
    n = x.shape[1] // 2
    hi = lax.bitcast_convert_type(x[:, :n].astype(BF16).astype(F32), U32)
    lo = lax.bitcast_convert_type(x[:, n:].astype(BF16).astype(F32), U32)
    return hi | lax.shift_right_logical(lo, jnp.uint32(16))


def _unpack_halves(p):
    hi = lax.bitcast_convert_type(p & jnp.uint32(0xFFFF0000), F32)
    lo = lax.bitcast_convert_type(lax.shift_left(p, jnp.uint32(16)), F32)
    return hi, lo


def _to_slabs(ref, x):
    m, width = x.shape
    c = width // LANES
    for j in range(c):
        ref[pl.ds(j, m, stride=c), :] = x[:, j * LANES:(j + 1) * LANES]


def _from_slabs(ref, m):
    c = ref.shape[0] // m
    return jnp.concatenate([ref[pl.ds(j, m, stride=c), :] for j in range(c)], axis=1)


def _split(x, terms):
    out = []
    for _ in range(terms):
        hi = x.astype(BF16)
        out.append(hi)
        x = x - hi.astype(F32)
    return out


def _split_dot(x, w12_ref):
    m = x.shape[0]
    n = w12_ref.shape[1] // 2
    prod = _dot(jnp.concatenate(_split(x, 2), axis=0), w12_ref[...])
    return (prod[:m, :n] + (prod[:m, n:] + prod[m:, :n])) + prod[m:, n:]


def _rope_lanes(seg, cos_t, sin_t):
    half = MLA_ROPE // 2
    lane = lax.broadcasted_iota(I32, seg.shape, 1)
    swapped = jnp.where(lane < half, pltpu.roll(seg, LANES - half, 1), pltpu.roll(seg, half, 1))
    return seg * cos_t + swapped * sin_t


def _rope_table_kernel(pos_ref, freq_ref, cos_ref, sin_ref):
    ang = pos_ref[...] * freq_ref[...]
    lane = lax.broadcasted_iota(I32, ang.shape, 1)
    half = MLA_ROPE // 2
    valid = lane < MLA_ROPE
    cos_ref[...] = jnp.where(valid, jnp.cos(ang), 0.0)
    sin_ref[...] = jnp.where(valid, jnp.where(lane < half, -jnp.sin(ang), jnp.sin(ang)), 0.0)


def _rope_tables(positions):
    t = positions.size
    tm = _tile(t, ROW_TILE)
    half = MLA_ROPE // 2
    inv_freq = ROPE_THETA ** (-jnp.arange(0, MLA_ROPE, 2, dtype=F32) / MLA_ROPE)
    freq_row = jnp.concatenate([inv_freq, inv_freq, jnp.zeros((LANES - 2 * half,), F32)])[None, :]
    pos = positions.reshape(t, 1).astype(F32)
    out = jax.ShapeDtypeStruct((t, LANES), F32)
    return pl.pallas_call(
        _rope_table_kernel,
        grid=(t // tm,),
        in_specs=[pl.BlockSpec((tm, 1), lambda i: (i, 0)), pl.BlockSpec((1, LANES), lambda i: (0, 0))],
        out_specs=[pl.BlockSpec((tm, LANES), lambda i: (i, 0))] * 2,
        out_shape=[out, out],
        compiler_params=_params("parallel"),
        name="rope_tables",
    )(pos, freq_row)


def _norm_kernel(h_ref, g_ref, a_ref):
    a_ref[...] = _rms(h_ref[...], g_ref[...]).astype(a_ref.dtype)


def _norm(h, gain, out_dtype):
    t, d = h.shape
    tm = _tile(t, ROUTE_TILE)
    return pl.pallas_call(
        _norm_kernel,
        grid=(t // tm,),
        in_specs=[pl.BlockSpec((tm, d), lambda i: (i, 0)), pl.BlockSpec((1, d), lambda i: (0, 0))],
        out_specs=pl.BlockSpec((tm, d), lambda i: (i, 0)),
        out_shape=jax.ShapeDtypeStruct((t, d), out_dtype),
        compiler_params=_params("parallel"),
        name="rmsnorm",
    )(h, gain.reshape(1, d))


def _mm_kernel(x_ref, w_ref, o_ref, *, act):
    y = _dot(x_ref[...], w_ref[...])
    if act == "sigmoid":
        y = jax.nn.sigmoid(y)
    o_ref[...] = y.astype(o_ref.dtype)


def _mm(x, w, out_dtype, act=None, name="mm", col_tile=COL_TILE):
    m, k = x.shape
    n = w.shape[1]
    tm, tn = _tile(m, ROW_TILE), _tile(n, col_tile)
    return pl.pallas_call(
        functools.partial(_mm_kernel, act=act),
        grid=(m // tm, n // tn),
        in_specs=[pl.BlockSpec((tm, k), lambda i, j: (i, 0)), pl.BlockSpec((k, tn), lambda i, j: (0, j))],
        out_specs=pl.BlockSpec((tm, tn), lambda i, j: (i, j)),
        out_shape=jax.ShapeDtypeStruct((m, n), out_dtype),
        compiler_params=_params("parallel", "parallel"),
        name=name,
    )(x, w)


def _mm_nt_kernel(wt_ref, x_ref, o_ref):
    o_ref[...] = _nt_dot(wt_ref[...], x_ref[...]).astype(o_ref.dtype)


def _mm_nt(w_t, x, out_dtype, name):
    n, k = w_t.shape
    m = x.shape[0]
    tm, tn = _tile(m, ROW_TILE), _tile(n, COL_TILE)
    return pl.pallas_call(
        _mm_nt_kernel,
        grid=(m // tm, n // tn),
        in_specs=[pl.BlockSpec((tn, k), lambda i, j: (j, 0)), pl.BlockSpec((tm, k), lambda i, j: (i, 0))],
        out_specs=pl.BlockSpec((tn, tm), lambda i, j: (j, i)),
        out_shape=jax.ShapeDtypeStruct((n, m), out_dtype),
        compiler_params=_params("parallel", "parallel"),
        name=name,
    )(w_t, x)


def _mm_res_kernel(x_ref, w_ref, h_ref, o_ref):
    o_ref[...] = h_ref[...] + _dot(x_ref[...], w_ref[...])


def _mm_residual(x, w, h):
    m, k = x.shape
    n = w.shape[1]
    tm, tn = _tile(m, ROW_TILE), _tile(n, COL_TILE)
    return pl.pallas_call(
        _mm_res_kernel,
        grid=(m // tm, n // tn),
        in_specs=[pl.BlockSpec((tm, k), lambda i, j: (i, 0)), pl.BlockSpec((k, tn), lambda i, j: (0, j)),
                  pl.BlockSpec((tm, tn), lambda i, j: (i, j))],
        out_specs=pl.BlockSpec((tm, tn), lambda i, j: (i, j)),
        out_shape=jax.ShapeDtypeStruct((m, n), F32),
        compiler_params=_params("parallel", "parallel"),
        name="out_proj_residual",
    )(x, w, h)


def _mla_q_kernel(c_ref, g_ref, w_ref, cos_ref, sin_ref, q_ref, *, scale):
    cq = _rms(c_ref[...], g_ref[...]).astype(BF16)
    y = _dot(cq, w_ref[...]) * scale
    cos_t, sin_t = cos_ref[...], sin_ref[...]
    for hd in range(y.shape[1] // MLA_QK_PAD):
        base = hd * MLA_QK_PAD
        q_ref[:, base:base + MLA_NOPE] = y[:, base:base + MLA_NOPE].astype(q_ref.dtype)
        roped = _rope_lanes(y[:, base + MLA_NOPE:base + MLA_QK_PAD], cos_t, sin_t)
        q_ref[:, base + MLA_NOPE:base + MLA_QK_PAD] = roped.astype(q_ref.dtype)


def _mla_kv_kernel(c_ref, pe_ref, g_ref, wk_ref, wvt_ref, cos_ref, sin_ref, k_ref, vt_ref):
    ckv = _rms(c_ref[...], g_ref[...]).astype(BF16)
    kn = _dot(ckv, wk_ref[...])
    vt_ref[...] = _nt_dot(wvt_ref[...], ckv).astype(vt_ref.dtype)
    k_pe = _rope_lanes(pe_ref[...], cos_ref[...], sin_ref[...]).astype(k_ref.dtype)
    for hd in range(kn.shape[1] // MLA_NOPE):
        k_ref[:, hd * MLA_QK_PAD:hd * MLA_QK_PAD + MLA_NOPE] = (
            kn[:, hd * MLA_NOPE:(hd + 1) * MLA_NOPE].astype(k_ref.dtype))
        k_ref[:, hd * MLA_QK_PAD + MLA_NOPE:(hd + 1) * MLA_QK_PAD] = k_pe


def _mla_project(a, cos_t, sin_t, w_dq, q_norm, w_uq, w_dkv, kv_norm, w_ukv):
    t, d = a.shape
    q_lora = w_dq.shape[1]
    kv_lora = w_dkv.shape[1] - MLA_ROPE
    heads = w_uq.shape[1] // (MLA_NOPE + MLA_ROPE)
    assert q_lora % LANES == 0 and kv_lora % LANES == 0

    w_down = jnp.concatenate(
        [w_dq, w_dkv, jnp.zeros((d, LANES - MLA_ROPE), w_dkv.dtype)], axis=1).astype(BF16)
    s1 = _mm(a, w_down, F32, name="mla_down", col_tile=w_down.shape[1])

    w_q = w_uq.reshape(q_lora, heads, MLA_NOPE + MLA_ROPE)
    w_q = jnp.pad(w_q, ((0, 0), (0, 0), (0, MLA_QK_PAD - MLA_NOPE - MLA_ROPE)))
    w_q = w_q.reshape(q_lora, heads * MLA_QK_PAD).astype(BF16)
    w_kv = w_ukv.reshape(kv_lora, heads, MLA_NOPE + MLA_V)
    w_k = w_kv[:, :, :MLA_NOPE].reshape(kv_lora, heads * MLA_NOPE).astype(BF16)
    w_vt = w_kv[:, :, MLA_NOPE:].reshape(kv_lora, heads * MLA_V).T.astype(BF16)

    tm = _tile(t, ROW_TILE)
    scale = (MLA_NOPE + MLA_ROPE) ** -0.5 * LOG2_E
    tn = _tile(heads * MLA_QK_PAD, COL_TILE)
    row128 = pl.BlockSpec((tm, LANES), lambda i, j: (i, 0))
    q = pl.pallas_call(
        functools.partial(_mla_q_kernel, scale=scale),
        grid=(t // tm, heads * MLA_QK_PAD // tn),
        in_specs=[pl.BlockSpec((tm, q_lora), lambda i, j: (i, 0)),
                  pl.BlockSpec((1, q_lora), lambda i, j: (0, 0)),
                  pl.BlockSpec((q_lora, tn), lambda i, j: (0, j)), row128, row128],
        out_specs=pl.BlockSpec((tm, tn), lambda i, j: (i, j)),
        out_shape=jax.ShapeDtypeStruct((t, heads * MLA_QK_PAD), BF16),
        compiler_params=_params("parallel", "parallel"),
        name="mla_q",
    )(s1, q_norm.reshape(1, q_lora), w_q, cos_t, sin_t)

    hb = min(heads, COL_TILE // MLA_QK_PAD)
    kv_blk = q_lora // kv_lora
    assert q_lora % kv_lora == 0
    pe_blk = (q_lora + kv_lora) // LANES
    k, v_t = pl.pallas_call(
        _mla_kv_kernel,
        grid=(t // tm, heads // hb),
        in_specs=[pl.BlockSpec((tm, kv_lora), lambda i, j: (i, kv_blk)),
                  pl.BlockSpec((tm, LANES), lambda i, j: (i, pe_blk)),
                  pl.BlockSpec((1, kv_lora), lambda i, j: (0, 0)),
                  pl.BlockSpec((kv_lora, hb * MLA_NOPE), lambda i, j: (0, j)),
                  pl.BlockSpec((hb * MLA_V, kv_lora), lambda i, j: (j, 0)), row128, row128],
        out_specs=[pl.BlockSpec((tm, hb * MLA_QK_PAD), lambda i, j: (i, j)),
                   pl.BlockSpec((hb * MLA_V, tm), lambda i, j: (j, i))],
        out_shape=[jax.ShapeDtypeStruct((t, heads * MLA_QK_PAD), BF16),
                   jax.ShapeDtypeStruct((heads * MLA_V, t), BF16)],
        compiler_params=_params("parallel", "parallel"),
        name="mla_kv",
    )(s1, s1, kv_norm.reshape(1, kv_lora), w_k, w_vt, cos_t, sin_t)
    return q, k, v_t


def _col_max(x):
    while x.shape[0] > SUBLANES and x.shape[0] % (2 * SUBLANES) == 0:
        half = x.shape[0] // 2
        x = jnp.maximum(x[:half], x[half:])
    return jnp.max(x, axis=0, keepdims=True)


def _softmax_update(s_t, m_cur, v_t, m_sc, l_sc, acc_sc, col_shift=None):
    m_prev = m_sc[...]
    if col_shift is not None:
        m_cur = m_cur + col_shift
    m_new = jnp.maximum(m_prev, m_cur)
    shift = m_new if col_shift is None else m_new - col_shift
    p_t = jnp.exp2(s_t - shift)
    alpha = jnp.exp2(m_prev - m_new)
    l_sc[...] = alpha * l_sc[...] + jnp.sum(p_t, axis=0, keepdims=True)
    acc_sc[...] = alpha * acc_sc[...] + _dot(v_t, p_t.astype(v_t.dtype))
    m_sc[...] = m_new


def _init_softmax(m_sc, l_sc, acc_sc):
    m_sc[...] = jnp.full(m_sc.shape, NEG_INF, F32)
    l_sc[...] = jnp.zeros(l_sc.shape, F32)
    acc_sc[...] = jnp.zeros(acc_sc.shape, F32)


def _row_sweep(nq, trips_ref, refresh_q, scores, consume, consume_diag, bufs):
    def produce(kt, s_buf, m_buf):
        s_t = scores(kt)
        s_buf[...] = s_t
        m_buf[...] = _col_max(s_t)

    refresh_q(0)
    produce(0, *bufs[0])
    first = 0
    for qi in range(nq):
        cur, other = bufs[first], bufs[1 - first]

        def pair(p, carry, qi=qi, cur=cur, other=other):
            produce(2 * p + 1, *other)
            consume(qi, 2 * p, *cur)
            produce(2 * p + 2, *cur)
            consume(qi, 2 * p + 1, *other)
            return carry

        lax.fori_loop(0, trips_ref[qi], pair, 0)
        if qi % 2 == 1:
            produce(qi, *other)
            consume(qi, qi - 1, *cur)
            cur, other, first = other, cur, 1 - first
        if qi + 1 < nq:
            refresh_q(qi + 1)
            produce(0, *other)
        consume_diag(qi, *cur)
        first = 1 - first


def _mla_attn_kernel(trips_ref, q_ref, k_ref, vt_ref, o_ref, m_sc, l_sc, acc_sc, sa_sc, ma_sc, sb_sc, mb_sc,
                     qt_sc, *, tq, nq, chunk_shift):
    _init_softmax(m_sc, l_sc, acc_sc)
    rows = lambda i: pl.ds(pl.multiple_of(i * tq, tq), tq)

    def refresh_q(qi):
        qt_sc[...] = q_ref[rows(qi), :].T

    def scores(kt):
        return _dot(k_ref[rows(kt), :], qt_sc[...])

    def consume(qi, kt, s_buf, m_buf):
        _softmax_update(s_buf[...], m_buf[...], vt_ref[:, rows(kt)], m_sc, l_sc, acc_sc)

    def consume_diag(qi, s_buf, m_buf):
        s_t = s_buf[...]
        key = lax.broadcasted_iota(I32, s_t.shape, 0)
        qry = lax.broadcasted_iota(I32, s_t.shape, 1)
        allowed = lax.shift_right_logical(key, chunk_shift) <= lax.shift_right_logical(qry, chunk_shift)
        s_t = jnp.where(allowed, s_t, NEG_INF)
        _softmax_update(s_t, _col_max(s_t), vt_ref[:, rows(qi)], m_sc, l_sc, acc_sc)
        o_ref[rows(qi), :] = (acc_sc[...] / l_sc[...]).T.astype(o_ref.dtype)
        _init_softmax(m_sc, l_sc, acc_sc)

    _row_sweep(nq, trips_ref, refresh_q, scores, consume, consume_diag, ((sa_sc, ma_sc), (sb_sc, mb_sc)))


def _mla_attention(q, k, v_t, batch, seq, heads):
    tq = _tile(seq, ATTN_TILE)
    assert tq % CHUNK == 0 and CHUNK & (CHUNK - 1) == 0
    nq = seq // tq
    return pl.pallas_call(
        functools.partial(_mla_attn_kernel, tq=tq, nq=nq, chunk_shift=CHUNK.bit_length() - 1),
        grid_spec=pltpu.PrefetchScalarGridSpec(
            num_scalar_prefetch=1,
            grid=(batch, heads),
            in_specs=[pl.BlockSpec((seq, MLA_QK_PAD), lambda b, h, tr: (b, h)),
                      pl.BlockSpec((seq, MLA_QK_PAD), lambda b, h, tr: (b, h)),
                      pl.BlockSpec((MLA_V, seq), lambda b, h, tr: (h, b))],
            out_specs=pl.BlockSpec((seq, MLA_V), lambda b, h, tr: (b, h)),
            scratch_shapes=[pltpu.VMEM((1, tq), F32), pltpu.VMEM((1, tq), F32), pltpu.VMEM((MLA_V, tq), F32),
                            pltpu.VMEM((tq, tq), F32), pltpu.VMEM((1, tq), F32),
                            pltpu.VMEM((tq, tq), F32), pltpu.VMEM((1, tq), F32),
                            pltpu.VMEM((MLA_QK_PAD, tq), BF16)],
        ),
        out_shape=jax.ShapeDtypeStruct((batch * seq, heads * MLA_V), BF16),
        compiler_params=_params("parallel", "parallel"),
        name="mla_attention",
    )(jnp.arange(nq, dtype=I32) // 2, q, k, v_t)


_FORGET_TERMS = 3


def _fox_attn_kernel(trips_ref, q_ref, k_ref, vt_ref, ctok_ref, crow_ref, g_ref, o_ref, m_sc, l_sc, acc_sc,
                     ka_sc, sa_sc, ma_sc, sb_sc, mb_sc, qt_sc, *, tq, nq):
    head = pl.program_id(1)
    dh = k_ref.shape[1]
    _init_softmax(m_sc, l_sc, acc_sc)
    rows = lambda i: pl.ds(pl.multiple_of(i * tq, tq), tq)

    row = lax.broadcasted_iota(I32, (LANES, tq), 0)
    qt_sc[dh:, :] = jnp.where(row < _FORGET_TERMS, -1.0, 0.0).astype(qt_sc.dtype)

    ka_sc[:, :dh] = k_ref[...]
    terms = _split(ctok_ref[...] * LOG2_E, _FORGET_TERMS)
    src = lax.broadcasted_iota(I32, (LANES, LANES), 0)
    dst = lax.broadcasted_iota(I32, (LANES, LANES), 1)
    aug = None
    for j, term in enumerate(terms):
        pick = jnp.logical_and(src == head, dst == j).astype(BF16)
        part = _dot(term, pick)
        aug = part if aug is None else aug + part
    ka_sc[:, dh:] = aug.astype(ka_sc.dtype)

    def refresh_q(qi):
        qt_sc[:dh, :] = q_ref[rows(qi), :].T

    def scores(kt):
        return _dot(ka_sc[rows(kt), :], qt_sc[...])

    def c_q(qi):
        return crow_ref[0, 0, :, rows(qi)] * LOG2_E

    def consume(qi, kt, s_buf, m_buf):
        _softmax_update(s_buf[...], m_buf[...], vt_ref[:, rows(kt)], m_sc, l_sc, acc_sc, col_shift=c_q(qi))

    def consume_diag(qi, s_buf, m_buf):
        s_t = s_buf[...]
        key = lax.broadcasted_iota(I32, s_t.shape, 0)
        qry = lax.broadcasted_iota(I32, s_t.shape, 1)
        s_t = jnp.where(key <= qry, s_t, NEG_INF)
        _softmax_update(s_t, _col_max(s_t), vt_ref[:, rows(qi)], m_sc, l_sc, acc_sc, col_shift=c_q(qi))
        gated = (acc_sc[...] / l_sc[...]).T * g_ref[rows(qi), :].astype(F32)
        o_ref[rows(qi), :] = gated.astype(o_ref.dtype)
        _init_softmax(m_sc, l_sc, acc_sc)

    _row_sweep(nq, trips_ref, refresh_q, scores, consume, consume_diag, ((sa_sc, ma_sc), (sb_sc, mb_sc)))


def _fox_attention(qk, v_t, c_tok, c_rows, gate, batch, seq, heads):
    tq = _tile(seq, ATTN_TILE)
    assert tq % LANES == 0 or tq == seq
    dh = FOX_HEAD_DIM
    nq = seq // tq
    return pl.pallas_call(
        functools.partial(_fox_attn_kernel, tq=tq, nq=nq),
        grid_spec=pltpu.PrefetchScalarGridSpec(
            num_scalar_prefetch=1,
            grid=(batch, heads),
            in_specs=[pl.BlockSpec((seq, dh), lambda b, h, tr: (b, h)),
                      pl.BlockSpec((seq, dh), lambda b, h, tr: (b, heads + h)),
                      pl.BlockSpec((dh, seq), lambda b, h, tr: (h, b)),
                      pl.BlockSpec((seq, LANES), lambda b, h, tr: (b, 0)),
                      pl.BlockSpec((1, 1, 1, seq), lambda b, h, tr: (b, h, 0, 0)),
                      pl.BlockSpec((seq, dh), lambda b, h, tr: (b, h))],
            out_specs=pl.BlockSpec((seq, dh), lambda b, h, tr: (b, h)),
            scratch_shapes=[pltpu.VMEM((1, tq), F32), pltpu.VMEM((1, tq), F32), pltpu.VMEM((dh, tq), F32),
                            pltpu.VMEM((seq, dh + LANES), BF16),
                            pltpu.VMEM((tq, tq), F32), pltpu.VMEM((1, tq), F32),
                            pltpu.VMEM((tq, tq), F32), pltpu.VMEM((1, tq), F32),
                            pltpu.VMEM((dh + LANES, tq), BF16)],
        ),
        out_shape=jax.ShapeDtypeStruct((batch * seq, heads * dh), BF16),
        compiler_params=_params("parallel", "parallel"),
        name="fox_attention",
    )(jnp.arange(nq, dtype=I32) // 2, qk, qk, v_t, c_tok, c_rows, gate)


def _fox_qk_kernel(x_ref, w_ref, g_ref, o_ref):
    y = _dot(x_ref[...], w_ref[...])
    g = g_ref[...]
    for hd in range(y.shape[1] // FOX_HEAD_DIM):
        sl = slice(hd * FOX_HEAD_DIM, (hd + 1) * FOX_HEAD_DIM)
        o_ref[:, sl] = _rms(y[:, sl], g[:, sl]).astype(o_ref.dtype)


def _fox_qk(a, w_qk, gain_row):
    t, d = a.shape
    n = w_qk.shape[1]
    tm, tn = _tile(t, ROW_TILE), _tile(n, COL_TILE)
    return pl.pallas_call(
        _fox_qk_kernel,
        grid=(t // tm, n // tn),
        in_specs=[pl.BlockSpec((tm, d), lambda i, j: (i, 0)), pl.BlockSpec((d, tn), lambda i, j: (0, j)),
                  pl.BlockSpec((1, tn), lambda i, j: (0, j))],
        out_specs=pl.BlockSpec((tm, tn), lambda i, j: (i, j)),
        out_shape=jax.ShapeDtypeStruct((t, n), BF16),
        compiler_params=_params("parallel", "parallel"),
        name="fox_qk",
    )(a, w_qk, gain_row)


def _fox_forget_kernel(h_ref, g_ref, w_ref, b_ref, c_ref, ctok_ref, carry_sc, *, tiles_per_seq):
    i = pl.program_id(0)

    @pl.when(i % tiles_per_seq == 0)
    def _():
        carry_sc[...] = jnp.zeros(carry_sc.shape, F32)

    z = _split_dot(_rms(h_ref[...], g_ref[...]), w_ref) + b_ref[...]
    log_f = jnp.minimum(z, 0.0) - jnp.log(1.0 + jnp.exp(-jnp.abs(z)))

    tm = log_f.shape[0]
    row = lax.broadcasted_iota(I32, (tm, tm), 0)
    col = lax.broadcasted_iota(I32, (tm, tm), 1)
    tri = (col <= row).astype(BF16)
    sums = _dot(tri, jnp.concatenate(_split(log_f, 3) + [jnp.zeros_like(log_f, BF16)], axis=1))
    c = (sums[:, :LANES] + (sums[:, LANES:2 * LANES] + sums[:, 2 * LANES:3 * LANES])) + carry_sc[...]
    carry_sc[...] = c[tm - 1:tm, :]
    c_ref[0] = c.T
    ctok_ref[...] = c


def _fox_forget_cumsum(h, gain, w_f, b_f, batch, seq):
    t, d = h.shape
    heads = w_f.shape[1]
    assert heads <= LANES
    tm = _tile(seq, ROUTE_TILE)
    w12 = jnp.concatenate(_split(jnp.pad(w_f, ((0, 0), (0, LANES - heads))), 2), axis=1)
    b = jnp.pad(b_f, (0, LANES - heads)).reshape(1, LANES)
    tps = seq // tm
    return pl.pallas_call(
        functools.partial(_fox_forget_kernel, tiles_per_seq=tps),
        grid=(t // tm,),
        in_specs=[pl.BlockSpec((tm, d), lambda i: (i, 0)), pl.BlockSpec((1, d), lambda i: (0, 0)),
                  pl.BlockSpec((d, 2 * LANES), lambda i: (0, 0)), pl.BlockSpec((1, LANES), lambda i: (0, 0))],
        out_specs=[pl.BlockSpec((1, LANES, tm), lambda i: (i // tps, 0, i % tps)),
                   pl.BlockSpec((tm, LANES), lambda i: (i, 0))],
        out_shape=[jax.ShapeDtypeStruct((batch, LANES, seq), F32), jax.ShapeDtypeStruct((t, LANES), F32)],
        scratch_shapes=[pltpu.VMEM((1, LANES), F32)],
        compiler_params=_params("arbitrary"),
        name="fox_forget_cumsum",
    )(h, gain.reshape(1, d), w12, b)


def _router_kernel(h_ref, g_ref, w_ref, b_ref, t_ref, meta_ref, metat_ref, cnt_ref, carry_sc):
    i = pl.program_id(0)

    @pl.when(i == 0)
    def _():
        carry_sc[...] = jnp.zeros(carry_sc.shape, F32)

    t = _rms(h_ref[...], g_ref[...])
    _to_slabs(t_ref, _pack_halves(t))

    logits = _split_dot(t, w_ref) + b_ref[...]
    tm = logits.shape[0]
    lane = lax.broadcasted_iota(I32, logits.shape, 1)
    lane_f = lane.astype(F32)
    first = lambda hit: jnp.min(jnp.where(hit, lane_f, float(LANES)), axis=1, keepdims=True).astype(I32)

    is_grp = lane < MOE_GROUPS
    gl = jnp.where(is_grp, logits, -jnp.inf)
    gmax = jnp.max(gl, axis=1, keepdims=True)
    g_sel = first(gl == gmax)
    gexp = jnp.where(is_grp, jnp.exp(logits - gmax), 0.0)
    g_w = 1.0 / jnp.sum(gexp, axis=1, keepdims=True)

    lo = MOE_GROUPS + MOE_EXPERTS_PER_GROUP * g_sel
    in_grp = jnp.logical_and(lane >= lo, lane < lo + MOE_EXPERTS_PER_GROUP)
    el = jnp.where(in_grp, logits, -jnp.inf)
    emax = jnp.max(el, axis=1, keepdims=True)
    eexp = jnp.where(in_grp, jnp.exp(logits - emax), 0.0)
    prob = eexp / jnp.sum(eexp, axis=1, keepdims=True)
    cand1 = jnp.where(in_grp, prob, -1.0)
    p1 = jnp.max(cand1, axis=1, keepdims=True)
    j1 = first(cand1 == p1)
    cand2 = jnp.where(lane == j1, -1.0, cand1)
    p2 = jnp.max(cand2, axis=1, keepdims=True)
    j2 = first(cand2 == p2)
    denom = p1 + p2
    wt1 = p1 / denom * g_w
    wt2 = p2 / denom * g_w
    e1 = j1 - MOE_GROUPS
    e2 = j2 - MOE_GROUPS

    hit1 = lane == e1
    hit2 = lane == e2
    row = lax.broadcasted_iota(I32, (tm, tm), 0)
    col = lax.broadcasted_iota(I32, (tm, tm), 1)
    before = (col < row).astype(BF16)
    pre = _dot(before, jnp.concatenate([hit1.astype(BF16), hit2.astype(BF16)], axis=1))
    pre1, pre2 = pre[:, :LANES], pre[:, LANES:]
    carry = carry_sc[...]
    cnt1 = jnp.sum(hit1.astype(F32), axis=0, keepdims=True)
    cnt2 = jnp.sum(hit2.astype(F32), axis=0, keepdims=True)
    rank1 = jnp.sum(jnp.where(hit1, pre1 + carry, 0.0), axis=1, keepdims=True)
    rank2 = jnp.sum(jnp.where(hit2, pre2 + (carry + cnt1), 0.0), axis=1, keepdims=True)
    total = carry + cnt1 + cnt2
    carry_sc[...] = total
    cnt_ref[...] = jnp.broadcast_to(total, cnt_ref.shape)

    bits = lambda x: lax.bitcast_convert_type(jnp.broadcast_to(x, logits.shape), I32)
    meta = jnp.where(lane == 0, e1, 0)
    meta = jnp.where(lane == 1, e2, meta)
    meta = jnp.where(lane == 2, rank1.astype(I32), meta)
    meta = jnp.where(lane == 3, rank2.astype(I32), meta)
    meta = jnp.where(lane == 4, bits(wt1), meta)
    meta = jnp.where(lane == 5, bits(wt2), meta)
    meta_ref[...] = meta
    metat_ref[...] = meta.T[:SUBLANES, :]


def _router(h, gain, w_grp, b_grp, w_rt, b_rt):
    t, d = h.shape
    tm = _tile(t, ROUTE_TILE)
    slab = d // 2 // LANES
    n_used = MOE_GROUPS + MOE_EXPERTS
    w = jnp.pad(jnp.concatenate([w_grp, w_rt], axis=1), ((0, 0), (0, LANES - n_used)))
    w12 = jnp.concatenate(_split(w, 2), axis=1)
    b = jnp.pad(jnp.concatenate([b_grp, b_rt]), (0, LANES - n_used)).reshape(1, LANES)
    return pl.pallas_call(
        _router_kernel,
        grid=(t // tm,),
        in_specs=[pl.BlockSpec((tm, d), lambda i: (i, 0)), pl.BlockSpec((1, d), lambda i: (0, 0)),
                  pl.BlockSpec((d, 2 * LANES), lambda i: (0, 0)), pl.BlockSpec((1, LANES), lambda i: (0, 0))],
        out_specs=[pl.BlockSpec((tm * slab, LANES), lambda i: (i, 0)),
                   pl.BlockSpec((tm, LANES), lambda i: (i, 0)),
                   pl.BlockSpec((SUBLANES, tm), lambda i: (0, i)),
                   pl.BlockSpec((SUBLANES, LANES), lambda i: (0, 0))],
        out_shape=[jax.ShapeDtypeStruct((t * slab, LANES), U32),
                   jax.ShapeDtypeStruct((t, LANES), I32),
                   jax.ShapeDtypeStruct((SUBLANES, t), I32),
                   jax.ShapeDtypeStruct((SUBLANES, LANES), F32)],
        scratch_shapes=[pltpu.VMEM((1, LANES), F32)],
        compiler_params=_params("arbitrary"),
        name="moe_router",
    )(h, gain.reshape(1, d), w12, b)


def _dest_kernel(offs_ref, mt_ref, dest_ref):
    mt = mt_ref[...]
    experts = mt[0:2, :]
    base = jnp.zeros(experts.shape, I32)
    for e in range(MOE_EXPERTS):
        base = jnp.where(experts == e, offs_ref[e], base)
    row = lax.broadcasted_iota(I32, mt.shape, 0)
    dest_ref[...] = jnp.where(row < 2, jnp.concatenate([base + mt[2:4, :], mt[2:SUBLANES, :]], axis=0), 0)


def _dest_rows_all(meta_t, offs):
    rows, t = meta_t.shape
    return pl.pallas_call(
        _dest_kernel,
        grid_spec=pltpu.PrefetchScalarGridSpec(
            num_scalar_prefetch=1,
            grid=(1,),
            in_specs=[pl.BlockSpec((rows, t), lambda i, offs: (0, 0))],
            out_specs=pl.BlockSpec((rows, t), lambda i, offs: (0, 0)),
        ),
        out_shape=jax.ShapeDtypeStruct((rows, t), I32),
        compiler_params=_params("arbitrary"),
        name="moe_dest",
    )(offs, meta_t)


def _slab_copy(src, src_tok, dst, dst_tok, sem, slab):
    rows = lambda tok: pl.ds(pl.multiple_of(tok * slab, slab), slab)
    return pltpu.make_async_copy(src.at[rows(src_tok)], dst.at[rows(dst_tok)], sem)


_ROW_UNROLL = 8


def _dispatch_kernel(zs_ref, nv_ref, dest_ref, t_ref, xs_ref, zero_sc, sem, *, tm, tz, slab):
    i = pl.program_id(0)

    @pl.when(i == 0)
    def _():
        zero_sc[...] = jnp.zeros(zero_sc.shape, zero_sc.dtype)
        fill = lambda tok0: pltpu.make_async_copy(
            zero_sc, xs_ref.at[pl.ds(pl.multiple_of(tok0 * slab, tz * slab), tz * slab)], sem)
        fills = [fill(zs_ref[e]) for e in range(MOE_EXPERTS)]
        for c in fills:
            c.start()
        for c in fills:
            c.wait()

        def tail(tile, carry):
            c = fill(tile * tz)
            c.start()
            c.wait()
            return carry

        lax.fori_loop(nv_ref[0], xs_ref.shape[0] // (tz * slab), tail, 0)

    def issue(blk, carry):
        for u in range(_ROW_UNROLL):
            r = blk * _ROW_UNROLL + u
            _slab_copy(t_ref, r, xs_ref, dest_ref[0, r], sem, slab).start(priority=0)
            _slab_copy(t_ref, r, xs_ref, dest_ref[1, r], sem, slab).start(priority=1)
        return carry

    lax.fori_loop(0, tm // _ROW_UNROLL, issue, 0)

    def drain(blk, carry):
        for u in range(2 * _ROW_UNROLL):
            _slab_copy(t_ref, 0, xs_ref, 0, sem, slab).wait()
        return carry

    lax.fori_loop(0, tm // _ROW_UNROLL, drain, 0)


def _dispatch(t_slabs, dest_t, zero_start, n_valid, n_rows):
    t = dest_t.shape[1]
    slab = t_slabs.shape[0] // t
    tm = _tile(t, MOVE_TILE)
    tz = EXPERT_TILE
    mspec = pl.BlockSpec((SUBLANES, tm), lambda i, *_: (0, i), memory_space=pltpu.SMEM)
    return pl.pallas_call(
        functools.partial(_dispatch_kernel, tm=tm, tz=tz, slab=slab),
        grid_spec=pltpu.PrefetchScalarGridSpec(
            num_scalar_prefetch=2,
            grid=(t // tm,),
            in_specs=[mspec, pl.BlockSpec((tm * slab, LANES), lambda i, *_: (i, 0))],
            out_specs=pl.BlockSpec(memory_space=pl.ANY),
            scratch_shapes=[pltpu.VMEM((tz * slab, LANES), U32), pltpu.SemaphoreType.DMA(())],
        ),
        out_shape=jax.ShapeDtypeStruct((n_rows * slab, LANES), U32),
        compiler_params=_params("arbitrary"),
        name="moe_dispatch",
    )(zero_start, n_valid, dest_t, t_slabs)


def _experts_kernel(te_ref, nv_ref, x_ref, wg_ref, wu_ref, wd_ref, y_ref, wg_sc, wu_sc, wd_sc, *, tm):
    i = pl.program_id(0)
    live = i < nv_ref[0]

    @pl.when(jnp.logical_or(i == 0, te_ref[i] != te_ref[jnp.maximum(i - 1, 0)]))
    def _():
        wg_sc[...] = wg_ref[0, 0].astype(BF16)
        wu_sc[...] = wu_ref[0, 0].astype(BF16)
        wd_sc[...] = wd_ref[0, 0].astype(BF16)

    @pl.when(jnp.logical_not(live))
    def _():
        y_ref[...] = jnp.zeros(y_ref.shape, y_ref.dtype)

    @pl.when(live)
    def _():
        hi, lo = _unpack_halves(_from_slabs(x_ref, tm))
        x = jnp.concatenate([hi, lo], axis=1).astype(BF16)
        g = _dot(x, wg_sc[...])
        u = _dot(x, wu_sc[...])
        hid = (g * jax.nn.sigmoid(g) * u).astype(BF16)
        _to_slabs(y_ref, _pack_halves(_dot(hid, wd_sc[...])))


def _experts(xs, tile_expert, n_valid, w_gate, w_up, w_down, layer, n_tiles):
    tm = EXPERT_TILE
    _, _, d, f = w_gate.shape
    rows = tm * (d // 2 // LANES)
    live = lambda i, nv: jnp.minimum(i, nv[0] - 1)
    return pl.pallas_call(
        functools.partial(_experts_kernel, tm=tm),
        grid_spec=pltpu.PrefetchScalarGridSpec(
            num_scalar_prefetch=2,
            grid=(n_tiles,),
            in_specs=[pl.BlockSpec((rows, LANES), lambda i, te, nv: (live(i, nv), 0)),
                      pl.BlockSpec((1, 1, d, f), lambda i, te, nv: (layer, te[i], 0, 0)),
                      pl.BlockSpec((1, 1, d, f), lambda i, te, nv: (layer, te[i], 0, 0)),
                      pl.BlockSpec((1, 1, f, d), lambda i, te, nv: (layer, te[i], 0, 0))],
            out_specs=pl.BlockSpec((rows, LANES), lambda i, te, nv: (i, 0)),
            scratch_shapes=[pltpu.VMEM((d, f), BF16), pltpu.VMEM((d, f), BF16), pltpu.VMEM((f, d), BF16)],
        ),
        out_shape=jax.ShapeDtypeStruct((n_tiles * rows, LANES), U32),
        compiler_params=_params("arbitrary"),
        name="moe_experts",
    )(tile_expert, n_valid, xs, w_gate, w_up, w_down)


def _combine_kernel(dest_ref, dnext_ref, meta_ref, h_ref, g_ref, ys_ref, *rest, tm, slab, final):
    if final:
        o_ref, y_sc, sem = rest
    else:
        hn_ref, a_ref, y_sc, sem = rest
    i = pl.program_id(0)
    n = pl.num_programs(0)

    def gather(dref, slot):
        def issue(blk, carry):
            for u in range(_ROW_UNROLL):
                r = blk * _ROW_UNROLL + u
                _slab_copy(ys_ref, dref[0, r], y_sc.at[slot, 0], r, sem.at[slot], slab).start(priority=0)
                _slab_copy(ys_ref, dref[1, r], y_sc.at[slot, 1], r, sem.at[slot], slab).start(priority=1)
            return carry

        lax.fori_loop(0, tm // _ROW_UNROLL, issue, 0)

    def finish(slot):
        def drain(blk, carry):
            for u in range(2 * _ROW_UNROLL):
                _slab_copy(ys_ref, 0, y_sc.at[slot, 0], 0, sem.at[slot], slab).wait()
            return carry

        lax.fori_loop(0, tm // _ROW_UNROLL, drain, 0)

        meta = meta_ref[...]
        lane = lax.broadcasted_iota(I32, meta.shape, 1)
        wbits = lax.bitcast_convert_type(meta, F32)
        wt1 = jnp.sum(jnp.where(lane == 4, wbits, 0.0), axis=1, keepdims=True)
        wt2 = jnp.sum(jnp.where(lane == 5, wbits, 0.0), axis=1, keepdims=True)
        hi1, lo1 = _unpack_halves(_from_slabs(y_sc.at[slot, 0], tm))
        hi2, lo2 = _unpack_halves(_from_slabs(y_sc.at[slot, 1], tm))
        moe = jnp.concatenate([wt1 * hi1 + wt2 * hi2, wt1 * lo1 + wt2 * lo2], axis=1)
        h_new = h_ref[...] + moe
        if final:
            o_ref[...] = _rms(h_new, g_ref[...])
        else:
            hn_ref[...] = h_new
            a_ref[...] = _rms(h_new, g_ref[...]).astype(a_ref.dtype)

    @pl.when(i == 0)
    def _():
        gather(dest_ref, 0)

    for slot in (0, 1):
        @pl.when((i & 1) == slot)
        def _():
            @pl.when(i + 1 < n)
            def _():
                gather(dnext_ref, 1 - slot)

            finish(slot)


def _combine(ys, dest_t, meta, h, gain, final):
    t, d = h.shape
    tm = _tile(t, MOVE_TILE)
    slab = d // 2 // LANES
    n = t // tm
    mspec = pl.BlockSpec((SUBLANES, tm), lambda i: (0, i), memory_space=pltpu.SMEM)
    mnext = pl.BlockSpec((SUBLANES, tm), lambda i: (0, jnp.minimum(i + 1, n - 1)), memory_space=pltpu.SMEM)
    row = pl.BlockSpec((tm, d), lambda i: (i, 0))
    if final:
        out_specs, out_shape = row, jax.ShapeDtypeStruct((t, d), F32)
    else:
        out_specs = [row, row]
        out_shape = [jax.ShapeDtypeStruct((t, d), F32), jax.ShapeDtypeStruct((t, d), BF16)]
    return pl.pallas_call(
        functools.partial(_combine_kernel, tm=tm, slab=slab, final=final),
        grid=(n,),
        in_specs=[mspec, mnext, pl.BlockSpec((tm, LANES), lambda i: (i, 0)), row,
                  pl.BlockSpec((1, d), lambda i: (0, 0)), pl.BlockSpec(memory_space=pl.ANY)],
        out_specs=out_specs,
        out_shape=out_shape,
        scratch_shapes=[pltpu.VMEM((2, 2, tm * slab, LANES), U32), pltpu.SemaphoreType.DMA((2,))],
        compiler_params=_params("arbitrary"),
        name="moe_combine",
    )(dest_t, dest_t, meta, h, gain.reshape(1, d), ys)


def _hier_moe(h, ffn_gain, w_grp, b_grp, w_rt, b_rt, w_gate, w_up, w_down, layer, next_gain, final):
    t, d = h.shape
    tmx = EXPERT_TILE
    t_slabs, meta, meta_t, cnt = _router(h, ffn_gain, w_grp, b_grp, w_rt, b_rt)

    counts = cnt[0, :MOE_EXPERTS].astype(I32)
    padded = (counts + tmx - 1) // tmx * tmx
    ends = jnp.cumsum(padded)
    offs = ends - padded
    n_tiles = (2 * t) // tmx + MOE_EXPERTS
    n_valid = (ends[-1] // tmx).reshape(1)
    tile_start = jnp.minimum(jnp.arange(n_tiles, dtype=I32) * tmx, ends[-1] - 1)
    tile_expert = jnp.sum((ends[None, :] <= tile_start[:, None]).astype(I32), axis=1)
    zero_start = offs + counts // tmx * tmx

    dest_t = _dest_rows_all(meta_t, offs)
    xs = _dispatch(t_slabs, dest_t, zero_start, n_valid, (n_tiles + 1) * tmx)
    ys = _experts(xs, tile_expert, n_valid, w_gate, w_up, w_down, layer, n_tiles)
    return _combine(ys, dest_t, meta, h, next_gain, final)


def kernel(x, positions, attn_norm, ffn_norm, final_norm, mla_w_dq, mla_q_norm, mla_w_uq, mla_w_dkv,
           mla_kv_norm, mla_w_ukv, mla_w_o, fox_w_qkv, fox_q_norm, fox_k_norm, fox_w_f, fox_b_f,
           fox_w_og, fox_w_o, moe_w_grp, moe_b_grp, moe_w_rt, moe_b_rt, moe_w_gate, moe_w_up, moe_w_down):
    batch, seq, d = x.shape
    depth = attn_norm.shape[0]
    t = batch * seq
    cos_t, sin_t = _rope_tables(positions)
    h = x.reshape(t, d)
    a = _norm(h, attn_norm[0], BF16)
    out = None
    for i in range(depth):
        j = i // 2
        if i % 2 == 0:
            heads = mla_w_uq.shape[2] // (MLA_NOPE + MLA_ROPE)
            q, k, v_t = _mla_project(a, cos_t, sin_t, mla_w_dq[j], mla_q_norm[j], mla_w_uq[j],
                                     mla_w_dkv[j], mla_kv_norm[j], mla_w_ukv[j])
            o = _mla_attention(q, k, v_t, batch, seq, heads)
            w_o = mla_w_o[j]
        else:
            dh = FOX_HEAD_DIM
            heads = fox_w_qkv.shape[2] // (3 * dh)
            w_qkv = fox_w_qkv[j].astype(BF16)
            gain_row = jnp.concatenate([jnp.tile(fox_q_norm[j] * (dh ** -0.5 * LOG2_E), heads),
                                        jnp.tile(fox_k_norm[j], heads)])[None, :]
            qk = _fox_qk(a, w_qkv[:, :2 * heads * dh], gain_row)
            v_t = _mm_nt(w_qkv[:, 2 * heads * dh:].T, a, BF16, name="fox_v")
            gate = _mm(a, fox_w_og[j].astype(BF16), BF16, act="sigmoid", name="fox_gate")
            c_t, c_tok = _fox_forget_cumsum(h, attn_norm[i], fox_w_f[j], fox_b_f[j], batch, seq)
            c_rows = c_t[:, :heads, :].reshape(batch, heads, 1, seq)
            o = _fox_attention(qk, v_t, c_tok, c_rows, gate, batch, seq, heads)
            w_o = fox_w_o[j]
        h = _mm_residual(o, w_o.astype(BF16), h)
        final = i == depth - 1
        next_gain = final_norm if final else attn_norm[i + 1]
        res = _hier_moe(h, ffn_norm[i], moe_w_grp[i], moe_b_grp[i], moe_w_rt[i], moe_b_rt[i],
                        moe_w_gate, moe_w_up, moe_w_down, i, next_gain, final)
        if final:
            out = res
        else:
            h, a = res
    return out.reshape(batch, seq, d)
```

```python
import functools

import jax
import jax.numpy as jnp
from jax import lax
from jax.experimental import pallas as pl
from jax.experimental.pallas import tpu as pltpu

F32 = jnp.float32
BF16 = jnp.bfloat16
I32 = jnp.int32
U32 = jnp.uint32

RMS_EPS = 1e-6
NEG_INF = -1e30
CHUNK = 64
MLA_NOPE = 128
MLA_ROPE = 64
MLA_V = 128
MLA_QK_PAD = 256
ROPE_THETA = 10000.0
LOG2_E = 1.4426950408889634
FOX_HEAD_DIM = 128
MOE_GROUPS = 8
MOE_EXPERTS_PER_GROUP = 4
MOE_EXPERTS = MOE_GROUPS * MOE_EXPERTS_PER_GROUP

LANES = 128
SUBLANES = 8
V7X_VMEM_LIMIT_BYTES = 56 * 1024 * 1024

ROW_TILE = 1024
COL_TILE = 1024
ATTN_TILE = 512
ROUTE_TILE = 512
MOVE_TILE = 512
EXPERT_TILE = 256


def _params(*sem):
    return pltpu.CompilerParams(dimension_semantics=sem, vmem_limit_bytes=V7X_VMEM_LIMIT_BYTES)


def _tile(n, t):
    if n <= t:
        return n
    step = LANES if t % LANES == 0 else SUBLANES
    for c in range(t - t % step, 0, -step):
        if n % c == 0:
            return c
    raise ValueError(f"no aligned tile for {n} under {t}")


def _rms(x, gain):
    ms = jnp.mean(x * x, axis=-1, keepdims=True)
    return x * lax.rsqrt(ms + RMS_EPS) * gain


def _dot(a, b):
    return jnp.dot(a, b, preferred_element_type=F32)


def _nt_dot(a, b):
    return lax.dot_general(a, b, (((1,), (1,)), ((), ())), preferred_element_type=F32)


def _pack_halves(x):
    n = x.shape[1] // 2
    hi = lax.bitcast_convert_type(x[:, :n].astype(BF16).astype(F32), U32)
    lo = lax.bitcast_convert_type(x[:, n:].astype(BF16).astype(F32), U32)
    return hi | lax.shift_right_logical(lo, jnp.uint32(16))


def _unpack_halves(p):
    hi = lax.bitcast_convert_type(p & jnp.uint32(0xFFFF0000), F32)
    lo = lax.bitcast_convert_type(lax.shift_left(p, jnp.uint32(16)), F32)
    return hi, lo


def _to_slabs(ref, x):
    m, width = x.shape
    c = width // LANES
    for j in range(c):
        ref[pl.ds(j, m, stride=c), :] = x[:, j * LANES:(j + 1) * LANES]


def _from_slabs(ref, m):
    c = ref.shape[0] // m
    return jnp.concatenate([ref[pl.ds(j, m, stride=c), :] for j in range(c)], axis=1)


def _split(x, terms):
    out = []
    for _ in range(terms):
        hi = x.astype(BF16)
        out.append(hi)
        x = x - hi.astype(F32)
    return out


def _split_dot(x, w12_ref):
    m = x.shape[0]
    n = w12_ref.shape[1] // 2
    prod = _dot(jnp.concatenate(_split(x, 2), axis=0), w12_ref[...])
    return (prod[:m, :n] + (prod[:m, n:] + prod[m:, :n])) + prod[m:, n:]


def _rope_lanes(seg, cos_t, sin_t):
    half = MLA_ROPE // 2
    lane = lax.broadcasted_iota(I32, seg.shape, 1)
    swapped = jnp.where(lane < half, pltpu.roll(seg, LANES - half, 1), pltpu.roll(seg, half, 1))
    return seg * cos_t + swapped * sin_t


def _rope_table_kernel(pos_ref, freq_ref, cos_ref, sin_ref):
    ang = pos_ref[...] * freq_ref[...]
    lane = lax.broadcasted_iota(I32, ang.shape, 1)
    half = MLA_ROPE // 2
    valid = lane < MLA_ROPE
    cos_ref[...] = jnp.where(valid, jnp.cos(ang), 0.0)
    sin_ref[...] = jnp.where(valid, jnp.where(lane < half, -jnp.sin(ang), jnp.sin(ang)), 0.0)


def _rope_tables(positions):
    t = positions.size
    tm = _tile(t, ROW_TILE)
    half = MLA_ROPE // 2
    inv_freq = ROPE_THETA ** (-jnp.arange(0, MLA_ROPE, 2, dtype=F32) / MLA_ROPE)
    freq_row = jnp.concatenate([inv_freq, inv_freq, jnp.zeros((LANES - 2 * half,), F32)])[None, :]
    pos = positions.reshape(t, 1).astype(F32)
    out = jax.ShapeDtypeStruct((t, LANES), F32)
    return pl.pallas_call(
        _rope_table_kernel,
        grid=(t // tm,),
        in_specs=[pl.BlockSpec((tm, 1), lambda i: (i, 0)), pl.BlockSpec((1, LANES), lambda i: (0, 0))],
        out_specs=[pl.BlockSpec((tm, LANES), lambda i: (i, 0))] * 2,
        out_shape=[out, out],
        compiler_params=_params("parallel"),
        name="rope_tables",
    )(pos, freq_row)


def _norm_kernel(h_ref, g_ref, a_ref):
    a_ref[...] = _rms(h_ref[...], g_ref[...]).astype(a_ref.dtype)


def _norm(h, gain, out_dtype):
    t, d = h.shape
    tm = _tile(t, ROUTE_TILE)
    return pl.pallas_call(
        _norm_kernel,
        grid=(t // tm,),
        in_specs=[pl.BlockSpec((tm, d), lambda i: (i, 0)), pl.BlockSpec((1, d), lambda i: (0, 0))],
        out_specs=pl.BlockSpec((tm, d), lambda i: (i, 0)),
        out_shape=jax.ShapeDtypeStruct((t, d), out_dtype),
        compiler_params=_params("parallel"),
        name="rmsnorm",
    )(h, gain.reshape(1, d))


def _mm_kernel(x_ref, w_ref, o_ref, *, act):
    y = _dot(x_ref[...], w_ref[...])
    if act == "sigmoid":
        y = jax.nn.sigmoid(y)
    o_ref[...] = y.astype(o_ref.dtype)


def _mm(x, w, out_dtype, act=None, name="mm", col_tile=COL_TILE):
    m, k = x.shape
    n = w.shape[1]
    tm, tn = _tile(m, ROW_TILE), _tile(n, col_tile)
    return pl.pallas_call(
        functools.partial(_mm_kernel, act=act),
        grid=(m // tm, n // tn),
        in_specs=[pl.BlockSpec((tm, k), lambda i, j: (i, 0)), pl.BlockSpec((k, tn), lambda i, j: (0, j))],
        out_specs=pl.BlockSpec((tm, tn), lambda i, j: (i, j)),
        out_shape=jax.ShapeDtypeStruct((m, n), out_dtype),
        compiler_params=_params("parallel", "parallel"),
        name=name,
    )(x, w)


def _mm_nt_kernel(wt_ref, x_ref, o_ref):
    o_ref[...] = _nt_dot(wt_ref[...], x_ref[...]).astype(o_ref.dtype)


def _mm_nt(w_t, x, out_dtype, name):
    n, k = w_t.shape
    m = x.shape[0]
    tm, tn = _tile(m, ROW_TILE), _tile(n, COL_TILE)
    return pl.pallas_call(
        _mm_nt_kernel,
        grid=(m // tm, n // tn),
        in_specs=[pl.BlockSpec((tn, k), lambda i, j: (j, 0)), pl.BlockSpec((tm, k), lambda i, j: (i, 0))],
        out_specs=pl.BlockSpec((tn, tm), lambda i, j: (j, i)),
        out_shape=jax.ShapeDtypeStruct((n, m), out_dtype),
        compiler_params=_params("parallel", "parallel"),
        name=name,
    )(w_t, x)


def _mm_res_kernel(x_ref, w_ref, h_ref, o_ref):
    o_ref[...] = h_ref[...] + _dot(x_ref[...], w_ref[...])


def _mm_residual(x, w, h):
    m, k = x.shape
    n = w.shape[1]
    tm, tn = _tile(m, ROW_TILE), _tile(n, COL_TILE)
    return pl.pallas_call(
        _mm_res_kernel,
        grid=(m // tm, n // tn),
        in_specs=[pl.BlockSpec((tm, k), lambda i, j: (i, 0)), pl.BlockSpec((k, tn), lambda i, j: (0, j)),
                  pl.BlockSpec((tm, tn), lambda i, j: (i, j))],
        out_specs=pl.BlockSpec((tm, tn), lambda i, j: (i, j)),
        out_shape=jax.ShapeDtypeStruct((m, n), F32),
        compiler_params=_params("parallel", "parallel"),
        name="out_proj_residual",
    )(x, w, h)


def _mla_q_kernel(c_ref, g_ref, w_ref, cos_ref, sin_ref, q_ref, *, scale):
    cq = _rms(c_ref[...], g_ref[...]).astype(BF16)
    y = _dot(cq, w_ref[...]) * scale
    cos_t, sin_t = cos_ref[...], sin_ref[...]
    for hd in range(y.shape[1] // MLA_QK_PAD):
        base = hd * MLA_QK_PAD
        q_ref[:, base:base + MLA_NOPE] = y[:, base:base + MLA_NOPE].astype(q_ref.dtype)
        roped = _rope_lanes(y[:, base + MLA_NOPE:base + MLA_QK_PAD], cos_t, sin_t)
        q_ref[:, base + MLA_NOPE:base + MLA_QK_PAD] = roped.astype(q_ref.dtype)


def _mla_kv_kernel(c_ref, pe_ref, g_ref, wk_ref, wvt_ref, cos_ref, sin_ref, k_ref, vt_ref):
    ckv = _rms(c_ref[...], g_ref[...]).astype(BF16)
    kn = _dot(ckv, wk_ref[...])
    vt_ref[...] = _nt_dot(wvt_ref[...], ckv).astype(vt_ref.dtype)
    k_pe = _rope_lanes(pe_ref[...], cos_ref[...], sin_ref[...]).astype(k_ref.dtype)
    for hd in range(kn.shape[1] // MLA_NOPE):
        k_ref[:, hd * MLA_QK_PAD:hd * MLA_QK_PAD + MLA_NOPE] = (
            kn[:, hd * MLA_NOPE:(hd + 1) * MLA_NOPE].astype(k_ref.dtype))
        k_ref[:, hd * MLA_QK_PAD + MLA_NOPE:(hd + 1) * MLA_QK_PAD] = k_pe


def _mla_project(a, cos_t, sin_t, w_dq, q_norm, w_uq, w_dkv, kv_norm, w_ukv):
    t, d = a.shape
    q_lora = w_dq.shape[1]
    kv_lora = w_dkv.shape[1] - MLA_ROPE
    heads = w_uq.shape[1] // (MLA_NOPE + MLA_ROPE)
    assert q_lora % LANES == 0 and kv_lora % LANES == 0

    w_down = jnp.concatenate(
        [w_dq, w_dkv, jnp.zeros((d, LANES - MLA_ROPE), w_dkv.dtype)], axis=1).astype(BF16)
    s1 = _mm(a, w_down, F32, name="mla_down", col_tile=w_down.shape[1])

    w_q = w_uq.reshape(q_lora, heads, MLA_NOPE + MLA_ROPE)
    w_q = jnp.pad(w_q, ((0, 0), (0, 0), (0, MLA_QK_PAD - MLA_NOPE - MLA_ROPE)))
    w_q = w_q.reshape(q_lora, heads * MLA_QK_PAD).astype(BF16)
    w_kv = w_ukv.reshape(kv_lora, heads, MLA_NOPE + MLA_V)
    w_k = w_kv[:, :, :MLA_NOPE].reshape(kv_lora, heads * MLA_NOPE).astype(BF16)
    w_vt = w_kv[:, :, MLA_NOPE:].reshape(kv_lora, heads * MLA_V).T.astype(BF16)

    tm = _tile(t, ROW_TILE)
    scale = (MLA_NOPE + MLA_ROPE) ** -0.5 * LOG2_E
    tn = heads * MLA_QK_PAD
    row128 = pl.BlockSpec((tm, LANES), lambda i, j: (i, 0))
    q = pl.pallas_call(
        functools.partial(_mla_q_kernel, scale=scale),
        grid=(t // tm, heads * MLA_QK_PAD // tn),
        in_specs=[pl.BlockSpec((tm, q_lora), lambda i, j: (i, 0)),
                  pl.BlockSpec((1, q_lora), lambda i, j: (0, 0)),
                  pl.BlockSpec((q_lora, tn), lambda i, j: (0, j)), row128, row128],
        out_specs=pl.BlockSpec((tm, tn), lambda i, j: (i, j)),
        out_shape=jax.ShapeDtypeStruct((t, heads * MLA_QK_PAD), BF16),
        compiler_params=_params("parallel", "parallel"),
        name="mla_q",
    )(s1, q_norm.reshape(1, q_lora), w_q, cos_t, sin_t)

    hb = heads
    kv_blk = q_lora // kv_lora
    assert q_lora % kv_lora == 0
    pe_blk = (q_lora + kv_lora) // LANES
    k, v_t = pl.pallas_call(
        _mla_kv_kernel,
        grid=(t // tm, heads // hb),
        in_specs=[pl.BlockSpec((tm, kv_lora), lambda i, j: (i, kv_blk)),
                  pl.BlockSpec((tm, LANES), lambda i, j: (i, pe_blk)),
                  pl.BlockSpec((1, kv_lora), lambda i, j: (0, 0)),
                  pl.BlockSpec((kv_lora, hb * MLA_NOPE), lambda i, j: (0, j)),
                  pl.BlockSpec((hb * MLA_V, kv_lora), lambda i, j: (j, 0)), row128, row128],
        out_specs=[pl.BlockSpec((tm, hb * MLA_QK_PAD), lambda i, j: (i, j)),
                   pl.BlockSpec((hb * MLA_V, tm), lambda i, j: (j, i))],
        out_shape=[jax.ShapeDtypeStruct((t, heads * MLA_QK_PAD), BF16),
                   jax.ShapeDtypeStruct((heads * MLA_V, t), BF16)],
        compiler_params=_params("parallel", "parallel"),
        name="mla_kv",
    )(s1, s1, kv_norm.reshape(1, kv_lora), w_k, w_vt, cos_t, sin_t)
    return q, k, v_t


def _col_max(x):
    while x.shape[0] > SUBLANES and x.shape[0] % (2 * SUBLANES) == 0:
        half = x.shape[0] // 2
        x = jnp.maximum(x[:half], x[half:])
    return jnp.max(x, axis=0, keepdims=True)


def _softmax_update(s_t, m_cur, v_t, m_sc, l_sc, acc_sc, col_shift=None):
    m_prev = m_sc[...]
    if col_shift is not None:
        m_cur = m_cur + col_shift
    m_new = jnp.maximum(m_prev, m_cur)
    shift = m_new if col_shift is None else m_new - col_shift
    p_t = jnp.exp2(s_t - shift)
    alpha = jnp.exp2(m_prev - m_new)
    l_sc[...] = alpha * l_sc[...] + jnp.sum(p_t, axis=0, keepdims=True)
    acc_sc[...] = alpha * acc_sc[...] + _dot(v_t, p_t.astype(v_t.dtype))
    m_sc[...] = m_new


def _init_softmax(m_sc, l_sc, acc_sc):
    m_sc[...] = jnp.full(m_sc.shape, NEG_INF, F32)
    l_sc[...] = jnp.zeros(l_sc.shape, F32)
    acc_sc[...] = jnp.zeros(acc_sc.shape, F32)


def _row_sweep(nq, trips_ref, refresh_q, scores, consume, consume_diag, bufs):
    def produce(kt, s_buf, m_buf):
        s_t = scores(kt)
        s_buf[...] = s_t
        m_buf[...] = _col_max(s_t)

    refresh_q(0)
    produce(0, *bufs[0])
    first = 0
    for qi in range(nq):
        cur, other = bufs[first], bufs[1 - first]

        def pair(p, carry, qi=qi, cur=cur, other=other):
            produce(2 * p + 1, *other)
            consume(qi, 2 * p, *cur)
            produce(2 * p + 2, *cur)
            consume(qi, 2 * p + 1, *other)
            return carry

        lax.fori_loop(0, trips_ref[qi], pair, 0)
        if qi % 2 == 1:
            produce(qi, *other)
            consume(qi, qi - 1, *cur)
            cur, other, first = other, cur, 1 - first
        if qi + 1 < nq:
            refresh_q(qi + 1)
            produce(0, *other)
        consume_diag(qi, *cur)
        first = 1 - first


def _mla_attn_kernel(trips_ref, q_ref, k_ref, vt_ref, o_ref, m_sc, l_sc, acc_sc, sa_sc, ma_sc, sb_sc, mb_sc,
                     qt_sc, *, tq, nq, chunk_shift):
    _init_softmax(m_sc, l_sc, acc_sc)
    rows = lambda i: pl.ds(pl.multiple_of(i * tq, tq), tq)

    def refresh_q(qi):
        qt_sc[...] = q_ref[rows(qi), :].T

    def scores(kt):
        return _dot(k_ref[rows(kt), :], qt_sc[...])

    def consume(qi, kt, s_buf, m_buf):
        _softmax_update(s_buf[...], m_buf[...], vt_ref[:, rows(kt)], m_sc, l_sc, acc_sc)

    def consume_diag(qi, s_buf, m_buf):
        s_t = s_buf[...]
        key = lax.broadcasted_iota(I32, s_t.shape, 0)
        qry = lax.broadcasted_iota(I32, s_t.shape, 1)
        allowed = lax.shift_right_logical(key, chunk_shift) <= lax.shift_right_logical(qry, chunk_shift)
        s_t = jnp.where(allowed, s_t, NEG_INF)
        _softmax_update(s_t, _col_max(s_t), vt_ref[:, rows(qi)], m_sc, l_sc, acc_sc)
        o_ref[rows(qi), :] = (acc_sc[...] / l_sc[...]).T.astype(o_ref.dtype)
        _init_softmax(m_sc, l_sc, acc_sc)

    _row_sweep(nq, trips_ref, refresh_q, scores, consume, consume_diag, ((sa_sc, ma_sc), (sb_sc, mb_sc)))


def _mla_attention(q, k, v_t, batch, seq, heads):
    tq = _tile(seq, ATTN_TILE)
    assert tq % CHUNK == 0 and CHUNK & (CHUNK - 1) == 0
    nq = seq // tq
    return pl.pallas_call(
        functools.partial(_mla_attn_kernel, tq=tq, nq=nq, chunk_shift=CHUNK.bit_length() - 1),
        grid_spec=pltpu.PrefetchScalarGridSpec(
            num_scalar_prefetch=1,
            grid=(batch, heads),
            in_specs=[pl.BlockSpec((seq, MLA_QK_PAD), lambda b, h, tr: (b, h)),
                      pl.BlockSpec((seq, MLA_QK_PAD), lambda b, h, tr: (b, h)),
                      pl.BlockSpec((MLA_V, seq), lambda b, h, tr: (h, b))],
            out_specs=pl.BlockSpec((seq, MLA_V), lambda b, h, tr: (b, h)),
            scratch_shapes=[pltpu.VMEM((1, tq), F32), pltpu.VMEM((1, tq), F32), pltpu.VMEM((MLA_V, tq), F32),
                            pltpu.VMEM((tq, tq), F32), pltpu.VMEM((1, tq), F32),
                            pltpu.VMEM((tq, tq), F32), pltpu.VMEM((1, tq), F32),
                            pltpu.VMEM((MLA_QK_PAD, tq), BF16)],
        ),
        out_shape=jax.ShapeDtypeStruct((batch * seq, heads * MLA_V), BF16),
        compiler_params=_params("parallel", "parallel"),
        name="mla_attention",
    )(jnp.arange(nq, dtype=I32) // 2, q, k, v_t)


_FORGET_TERMS = 3


def _fox_attn_kernel(trips_ref, q_ref, k_ref, vt_ref, ctok_ref, crow_ref, g_ref, o_ref, m_sc, l_sc, acc_sc,
                     ka_sc, sa_sc, ma_sc, sb_sc, mb_sc, qt_sc, *, tq, nq):
    head = pl.program_id(1)
    dh = k_ref.shape[1]
    _init_softmax(m_sc, l_sc, acc_sc)
    rows = lambda i: pl.ds(pl.multiple_of(i * tq, tq), tq)

    row = lax.broadcasted_iota(I32, (LANES, tq), 0)
    qt_sc[dh:, :] = jnp.where(row < _FORGET_TERMS, -1.0, 0.0).astype(qt_sc.dtype)

    ka_sc[:, :dh] = k_ref[...]
    terms = _split(ctok_ref[...] * LOG2_E, _FORGET_TERMS)
    src = lax.broadcasted_iota(I32, (LANES, LANES), 0)
    dst = lax.broadcasted_iota(I32, (LANES, LANES), 1)
    aug = None
    for j, term in enumerate(terms):
        pick = jnp.logical_and(src == head, dst == j).astype(BF16)
        part = _dot(term, pick)
        aug = part if aug is None else aug + part
    ka_sc[:, dh:] = aug.astype(ka_sc.dtype)

    def refresh_q(qi):
        qt_sc[:dh, :] = q_ref[rows(qi), :].T

    def scores(kt):
        return _dot(ka_sc[rows(kt), :], qt_sc[...])

    def c_q(qi):
        return crow_ref[0, 0, :, rows(qi)] * LOG2_E

    def consume(qi, kt, s_buf, m_buf):
        _softmax_update(s_buf[...], m_buf[...], vt_ref[:, rows(kt)], m_sc, l_sc, acc_sc, col_shift=c_q(qi))

    def consume_diag(qi, s_buf, m_buf):
        s_t = s_buf[...]
        key = lax.broadcasted_iota(I32, s_t.shape, 0)
        qry = lax.broadcasted_iota(I32, s_t.shape, 1)
        s_t = jnp.where(key <= qry, s_t, NEG_INF)
        _softmax_update(s_t, _col_max(s_t), vt_ref[:, rows(qi)], m_sc, l_sc, acc_sc, col_shift=c_q(qi))
        gated = (acc_sc[...] / l_sc[...]).T * g_ref[rows(qi), :].astype(F32)
        o_ref[rows(qi), :] = gated.astype(o_ref.dtype)
        _init_softmax(m_sc, l_sc, acc_sc)

    _row_sweep(nq, trips_ref, refresh_q, scores, consume, consume_diag, ((sa_sc, ma_sc), (sb_sc, mb_sc)))


def _fox_attention(qk, v_t, c_tok, c_rows, gate, batch, seq, heads):
    tq = _tile(seq, ATTN_TILE)
    assert tq % LANES == 0 or tq == seq
    dh = FOX_HEAD_DIM
    nq = seq // tq
    return pl.pallas_call(
        functools.partial(_fox_attn_kernel, tq=tq, nq=nq),
        grid_spec=pltpu.PrefetchScalarGridSpec(
            num_scalar_prefetch=1,
            grid=(batch, heads),
            in_specs=[pl.BlockSpec((seq, dh), lambda b, h, tr: (b, h)),
                      pl.BlockSpec((seq, dh), lambda b, h, tr: (b, heads + h)),
                      pl.BlockSpec((dh, seq), lambda b, h, tr: (h, b)),
                      pl.BlockSpec((seq, LANES), lambda b, h, tr: (b, 0)),
                      pl.BlockSpec((1, 1, 1, seq), lambda b, h, tr: (b, h, 0, 0)),
                      pl.BlockSpec((seq, dh), lambda b, h, tr: (b, h))],
            out_specs=pl.BlockSpec((seq, dh), lambda b, h, tr: (b, h)),
            scratch_shapes=[pltpu.VMEM((1, tq), F32), pltpu.VMEM((1, tq), F32), pltpu.VMEM((dh, tq), F32),
                            pltpu.VMEM((seq, dh + LANES), BF16),
                            pltpu.VMEM((tq, tq), F32), pltpu.VMEM((1, tq), F32),
                            pltpu.VMEM((tq, tq), F32), pltpu.VMEM((1, tq), F32),
                            pltpu.VMEM((dh + LANES, tq), BF16)],
        ),
        out_shape=jax.ShapeDtypeStruct((batch * seq, heads * dh), BF16),
        compiler_params=_params("parallel", "parallel"),
        name="fox_attention",
    )(jnp.arange(nq, dtype=I32) // 2, qk, qk, v_t, c_tok, c_rows, gate)


def _fox_qk_kernel(x_ref, w_ref, g_ref, o_ref):
    y = _dot(x_ref[...], w_ref[...])
    g = g_ref[...]
    for hd in range(y.shape[1] // FOX_HEAD_DIM):
        sl = slice(hd * FOX_HEAD_DIM, (hd + 1) * FOX_HEAD_DIM)
        o_ref[:, sl] = _rms(y[:, sl], g[:, sl]).astype(o_ref.dtype)


def _fox_qk(a, w_qk, gain_row):
    t, d = a.shape
    n = w_qk.shape[1]
    tm, tn = _tile(t, ROW_TILE), _tile(n, COL_TILE)
    return pl.pallas_call(
        _fox_qk_kernel,
        grid=(t // tm, n // tn),
        in_specs=[pl.BlockSpec((tm, d), lambda i, j: (i, 0)), pl.BlockSpec((d, tn), lambda i, j: (0, j)),
                  pl.BlockSpec((1, tn), lambda i, j: (0, j))],
        out_specs=pl.BlockSpec((tm, tn), lambda i, j: (i, j)),
        out_shape=jax.ShapeDtypeStruct((t, n), BF16),
        compiler_params=_params("parallel", "parallel"),
        name="fox_qk",
    )(a, w_qk, gain_row)


def _fox_forget_kernel(h_ref, g_ref, w_ref, b_ref, c_ref, ctok_ref, carry_sc, *, tiles_per_seq):
    i = pl.program_id(0)

    @pl.when(i % tiles_per_seq == 0)
    def _():
        carry_sc[...] = jnp.zeros(carry_sc.shape, F32)

    z = _split_dot(_rms(h_ref[...], g_ref[...]), w_ref) + b_ref[...]
    log_f = jnp.minimum(z, 0.0) - jnp.log(1.0 + jnp.exp(-jnp.abs(z)))

    tm = log_f.shape[0]
    row = lax.broadcasted_iota(I32, (tm, tm), 0)
    col = lax.broadcasted_iota(I32, (tm, tm), 1)
    tri = (col <= row).astype(BF16)
    sums = _dot(tri, jnp.concatenate(_split(log_f, 3) + [jnp.zeros_like(log_f, BF16)], axis=1))
    c = (sums[:, :LANES] + (sums[:, LANES:2 * LANES] + sums[:, 2 * LANES:3 * LANES])) + carry_sc[...]
    carry_sc[...] = c[tm - 1:tm, :]
    c_ref[0] = c.T
    ctok_ref[...] = c


def _fox_forget_cumsum(h, gain, w_f, b_f, batch, seq):
    t, d = h.shape
    heads = w_f.shape[1]
    assert heads <= LANES
    tm = _tile(seq, ROUTE_TILE)
    w12 = jnp.concatenate(_split(jnp.pad(w_f, ((0, 0), (0, LANES - heads))), 2), axis=1)
    b = jnp.pad(b_f, (0, LANES - heads)).reshape(1, LANES)
    tps = seq // tm
    return pl.pallas_call(
        functools.partial(_fox_forget_kernel, tiles_per_seq=tps),
        grid=(t // tm,),
        in_specs=[pl.BlockSpec((tm, d), lambda i: (i, 0)), pl.BlockSpec((1, d), lambda i: (0, 0)),
                  pl.BlockSpec((d, 2 * LANES), lambda i: (0, 0)), pl.BlockSpec((1, LANES), lambda i: (0, 0))],
        out_specs=[pl.BlockSpec((1, LANES, tm), lambda i: (i // tps, 0, i % tps)),
                   pl.BlockSpec((tm, LANES), lambda i: (i, 0))],
        out_shape=[jax.ShapeDtypeStruct((batch, LANES, seq), F32), jax.ShapeDtypeStruct((t, LANES), F32)],
        scratch_shapes=[pltpu.VMEM((1, LANES), F32)],
        compiler_params=_params("arbitrary"),
        name="fox_forget_cumsum",
    )(h, gain.reshape(1, d), w12, b)


def _router_kernel(h_ref, g_ref, w_ref, b_ref, t_ref, meta_ref, metat_ref, cnt_ref, carry_sc):
    i = pl.program_id(0)

    @pl.when(i == 0)
    def _():
        carry_sc[...] = jnp.zeros(carry_sc.shape, F32)

    t = _rms(h_ref[...], g_ref[...])
    _to_slabs(t_ref, _pack_halves(t))

    logits = _split_dot(t, w_ref) + b_ref[...]
    tm = logits.shape[0]
    lane = lax.broadcasted_iota(I32, logits.shape, 1)
    lane_f = lane.astype(F32)
    first = lambda hit: jnp.min(jnp.where(hit, lane_f, float(LANES)), axis=1, keepdims=True).astype(I32)

    is_grp = lane < MOE_GROUPS
    gl = jnp.where(is_grp, logits, -jnp.inf)
    gmax = jnp.max(gl, axis=1, keepdims=True)
    g_sel = first(gl == gmax)
    gexp = jnp.where(is_grp, jnp.exp(logits - gmax), 0.0)
    g_w = 1.0 / jnp.sum(gexp, axis=1, keepdims=True)

    lo = MOE_GROUPS + MOE_EXPERTS_PER_GROUP * g_sel
    in_grp = jnp.logical_and(lane >= lo, lane < lo + MOE_EXPERTS_PER_GROUP)
    el = jnp.where(in_grp, logits, -jnp.inf)
    emax = jnp.max(el, axis=1, keepdims=True)
    eexp = jnp.where(in_grp, jnp.exp(logits - emax), 0.0)
    prob = eexp / jnp.sum(eexp, axis=1, keepdims=True)
    cand1 = jnp.where(in_grp, prob, -1.0)
    p1 = jnp.max(cand1, axis=1, keepdims=True)
    j1 = first(cand1 == p1)
    cand2 = jnp.where(lane == j1, -1.0, cand1)
    p2 = jnp.max(cand2, axis=1, keepdims=True)
    j2 = first(cand2 == p2)
    denom = p1 + p2
    wt1 = p1 / denom * g_w
    wt2 = p2 / denom * g_w
    e1 = j1 - MOE_GROUPS
    e2 = j2 - MOE_GROUPS

    hit1 = lane == e1
    hit2 = lane == e2
    row = lax.broadcasted_iota(I32, (tm, tm), 0)
    col = lax.broadcasted_iota(I32, (tm, tm), 1)
    before = (col < row).astype(BF16)
    pre = _dot(before, jnp.concatenate([hit1.astype(BF16), hit2.astype(BF16)], axis=1))
    pre1, pre2 = pre[:, :LANES], pre[:, LANES:]
    carry = carry_sc[...]
    cnt1 = jnp.sum(hit1.astype(F32), axis=0, keepdims=True)
    cnt2 = jnp.sum(hit2.astype(F32), axis=0, keepdims=True)
    rank1 = jnp.sum(jnp.where(hit1, pre1 + carry, 0.0), axis=1, keepdims=True)
    rank2 = jnp.sum(jnp.where(hit2, pre2 + (carry + cnt1), 0.0), axis=1, keepdims=True)
    total = carry + cnt1 + cnt2
    carry_sc[...] = total
    cnt_ref[...] = jnp.broadcast_to(total, cnt_ref.shape)

    bits = lambda x: lax.bitcast_convert_type(jnp.broadcast_to(x, logits.shape), I32)
    meta = jnp.where(lane == 0, e1, 0)
    meta = jnp.where(lane == 1, e2, meta)
    meta = jnp.where(lane == 2, rank1.astype(I32), meta)
    meta = jnp.where(lane == 3, rank2.astype(I32), meta)
    meta = jnp.where(lane == 4, bits(wt1), meta)
    meta = jnp.where(lane == 5, bits(wt2), meta)
    meta_ref[...] = meta
    metat_ref[...] = meta.T[:SUBLANES, :]


def _router(h, gain, w_grp, b_grp, w_rt, b_rt):
    t, d = h.shape
    tm = _tile(t, ROUTE_TILE)
    slab = d // 2 // LANES
    n_used = MOE_GROUPS + MOE_EXPERTS
    w = jnp.pad(jnp.concatenate([w_grp, w_rt], axis=1), ((0, 0), (0, LANES - n_used)))
    w12 = jnp.concatenate(_split(w, 2), axis=1)
    b = jnp.pad(jnp.concatenate([b_grp, b_rt]), (0, LANES - n_used)).reshape(1, LANES)
    return pl.pallas_call(
        _router_kernel,
        grid=(t // tm,),
        in_specs=[pl.BlockSpec((tm, d), lambda i: (i, 0)), pl.BlockSpec((1, d), lambda i: (0, 0)),
                  pl.BlockSpec((d, 2 * LANES), lambda i: (0, 0)), pl.BlockSpec((1, LANES), lambda i: (0, 0))],
        out_specs=[pl.BlockSpec((tm * slab, LANES), lambda i: (i, 0)),
                   pl.BlockSpec((tm, LANES), lambda i: (i, 0)),
                   pl.BlockSpec((SUBLANES, tm), lambda i: (0, i)),
                   pl.BlockSpec((SUBLANES, LANES), lambda i: (0, 0))],
        out_shape=[jax.ShapeDtypeStruct((t * slab, LANES), U32),
                   jax.ShapeDtypeStruct((t, LANES), I32),
                   jax.ShapeDtypeStruct((SUBLANES, t), I32),
                   jax.ShapeDtypeStruct((SUBLANES, LANES), F32)],
        scratch_shapes=[pltpu.VMEM((1, LANES), F32)],
        compiler_params=_params("arbitrary"),
        name="moe_router",
    )(h, gain.reshape(1, d), w12, b)


def _dest_kernel(offs_ref, mt_ref, dest_ref):
    mt = mt_ref[...]
    experts = mt[0:2, :]
    base = jnp.zeros(experts.shape, I32)
    for e in range(MOE_EXPERTS):
        base = jnp.where(experts == e, offs_ref[e], base)
    row = lax.broadcasted_iota(I32, mt.shape, 0)
    dest_ref[...] = jnp.where(row < 2, jnp.concatenate([base + mt[2:4, :], mt[2:SUBLANES, :]], axis=0), 0)


def _dest_rows_all(meta_t, offs):
    rows, t = meta_t.shape
    return pl.pallas_call(
        _dest_kernel,
        grid_spec=pltpu.PrefetchScalarGridSpec(
            num_scalar_prefetch=1,
            grid=(1,),
            in_specs=[pl.BlockSpec((rows, t), lambda i, offs: (0, 0))],
            out_specs=pl.BlockSpec((rows, t), lambda i, offs: (0, 0)),
        ),
        out_shape=jax.ShapeDtypeStruct((rows, t), I32),
        compiler_params=_params("arbitrary"),
        name="moe_dest",
    )(offs, meta_t)


def _slab_copy(src, src_tok, dst, dst_tok, sem, slab):
    rows = lambda tok: pl.ds(pl.multiple_of(tok * slab, slab), slab)
    return pltpu.make_async_copy(src.at[rows(src_tok)], dst.at[rows(dst_tok)], sem)


_ROW_UNROLL = 8


def _dispatch_kernel(zs_ref, nv_ref, dest_ref, t_ref, xs_ref, zero_sc, sem, *, tm, tz, slab):
    i = pl.program_id(0)

    @pl.when(i == 0)
    def _():
        zero_sc[...] = jnp.zeros(zero_sc.shape, zero_sc.dtype)
        fill = lambda tok0: pltpu.make_async_copy(
            zero_sc, xs_ref.at[pl.ds(pl.multiple_of(tok0 * slab, tz * slab), tz * slab)], sem)
        fills = [fill(zs_ref[e]) for e in range(MOE_EXPERTS)]
        for c in fills:
            c.start()
        for c in fills:
            c.wait()

        def tail(tile, carry):
            c = fill(tile * tz)
            c.start()
            c.wait()
            return carry

        lax.fori_loop(nv_ref[0], xs_ref.shape[0] // (tz * slab), tail, 0)

    def issue(blk, carry):
        for u in range(_ROW_UNROLL):
            r = blk * _ROW_UNROLL + u
            _slab_copy(t_ref, r, xs_ref, dest_ref[0, r], sem, slab).start(priority=0)
            _slab_copy(t_ref, r, xs_ref, dest_ref[1, r], sem, slab).start(priority=1)
        return carry

    lax.fori_loop(0, tm // _ROW_UNROLL, issue, 0)

    def drain(blk, carry):
        for u in range(2 * _ROW_UNROLL):
            _slab_copy(t_ref, 0, xs_ref, 0, sem, slab).wait()
        return carry

    lax.fori_loop(0, tm // _ROW_UNROLL, drain, 0)


def _dispatch(t_slabs, dest_t, zero_start, n_valid, n_rows):
    t = dest_t.shape[1]
    slab = t_slabs.shape[0] // t
    tm = _tile(t, MOVE_TILE)
    tz = EXPERT_TILE
    mspec = pl.BlockSpec((SUBLANES, tm), lambda i, *_: (0, i), memory_space=pltpu.SMEM)
    return pl.pallas_call(
        functools.partial(_dispatch_kernel, tm=tm, tz=tz, slab=slab),
        grid_spec=pltpu.PrefetchScalarGridSpec(
            num_scalar_prefetch=2,
            grid=(t // tm,),
            in_specs=[mspec, pl.BlockSpec((tm * slab, LANES), lambda i, *_: (i, 0))],
            out_specs=pl.BlockSpec(memory_space=pl.ANY),
            scratch_shapes=[pltpu.VMEM((tz * slab, LANES), U32), pltpu.SemaphoreType.DMA(())],
        ),
        out_shape=jax.ShapeDtypeStruct((n_rows * slab, LANES), U32),
        compiler_params=_params("arbitrary"),
        name="moe_dispatch",
    )(zero_start, n_valid, dest_t, t_slabs)


def _experts_kernel(te_ref, nv_ref, x_ref, wg_ref, wu_ref, wd_ref, y_ref, wg_sc, wu_sc, wd_sc, *, tm):
    i = pl.program_id(0)
    live = i < nv_ref[0]

    @pl.when(jnp.logical_or(i == 0, te_ref[i] != te_ref[jnp.maximum(i - 1, 0)]))
    def _():
        wg_sc[...] = wg_ref[0, 0].astype(BF16)
        wu_sc[...] = wu_ref[0, 0].astype(BF16)
        wd_sc[...] = wd_ref[0, 0].astype(BF16)

    @pl.when(jnp.logical_not(live))
    def _():
        y_ref[...] = jnp.zeros(y_ref.shape, y_ref.dtype)

    @pl.when(live)
    def _():
        hi, lo = _unpack_halves(_from_slabs(x_ref, tm))
        x = jnp.concatenate([hi, lo], axis=1).astype(BF16)
        g = _dot(x, wg_sc[...])
        u = _dot(x, wu_sc[...])
        hid = (g * jax.nn.sigmoid(g) * u).astype(BF16)
        _to_slabs(y_ref, _pack_halves(_dot(hid, wd_sc[...])))


def _experts(xs, tile_expert, n_valid, w_gate, w_up, w_down, layer, n_tiles):
    tm = EXPERT_TILE
    _, _, d, f = w_gate.shape
    rows = tm * (d // 2 // LANES)
    live = lambda i, nv: jnp.minimum(i, nv[0] - 1)
    return pl.pallas_call(
        functools.partial(_experts_kernel, tm=tm),
        grid_spec=pltpu.PrefetchScalarGridSpec(
            num_scalar_prefetch=2,
            grid=(n_tiles,),
            in_specs=[pl.BlockSpec((rows, LANES), lambda i, te, nv: (live(i, nv), 0)),
                      pl.BlockSpec((1, 1, d, f), lambda i, te, nv: (layer, te[i], 0, 0)),
                      pl.BlockSpec((1, 1, d, f), lambda i, te, nv: (layer, te[i], 0, 0)),
                      pl.BlockSpec((1, 1, f, d), lambda i, te, nv: (layer, te[i], 0, 0))],
            out_specs=pl.BlockSpec((rows, LANES), lambda i, te, nv: (i, 0)),
            scratch_shapes=[pltpu.VMEM((d, f), BF16), pltpu.VMEM((d, f), BF16), pltpu.VMEM((f, d), BF16)],
        ),
        out_shape=jax.ShapeDtypeStruct((n_tiles * rows, LANES), U32),
        compiler_params=_params("arbitrary"),
        name="moe_experts",
    )(tile_expert, n_valid, xs, w_gate, w_up, w_down)


def _combine_kernel(dest_ref, dnext_ref, meta_ref, h_ref, g_ref, ys_ref, *rest, tm, slab, final):
    if final:
        o_ref, y_sc, sem = rest
    else:
        hn_ref, a_ref, y_sc, sem = rest
    i = pl.program_id(0)
    n = pl.num_programs(0)

    def gather(dref, slot):
        def issue(blk, carry):
            for u in range(_ROW_UNROLL):
                r = blk * _ROW_UNROLL + u
                _slab_copy(ys_ref, dref[0, r], y_sc.at[slot, 0], r, sem.at[slot], slab).start(priority=0)
                _slab_copy(ys_ref, dref[1, r], y_sc.at[slot, 1], r, sem.at[slot], slab).start(priority=1)
            return carry

        lax.fori_loop(0, tm // _ROW_UNROLL, issue, 0)

    def finish(slot):
        def drain(blk, carry):
            for u in range(2 * _ROW_UNROLL):
                _slab_copy(ys_ref, 0, y_sc.at[slot, 0], 0, sem.at[slot], slab).wait()
            return carry

        lax.fori_loop(0, tm // _ROW_UNROLL, drain, 0)

        meta = meta_ref[...]
        lane = lax.broadcasted_iota(I32, meta.shape, 1)
        wbits = lax.bitcast_convert_type(meta, F32)
        wt1 = jnp.sum(jnp.where(lane == 4, wbits, 0.0), axis=1, keepdims=True)
        wt2 = jnp.sum(jnp.where(lane == 5, wbits, 0.0), axis=1, keepdims=True)
        hi1, lo1 = _unpack_halves(_from_slabs(y_sc.at[slot, 0], tm))
        hi2, lo2 = _unpack_halves(_from_slabs(y_sc.at[slot, 1], tm))
        moe = jnp.concatenate([wt1 * hi1 + wt2 * hi2, wt1 * lo1 + wt2 * lo2], axis=1)
        h_new = h_ref[...] + moe
        if final:
            o_ref[...] = _rms(h_new, g_ref[...])
        else:
            hn_ref[...] = h_new
            a_ref[...] = _rms(h_new, g_ref[...]).astype(a_ref.dtype)

    @pl.when(i == 0)
    def _():
        gather(dest_ref, 0)

    for slot in (0, 1):
        @pl.when((i & 1) == slot)
        def _():
            @pl.when(i + 1 < n)
            def _():
                gather(dnext_ref, 1 - slot)

            finish(slot)


def _combine(ys, dest_t, meta, h, gain, final):
    t, d = h.shape
    tm = _tile(t, MOVE_TILE)
    slab = d // 2 // LANES
    n = t // tm
    mspec = pl.BlockSpec((SUBLANES, tm), lambda i: (0, i), memory_space=pltpu.SMEM)
    mnext = pl.BlockSpec((SUBLANES, tm), lambda i: (0, jnp.minimum(i + 1, n - 1)), memory_space=pltpu.SMEM)
    row = pl.BlockSpec((tm, d), lambda i: (i, 0))
    if final:
        out_specs, out_shape = row, jax.ShapeDtypeStruct((t, d), F32)
    else:
        out_specs = [row, row]
        out_shape = [jax.ShapeDtypeStruct((t, d), F32), jax.ShapeDtypeStruct((t, d), BF16)]
    return pl.pallas_call(
        functools.partial(_combine_kernel, tm=tm, slab=slab, final=final),
        grid=(n,),
        in_specs=[mspec, mnext, pl.BlockSpec((tm, LANES), lambda i: (i, 0)), row,
                  pl.BlockSpec((1, d), lambda i: (0, 0)), pl.BlockSpec(memory_space=pl.ANY)],
        out_specs=out_specs,
        out_shape=out_shape,
        scratch_shapes=[pltpu.VMEM((2, 2, tm * slab, LANES), U32), pltpu.SemaphoreType.DMA((2,))],
        compiler_params=_params("arbitrary"),
        name="moe_combine",
    )(dest_t, dest_t, meta, h, gain.reshape(1, d), ys)


def _hier_moe(h, ffn_gain, w_grp, b_grp, w_rt, b_rt, w_gate, w_up, w_down, layer, next_gain, final):
    t, d = h.shape
    tmx = EXPERT_TILE
    t_slabs, meta, meta_t, cnt = _router(h, ffn_gain, w_grp, b_grp, w_rt, b_rt)

    counts = cnt[0, :MOE_EXPERTS].astype(I32)
    padded = (counts + tmx - 1) // tmx * tmx
    ends = jnp.cumsum(padded)
    offs = ends - padded
    n_tiles = (2 * t) // tmx + MOE_EXPERTS
    n_valid = (ends[-1] // tmx).reshape(1)
    tile_start = jnp.minimum(jnp.arange(n_tiles, dtype=I32) * tmx, ends[-1] - 1)
    tile_expert = jnp.sum((ends[None, :] <= tile_start[:, None]).astype(I32), axis=1)
    zero_start = offs + counts // tmx * tmx

    dest_t = _dest_rows_all(meta_t, offs)
    xs = _dispatch(t_slabs, dest_t, zero_start, n_valid, (n_tiles + 1) * tmx)
    ys = _experts(xs, tile_expert, n_valid, w_gate, w_up, w_down, layer, n_tiles)
    return _combine(ys, dest_t, meta, h, next_gain, final)


def kernel(x, positions, attn_norm, ffn_norm, final_norm, mla_w_dq, mla_q_norm, mla_w_uq, mla_w_dkv,
           mla_kv_norm, mla_w_ukv, mla_w_o, fox_w_qkv, fox_q_norm, fox_k_norm, fox_w_f, fox_b_f,
           fox_w_og, fox_w_o, moe_w_grp, moe_b_grp, moe_w_rt, moe_b_rt, moe_w_gate, moe_w_up, moe_w_down):
    batch, seq, d = x.shape
    depth = attn_norm.shape[0]
    t = batch * seq
    cos_t, sin_t = _rope_tables(positions)
    h = x.reshape(t, d)
    a = _norm(h, attn_norm[0], BF16)
    out = None
    for i in range(depth):
        j = i // 2
        if i % 2 == 0:
            heads = mla_w_uq.shape[2] // (MLA_NOPE + MLA_ROPE)
            q, k, v_t = _mla_project(a, cos_t, sin_t, mla_w_dq[j], mla_q_norm[j], mla_w_uq[j],
                                     mla_w_dkv[j], mla_kv_norm[j], mla_w_ukv[j])
            o = _mla_attention(q, k, v_t, batch, seq, heads)
            w_o = mla_w_o[j]
        else:
            dh = FOX_HEAD_DIM
            heads = fox_w_qkv.shape[2] // (3 * dh)
            w_qkv = fox_w_qkv[j].astype(BF16)
            gain_row = jnp.concatenate([jnp.tile(fox_q_norm[j] * (dh ** -0.5 * LOG2_E), heads),
                                        jnp.tile(fox_k_norm[j], heads)])[None, :]
            qk = _fox_qk(a, w_qkv[:, :2 * heads * dh], gain_row)
            v_t = _mm_nt(w_qkv[:, 2 * heads * dh:].T, a, BF16, name="fox_v")
            gate = _mm(a, fox_w_og[j].astype(BF16), BF16, act="sigmoid", name="fox_gate")
            c_t, c_tok = _fox_forget_cumsum(h, attn_norm[i], fox_w_f[j], fox_b_f[j], batch, seq)
            c_rows = c_t[:, :heads, :].reshape(batch, heads, 1, seq)
            o = _fox_attention(qk, v_t, c_tok, c_rows, gate, batch, seq, heads)
            w_o = fox_w_o[j]
        h = _mm_residual(o, w_o.astype(BF16), h)
        final = i == depth - 1
        next_gain = final_norm if final else attn_norm[i + 1]
        res = _hier_moe(h, ffn_norm[i], moe_w_grp[i], moe_b_grp[i], moe_w_rt[i], moe_b_rt[i],
                        moe_w_gate, moe_w_up, moe_w_down, i, next_gain, final)
        if final:
            out = res
        else:
            h, a = res
    return out.reshape(batch, seq, d)
```

```python
import functools

import jax
import jax.numpy as jnp
from jax import lax
from jax.experimental import pallas as pl
from jax.experimental.pallas import tpu as pltpu

F32 = jnp.float32
BF16 = jnp.bfloat16
I32 = jnp.int32
U32 = jnp.uint32

RMS_EPS = 1e-6
NEG_INF = -1e30
CHUNK = 64
MLA_NOPE = 128
MLA_ROPE = 64
MLA_V = 128
MLA_QK_PAD = 256
ROPE_THETA = 10000.0
LOG2_E = 1.4426950408889634
FOX_HEAD_DIM = 128
MOE_GROUPS = 8
MOE_EXPERTS_PER_GROUP = 4
MOE_EXPERTS = MOE_GROUPS * MOE_EXPERTS_PER_GROUP

LANES = 128
SUBLANES = 8
V7X_VMEM_LIMIT_BYTES = 56 * 1024 * 1024

ROW_TILE = 1024
COL_TILE = 1024
ATTN_TILE = 512
ROUTE_TILE = 512
MOVE_TILE = 512
COMBINE_TILE = 256
EXPERT_TILE = 256


def _params(*sem):
    return pltpu.CompilerParams(dimension_semantics=sem, vmem_limit_bytes=V7X_VMEM_LIMIT_BYTES)


def _tile(n, t):
    if n <= t:
        return n
    step = LANES if t % LANES == 0 else SUBLANES
    for c in range(t - t % step, 0, -step):
        if n % c == 0:
            return c
    raise ValueError(f"no aligned tile for {n} under {t}")


def _rms(x, gain):
    ms = jnp.mean(x * x, axis=-1, keepdims=True)
    return x * lax.rsqrt(ms + RMS_EPS) * gain


def _dot(a, b):
    return jnp.dot(a, b, preferred_element_type=F32)


def _nt_dot(a, b):
    return lax.dot_general(a, b, (((1,), (1,)), ((), ())), preferred_element_type=F32)


def _pack_halves(x):
    n = x.shape[1] // 2
    hi = lax.bitcast_convert_type(x[:, :n].astype(BF16).astype(F32), U32)
    lo = lax.bitcast_convert_type(x[:, n:].astype(BF16).astype(F32), U32)
    return hi | lax.shift_right_logical(lo, jnp.uint32(16))


def _unpack_halves(p):
    hi = lax.bitcast_convert_type(p & jnp.uint32(0xFFFF0000), F32)
    lo = lax.bitcast_convert_type(lax.shift_left(p, jnp.uint32(16)), F32)
    return hi, lo


def _to_slabs(ref, x):
    m, width = x.shape
    c = width // LANES
    for j in range(c):
        ref[pl.ds(j, m, stride=c), :] = x[:, j * LANES:(j + 1) * LANES]


def _from_slabs(ref, m):
    c = ref.shape[0] // m
    return jnp.concatenate([ref[pl.ds(j, m, stride=c), :] for j in range(c)], axis=1)


def _split(x, terms):
    out = []
    for _ in range(terms):
        hi = x.astype(BF16)
        out.append(hi)
        x = x - hi.astype(F32)
    return out


def _split_dot(x, w12_ref):
    m = x.shape[0]
    n = w12_ref.shape[1] // 2
    prod = _dot(jnp.concatenate(_split(x, 2), axis=0), w12_ref[...])
    return (prod[:m, :n] + (prod[:m, n:] + prod[m:, :n])) + prod[m:, n:]


def _rope_lanes(seg, cos_t, sin_t):
    half = MLA_ROPE // 2
    lane = lax.broadcasted_iota(I32, seg.shape, 1)
    swapped = jnp.where(lane < half, pltpu.roll(seg, LANES - half, 1), pltpu.roll(seg, half, 1))
    return seg * cos_t + swapped * sin_t


def _rope_table_kernel(pos_ref, freq_ref, cos_ref, sin_ref):
    ang = pos_ref[...] * freq_ref[...]
    lane = lax.broadcasted_iota(I32, ang.shape, 1)
    half = MLA_ROPE // 2
    valid = lane < MLA_ROPE
    cos_ref[...] = jnp.where(valid, jnp.cos(ang), 0.0)
    sin_ref[...] = jnp.where(valid, jnp.where(lane < half, -jnp.sin(ang), jnp.sin(ang)), 0.0)


def _rope_tables(positions):
    t = positions.size
    tm = _tile(t, ROW_TILE)
    half = MLA_ROPE // 2
    inv_freq = ROPE_THETA ** (-jnp.arange(0, MLA_ROPE, 2, dtype=F32) / MLA_ROPE)
    freq_row = jnp.concatenate([inv_freq, inv_freq, jnp.zeros((LANES - 2 * half,), F32)])[None, :]
    pos = positions.reshape(t, 1).astype(F32)
    out = jax.ShapeDtypeStruct((t, LANES), F32)
    return pl.pallas_call(
        _rope_table_kernel,
        grid=(t // tm,),
        in_specs=[pl.BlockSpec((tm, 1), lambda i: (i, 0)), pl.BlockSpec((1, LANES), lambda i: (0, 0))],
        out_specs=[pl.BlockSpec((tm, LANES), lambda i: (i, 0))] * 2,
        out_shape=[out, out],
        compiler_params=_params("parallel"),
        name="rope_tables",
    )(pos, freq_row)


def _norm_kernel(h_ref, g_ref, a_ref):
    a_ref[...] = _rms(h_ref[...], g_ref[...]).astype(a_ref.dtype)


def _norm(h, gain, out_dtype):
    t, d = h.shape
    tm = _tile(t, ROUTE_TILE)
    return pl.pallas_call(
        _norm_kernel,
        grid=(t // tm,),
        in_specs=[pl.BlockSpec((tm, d), lambda i: (i, 0)), pl.BlockSpec((1, d), lambda i: (0, 0))],
        out_specs=pl.BlockSpec((tm, d), lambda i: (i, 0)),
        out_shape=jax.ShapeDtypeStruct((t, d), out_dtype),
        compiler_params=_params("parallel"),
        name="rmsnorm",
    )(h, gain.reshape(1, d))


def _mm_kernel(x_ref, w_ref, o_ref, *, act):
    y = _dot(x_ref[...], w_ref[...])
    if act == "sigmoid":
        y = jax.nn.sigmoid(y)
    o_ref[...] = y.astype(o_ref.dtype)


def _mm(x, w, out_dtype, act=None, name="mm", col_tile=COL_TILE):
    m, k = x.shape
    n = w.shape[1]
    tm, tn = _tile(m, ROW_TILE), _tile(n, col_tile)
    return pl.pallas_call(
        functools.partial(_mm_kernel, act=act),
        grid=(m // tm, n // tn),
        in_specs=[pl.BlockSpec((tm, k), lambda i, j: (i, 0)), pl.BlockSpec((k, tn), lambda i, j: (0, j))],
        out_specs=pl.BlockSpec((tm, tn), lambda i, j: (i, j)),
        out_shape=jax.ShapeDtypeStruct((m, n), out_dtype),
        compiler_params=_params("parallel", "parallel"),
        name=name,
    )(x, w)


def _mm_nt_kernel(wt_ref, x_ref, o_ref):
    o_ref[...] = _nt_dot(wt_ref[...], x_ref[...]).astype(o_ref.dtype)


def _mm_nt(w_t, x, out_dtype, name):
    n, k = w_t.shape
    m = x.shape[0]
    tm, tn = _tile(m, ROW_TILE), _tile(n, COL_TILE)
    return pl.pallas_call(
        _mm_nt_kernel,
        grid=(m // tm, n // tn),
        in_specs=[pl.BlockSpec((tn, k), lambda i, j: (j, 0)), pl.BlockSpec((tm, k), lambda i, j: (i, 0))],
        out_specs=pl.BlockSpec((tn, tm), lambda i, j: (j, i)),
        out_shape=jax.ShapeDtypeStruct((n, m), out_dtype),
        compiler_params=_params("parallel", "parallel"),
        name=name,
    )(w_t, x)


def _mm_res_kernel(x_ref, w_ref, h_ref, o_ref):
    o_ref[...] = h_ref[...] + _dot(x_ref[...], w_ref[...])


def _mm_residual(x, w, h):
    m, k = x.shape
    n = w.shape[1]
    tm, tn = _tile(m, ROW_TILE), _tile(n, COL_TILE)
    return pl.pallas_call(
        _mm_res_kernel,
        grid=(m // tm, n // tn),
        in_specs=[pl.BlockSpec((tm, k), lambda i, j: (i, 0)), pl.BlockSpec((k, tn), lambda i, j: (0, j)),
                  pl.BlockSpec((tm, tn), lambda i, j: (i, j))],
        out_specs=pl.BlockSpec((tm, tn), lambda i, j: (i, j)),
        out_shape=jax.ShapeDtypeStruct((m, n), F32),
        compiler_params=_params("parallel", "parallel"),
        name="out_proj_residual",
    )(x, w, h)


def _mla_q_kernel(c_ref, g_ref, w_ref, cos_ref, sin_ref, q_ref, *, scale):
    cq = _rms(c_ref[...], g_ref[...]).astype(BF16)
    y = _dot(cq, w_ref[...]) * scale
    cos_t, sin_t = cos_ref[...], sin_ref[...]
    for hd in range(y.shape[1] // MLA_QK_PAD):
        base = hd * MLA_QK_PAD
        q_ref[:, base:base + MLA_NOPE] = y[:, base:base + MLA_NOPE].astype(q_ref.dtype)
        roped = _rope_lanes(y[:, base + MLA_NOPE:base + MLA_QK_PAD], cos_t, sin_t)
        q_ref[:, base + MLA_NOPE:base + MLA_QK_PAD] = roped.astype(q_ref.dtype)


def _mla_kv_kernel(c_ref, pe_ref, g_ref, wk_ref, wvt_ref, cos_ref, sin_ref, k_ref, vt_ref):
    ckv = _rms(c_ref[...], g_ref[...]).astype(BF16)
    kn = _dot(ckv, wk_ref[...])
    vt_ref[...] = _nt_dot(wvt_ref[...], ckv).astype(vt_ref.dtype)
    k_pe = _rope_lanes(pe_ref[...], cos_ref[...], sin_ref[...]).astype(k_ref.dtype)
    for hd in range(kn.shape[1] // MLA_NOPE):
        k_ref[:, hd * MLA_QK_PAD:hd * MLA_QK_PAD + MLA_NOPE] = (
            kn[:, hd * MLA_NOPE:(hd + 1) * MLA_NOPE].astype(k_ref.dtype))
        k_ref[:, hd * MLA_QK_PAD + MLA_NOPE:(hd + 1) * MLA_QK_PAD] = k_pe


def _mla_project(a, cos_t, sin_t, w_dq, q_norm, w_uq, w_dkv, kv_norm, w_ukv):
    t, d = a.shape
    q_lora = w_dq.shape[1]
    kv_lora = w_dkv.shape[1] - MLA_ROPE
    heads = w_uq.shape[1] // (MLA_NOPE + MLA_ROPE)
    assert q_lora % LANES == 0 and kv_lora % LANES == 0

    w_down = jnp.concatenate(
        [w_dq, w_dkv, jnp.zeros((d, LANES - MLA_ROPE), w_dkv.dtype)], axis=1).astype(BF16)
    s1 = _mm(a, w_down, F32, name="mla_down", col_tile=w_down.shape[1])

    w_q = w_uq.reshape(q_lora, heads, MLA_NOPE + MLA_ROPE)
    w_q = jnp.pad(w_q, ((0, 0), (0, 0), (0, MLA_QK_PAD - MLA_NOPE - MLA_ROPE)))
    w_q = w_q.reshape(q_lora, heads * MLA_QK_PAD).astype(BF16)
    w_kv = w_ukv.reshape(kv_lora, heads, MLA_NOPE + MLA_V)
    w_k = w_kv[:, :, :MLA_NOPE].reshape(kv_lora, heads * MLA_NOPE).astype(BF16)
    w_vt = w_kv[:, :, MLA_NOPE:].reshape(kv_lora, heads * MLA_V).T.astype(BF16)

    tm = _tile(t, ROW_TILE)
    scale = (MLA_NOPE + MLA_ROPE) ** -0.5 * LOG2_E
    tn = heads * MLA_QK_PAD
    row128 = pl.BlockSpec((tm, LANES), lambda i, j: (i, 0))
    q = pl.pallas_call(
        functools.partial(_mla_q_kernel, scale=scale),
        grid=(t // tm, heads * MLA_QK_PAD // tn),
        in_specs=[pl.BlockSpec((tm, q_lora), lambda i, j: (i, 0)),
                  pl.BlockSpec((1, q_lora), lambda i, j: (0, 0)),
                  pl.BlockSpec((q_lora, tn), lambda i, j: (0, j)), row128, row128],
        out_specs=pl.BlockSpec((tm, tn), lambda i, j: (i, j)),
        out_shape=jax.ShapeDtypeStruct((t, heads * MLA_QK_PAD), BF16),
        compiler_params=_params("parallel", "parallel"),
        name="mla_q",
    )(s1, q_norm.reshape(1, q_lora), w_q, cos_t, sin_t)

    hb = heads
    kv_blk = q_lora // kv_lora
    assert q_lora % kv_lora == 0
    pe_blk = (q_lora + kv_lora) // LANES
    k, v_t = pl.pallas_call(
        _mla_kv_kernel,
        grid=(t // tm, heads // hb),
        in_specs=[pl.BlockSpec((tm, kv_lora), lambda i, j: (i, kv_blk)),
                  pl.BlockSpec((tm, LANES), lambda i, j: (i, pe_blk)),
                  pl.BlockSpec((1, kv_lora), lambda i, j: (0, 0)),
                  pl.BlockSpec((kv_lora, hb * MLA_NOPE), lambda i, j: (0, j)),
                  pl.BlockSpec((hb * MLA_V, kv_lora), lambda i, j: (j, 0)), row128, row128],
        out_specs=[pl.BlockSpec((tm, hb * MLA_QK_PAD), lambda i, j: (i, j)),
                   pl.BlockSpec((hb * MLA_V, tm), lambda i, j: (j, i))],
        out_shape=[jax.ShapeDtypeStruct((t, heads * MLA_QK_PAD), BF16),
                   jax.ShapeDtypeStruct((heads * MLA_V, t), BF16)],
        compiler_params=_params("parallel", "parallel"),
        name="mla_kv",
    )(s1, s1, kv_norm.reshape(1, kv_lora), w_k, w_vt, cos_t, sin_t)
    return q, k, v_t


def _col_max(x):
    while x.shape[0] > SUBLANES and x.shape[0] % (2 * SUBLANES) == 0:
        half = x.shape[0] // 2
        x = jnp.maximum(x[:half], x[half:])
    return jnp.max(x, axis=0, keepdims=True)


def _softmax_update(s_t, m_cur, v_t, m_sc, l_sc, acc_sc, col_shift=None):
    m_prev = m_sc[...]
    if col_shift is not None:
        m_cur = m_cur + col_shift
    m_new = jnp.maximum(m_prev, m_cur)
    shift = m_new if col_shift is None else m_new - col_shift
    p_t = jnp.exp2(s_t - shift)
    alpha = jnp.exp2(m_prev - m_new)
    l_sc[...] = alpha * l_sc[...] + jnp.sum(p_t, axis=0, keepdims=True)
    acc_sc[...] = alpha * acc_sc[...] + _dot(v_t, p_t.astype(v_t.dtype))
    m_sc[...] = m_new


def _init_softmax(m_sc, l_sc, acc_sc):
    m_sc[...] = jnp.full(m_sc.shape, NEG_INF, F32)
    l_sc[...] = jnp.zeros(l_sc.shape, F32)
    acc_sc[...] = jnp.zeros(acc_sc.shape, F32)


def _row_sweep(nq, trips_ref, refresh_q, scores, consume, consume_diag, bufs):
    def produce(kt, s_buf, m_buf):
        s_t = scores(kt)
        s_buf[...] = s_t
        m_buf[...] = _col_max(s_t)

    refresh_q(0)
    produce(0, *bufs[0])
    first = 0
    for qi in range(nq):
        cur, other = bufs[first], bufs[1 - first]

        def pair(p, carry, qi=qi, cur=cur, other=other):
            produce(2 * p + 1, *other)
            consume(qi, 2 * p, *cur)
            produce(2 * p + 2, *cur)
            consume(qi, 2 * p + 1, *other)
            return carry

        lax.fori_loop(0, trips_ref[qi], pair, 0)
        if qi % 2 == 1:
            produce(qi, *other)
            consume(qi, qi - 1, *cur)
            cur, other, first = other, cur, 1 - first
        if qi + 1 < nq:
            refresh_q(qi + 1)
            produce(0, *other)
        consume_diag(qi, *cur)
        first = 1 - first


def _mla_attn_kernel(trips_ref, q_ref, k_ref, vt_ref, o_ref, m_sc, l_sc, acc_sc, sa_sc, ma_sc, sb_sc, mb_sc,
                     qt_sc, *, tq, nq, chunk_shift):
    _init_softmax(m_sc, l_sc, acc_sc)
    rows = lambda i: pl.ds(pl.multiple_of(i * tq, tq), tq)

    def refresh_q(qi):
        qt_sc[...] = q_ref[rows(qi), :].T

    def scores(kt):
        return _dot(k_ref[rows(kt), :], qt_sc[...])

    def consume(qi, kt, s_buf, m_buf):
        _softmax_update(s_buf[...], m_buf[...], vt_ref[:, rows(kt)], m_sc, l_sc, acc_sc)

    def consume_diag(qi, s_buf, m_buf):
        s_t = s_buf[...]
        key = lax.broadcasted_iota(I32, s_t.shape, 0)
        qry = lax.broadcasted_iota(I32, s_t.shape, 1)
        allowed = lax.shift_right_logical(key, chunk_shift) <= lax.shift_right_logical(qry, chunk_shift)
        s_t = jnp.where(allowed, s_t, NEG_INF)
        _softmax_update(s_t, _col_max(s_t), vt_ref[:, rows(qi)], m_sc, l_sc, acc_sc)
        o_ref[rows(qi), :] = (acc_sc[...] / l_sc[...]).T.astype(o_ref.dtype)
        _init_softmax(m_sc, l_sc, acc_sc)

    _row_sweep(nq, trips_ref, refresh_q, scores, consume, consume_diag, ((sa_sc, ma_sc), (sb_sc, mb_sc)))


def _mla_attention(q, k, v_t, batch, seq, heads):
    tq = _tile(seq, ATTN_TILE)
    assert tq % CHUNK == 0 and CHUNK & (CHUNK - 1) == 0
    nq = seq // tq
    return pl.pallas_call(
        functools.partial(_mla_attn_kernel, tq=tq, nq=nq, chunk_shift=CHUNK.bit_length() - 1),
        grid_spec=pltpu.PrefetchScalarGridSpec(
            num_scalar_prefetch=1,
            grid=(batch, heads),
            in_specs=[pl.BlockSpec((seq, MLA_QK_PAD), lambda b, h, tr: (b, h)),
                      pl.BlockSpec((seq, MLA_QK_PAD), lambda b, h, tr: (b, h)),
                      pl.BlockSpec((MLA_V, seq), lambda b, h, tr: (h, b))],
            out_specs=pl.BlockSpec((seq, MLA_V), lambda b, h, tr: (b, h)),
            scratch_shapes=[pltpu.VMEM((1, tq), F32), pltpu.VMEM((1, tq), F32), pltpu.VMEM((MLA_V, tq), F32),
                            pltpu.VMEM((tq, tq), F32), pltpu.VMEM((1, tq), F32),
                            pltpu.VMEM((tq, tq), F32), pltpu.VMEM((1, tq), F32),
                            pltpu.VMEM((MLA_QK_PAD, tq), BF16)],
        ),
        out_shape=jax.ShapeDtypeStruct((batch * seq, heads * MLA_V), BF16),
        compiler_params=_params("parallel", "parallel"),
        name="mla_attention",
    )(jnp.arange(nq, dtype=I32) // 2, q, k, v_t)


_FORGET_TERMS = 3


def _fox_attn_kernel(trips_ref, q_ref, k_ref, vt_ref, ctok_ref, crow_ref, g_ref, o_ref, m_sc, l_sc, acc_sc,
                     ka_sc, sa_sc, ma_sc, sb_sc, mb_sc, qt_sc, *, tq, nq):
    head = pl.program_id(1)
    dh = k_ref.shape[1]
    _init_softmax(m_sc, l_sc, acc_sc)
    rows = lambda i: pl.ds(pl.multiple_of(i * tq, tq), tq)

    row = lax.broadcasted_iota(I32, (LANES, tq), 0)
    qt_sc[dh:, :] = jnp.where(row < _FORGET_TERMS, -1.0, 0.0).astype(qt_sc.dtype)

    ka_sc[:, :dh] = k_ref[...]
    terms = _split(ctok_ref[...] * LOG2_E, _FORGET_TERMS)
    src = lax.broadcasted_iota(I32, (LANES, LANES), 0)
    dst = lax.broadcasted_iota(I32, (LANES, LANES), 1)
    aug = None
    for j, term in enumerate(terms):
        pick = jnp.logical_and(src == head, dst == j).astype(BF16)
        part = _dot(term, pick)
        aug = part if aug is None else aug + part
    ka_sc[:, dh:] = aug.astype(ka_sc.dtype)

    def refresh_q(qi):
        qt_sc[:dh, :] = q_ref[rows(qi), :].T

    def scores(kt):
        return _dot(ka_sc[rows(kt), :], qt_sc[...])

    def c_q(qi):
        return crow_ref[0, 0, :, rows(qi)] * LOG2_E

    def consume(qi, kt, s_buf, m_buf):
        _softmax_update(s_buf[...], m_buf[...], vt_ref[:, rows(kt)], m_sc, l_sc, acc_sc, col_shift=c_q(qi))

    def consume_diag(qi, s_buf, m_buf):
        s_t = s_buf[...]
        key = lax.broadcasted_iota(I32, s_t.shape, 0)
        qry = lax.broadcasted_iota(I32, s_t.shape, 1)
        s_t = jnp.where(key <= qry, s_t, NEG_INF)
        _softmax_update(s_t, _col_max(s_t), vt_ref[:, rows(qi)], m_sc, l_sc, acc_sc, col_shift=c_q(qi))
        gated = (acc_sc[...] / l_sc[...]).T * g_ref[rows(qi), :].astype(F32)
        o_ref[rows(qi), :] = gated.astype(o_ref.dtype)
        _init_softmax(m_sc, l_sc, acc_sc)

    _row_sweep(nq, trips_ref, refresh_q, scores, consume, consume_diag, ((sa_sc, ma_sc), (sb_sc, mb_sc)))


def _fox_attention(qk, v_t, c_tok, c_rows, gate, batch, seq, heads):
    tq = _tile(seq, ATTN_TILE)
    assert tq % LANES == 0 or tq == seq
    dh = FOX_HEAD_DIM
    nq = seq // tq
    return pl.pallas_call(
        functools.partial(_fox_attn_kernel, tq=tq, nq=nq),
        grid_spec=pltpu.PrefetchScalarGridSpec(
            num_scalar_prefetch=1,
            grid=(batch, heads),
            in_specs=[pl.BlockSpec((seq, dh), lambda b, h, tr: (b, h)),
                      pl.BlockSpec((seq, dh), lambda b, h, tr: (b, heads + h)),
                      pl.BlockSpec((dh, seq), lambda b, h, tr: (h, b)),
                      pl.BlockSpec((seq, LANES), lambda b, h, tr: (b, 0)),
                      pl.BlockSpec((1, 1, 1, seq), lambda b, h, tr: (b, h, 0, 0)),
                      pl.BlockSpec((seq, dh), lambda b, h, tr: (b, h))],
            out_specs=pl.BlockSpec((seq, dh), lambda b, h, tr: (b, h)),
            scratch_shapes=[pltpu.VMEM((1, tq), F32), pltpu.VMEM((1, tq), F32), pltpu.VMEM((dh, tq), F32),
                            pltpu.VMEM((seq, dh + LANES), BF16),
                            pltpu.VMEM((tq, tq), F32), pltpu.VMEM((1, tq), F32),
                            pltpu.VMEM((tq, tq), F32), pltpu.VMEM((1, tq), F32),
                            pltpu.VMEM((dh + LANES, tq), BF16)],
        ),
        out_shape=jax.ShapeDtypeStruct((batch * seq, heads * dh), BF16),
        compiler_params=_params("parallel", "parallel"),
        name="fox_attention",
    )(jnp.arange(nq, dtype=I32) // 2, qk, qk, v_t, c_tok, c_rows, gate)


def _fox_qk_kernel(x_ref, w_ref, g_ref, o_ref):
    y = _dot(x_ref[...], w_ref[...])
    g = g_ref[...]
    for hd in range(y.shape[1] // FOX_HEAD_DIM):
        sl = slice(hd * FOX_HEAD_DIM, (hd + 1) * FOX_HEAD_DIM)
        o_ref[:, sl] = _rms(y[:, sl], g[:, sl]).astype(o_ref.dtype)


def _fox_qk(a, w_qk, gain_row):
    t, d = a.shape
    n = w_qk.shape[1]
    tm, tn = _tile(t, ROW_TILE), _tile(n, COL_TILE)
    return pl.pallas_call(
        _fox_qk_kernel,
        grid=(t // tm, n // tn),
        in_specs=[pl.BlockSpec((tm, d), lambda i, j: (i, 0)), pl.BlockSpec((d, tn), lambda i, j: (0, j)),
                  pl.BlockSpec((1, tn), lambda i, j: (0, j))],
        out_specs=pl.BlockSpec((tm, tn), lambda i, j: (i, j)),
        out_shape=jax.ShapeDtypeStruct((t, n), BF16),
        compiler_params=_params("parallel", "parallel"),
        name="fox_qk",
    )(a, w_qk, gain_row)


def _fox_forget_kernel(h_ref, g_ref, w_ref, b_ref, c_ref, ctok_ref, carry_sc, *, tiles_per_seq):
    i = pl.program_id(0)

    @pl.when(i % tiles_per_seq == 0)
    def _():
        carry_sc[...] = jnp.zeros(carry_sc.shape, F32)

    z = _split_dot(_rms(h_ref[...], g_ref[...]), w_ref) + b_ref[...]
    log_f = jnp.minimum(z, 0.0) - jnp.log(1.0 + jnp.exp(-jnp.abs(z)))

    tm = log_f.shape[0]
    row = lax.broadcasted_iota(I32, (tm, tm), 0)
    col = lax.broadcasted_iota(I32, (tm, tm), 1)
    tri = (col <= row).astype(BF16)
    sums = _dot(tri, jnp.concatenate(_split(log_f, 3) + [jnp.zeros_like(log_f, BF16)], axis=1))
    c = (sums[:, :LANES] + (sums[:, LANES:2 * LANES] + sums[:, 2 * LANES:3 * LANES])) + carry_sc[...]
    carry_sc[...] = c[tm - 1:tm, :]
    c_ref[0] = c.T
    ctok_ref[...] = c


def _fox_forget_cumsum(h, gain, w_f, b_f, batch, seq):
    t, d = h.shape
    heads = w_f.shape[1]
    assert heads <= LANES
    tm = _tile(seq, ROUTE_TILE)
    w12 = jnp.concatenate(_split(jnp.pad(w_f, ((0, 0), (0, LANES - heads))), 2), axis=1)
    b = jnp.pad(b_f, (0, LANES - heads)).reshape(1, LANES)
    tps = seq // tm
    return pl.pallas_call(
        functools.partial(_fox_forget_kernel, tiles_per_seq=tps),
        grid=(t // tm,),
        in_specs=[pl.BlockSpec((tm, d), lambda i: (i, 0)), pl.BlockSpec((1, d), lambda i: (0, 0)),
                  pl.BlockSpec((d, 2 * LANES), lambda i: (0, 0)), pl.BlockSpec((1, LANES), lambda i: (0, 0))],
        out_specs=[pl.BlockSpec((1, LANES, tm), lambda i: (i // tps, 0, i % tps)),
                   pl.BlockSpec((tm, LANES), lambda i: (i, 0))],
        out_shape=[jax.ShapeDtypeStruct((batch, LANES, seq), F32), jax.ShapeDtypeStruct((t, LANES), F32)],
        scratch_shapes=[pltpu.VMEM((1, LANES), F32)],
        compiler_params=_params("arbitrary"),
        name="fox_forget_cumsum",
    )(h, gain.reshape(1, d), w12, b)


def _router_kernel(h_ref, g_ref, w_ref, b_ref, t_ref, meta_ref, metat_ref, cnt_ref, carry_sc):
    i = pl.program_id(0)

    @pl.when(i == 0)
    def _():
        carry_sc[...] = jnp.zeros(carry_sc.shape, F32)

    t = _rms(h_ref[...], g_ref[...])
    _to_slabs(t_ref, _pack_halves(t))

    logits = _split_dot(t, w_ref) + b_ref[...]
    tm = logits.shape[0]
    lane = lax.broadcasted_iota(I32, logits.shape, 1)
    lane_f = lane.astype(F32)
    first = lambda hit: jnp.min(jnp.where(hit, lane_f, float(LANES)), axis=1, keepdims=True).astype(I32)

    is_grp = lane < MOE_GROUPS
    gl = jnp.where(is_grp, logits, -jnp.inf)
    gmax = jnp.max(gl, axis=1, keepdims=True)
    g_sel = first(gl == gmax)
    gexp = jnp.where(is_grp, jnp.exp(logits - gmax), 0.0)
    g_w = 1.0 / jnp.sum(gexp, axis=1, keepdims=True)

    lo = MOE_GROUPS + MOE_EXPERTS_PER_GROUP * g_sel
    in_grp = jnp.logical_and(lane >= lo, lane < lo + MOE_EXPERTS_PER_GROUP)
    el = jnp.where(in_grp, logits, -jnp.inf)
    emax = jnp.max(el, axis=1, keepdims=True)
    eexp = jnp.where(in_grp, jnp.exp(logits - emax), 0.0)
    prob = eexp / jnp.sum(eexp, axis=1, keepdims=True)
    cand1 = jnp.where(in_grp, prob, -1.0)
    p1 = jnp.max(cand1, axis=1, keepdims=True)
    j1 = first(cand1 == p1)
    cand2 = jnp.where(lane == j1, -1.0, cand1)
    p2 = jnp.max(cand2, axis=1, keepdims=True)
    j2 = first(cand2 == p2)
    denom = p1 + p2
    wt1 = p1 / denom * g_w
    wt2 = p2 / denom * g_w
    e1 = j1 - MOE_GROUPS
    e2 = j2 - MOE_GROUPS

    hit1 = lane == e1
    hit2 = lane == e2
    row = lax.broadcasted_iota(I32, (tm, tm), 0)
    col = lax.broadcasted_iota(I32, (tm, tm), 1)
    before = (col < row).astype(BF16)
    pre = _dot(before, jnp.concatenate([hit1.astype(BF16), hit2.astype(BF16)], axis=1))
    pre1, pre2 = pre[:, :LANES], pre[:, LANES:]
    carry = carry_sc[...]
    cnt1 = jnp.sum(hit1.astype(F32), axis=0, keepdims=True)
    cnt2 = jnp.sum(hit2.astype(F32), axis=0, keepdims=True)
    rank1 = jnp.sum(jnp.where(hit1, pre1 + carry, 0.0), axis=1, keepdims=True)
    rank2 = jnp.sum(jnp.where(hit2, pre2 + (carry + cnt1), 0.0), axis=1, keepdims=True)
    total = carry + cnt1 + cnt2
    carry_sc[...] = total
    cnt_ref[...] = jnp.broadcast_to(total, cnt_ref.shape)

    bits = lambda x: lax.bitcast_convert_type(jnp.broadcast_to(x, logits.shape), I32)
    meta = jnp.where(lane == 0, e1, 0)
    meta = jnp.where(lane == 1, e2, meta)
    meta = jnp.where(lane == 2, rank1.astype(I32), meta)
    meta = jnp.where(lane == 3, rank2.astype(I32), meta)
    meta = jnp.where(lane == 4, bits(wt1), meta)
    meta = jnp.where(lane == 5, bits(wt2), meta)
    meta_ref[...] = meta
    metat_ref[...] = meta.T[:SUBLANES, :]


def _router(h, gain, w_grp, b_grp, w_rt, b_rt):
    t, d = h.shape
    tm = _tile(t, ROUTE_TILE)
    slab = d // 2 // LANES
    n_used = MOE_GROUPS + MOE_EXPERTS
    w = jnp.pad(jnp.concatenate([w_grp, w_rt], axis=1), ((0, 0), (0, LANES - n_used)))
    w12 = jnp.concatenate(_split(w, 2), axis=1)
    b = jnp.pad(jnp.concatenate([b_grp, b_rt]), (0, LANES - n_used)).reshape(1, LANES)
    return pl.pallas_call(
        _router_kernel,
        grid=(t // tm,),
        in_specs=[pl.BlockSpec((tm, d), lambda i: (i, 0)), pl.BlockSpec((1, d), lambda i: (0, 0)),
                  pl.BlockSpec((d, 2 * LANES), lambda i: (0, 0)), pl.BlockSpec((1, LANES), lambda i: (0, 0))],
        out_specs=[pl.BlockSpec((tm * slab, LANES), lambda i: (i, 0)),
                   pl.BlockSpec((tm, LANES), lambda i: (i, 0)),
                   pl.BlockSpec((SUBLANES, tm), lambda i: (0, i)),
                   pl.BlockSpec((SUBLANES, LANES), lambda i: (0, 0))],
        out_shape=[jax.ShapeDtypeStruct((t * slab, LANES), U32),
                   jax.ShapeDtypeStruct((t, LANES), I32),
                   jax.ShapeDtypeStruct((SUBLANES, t), I32),
                   jax.ShapeDtypeStruct((SUBLANES, LANES), F32)],
        scratch_shapes=[pltpu.VMEM((1, LANES), F32)],
        compiler_params=_params("arbitrary"),
        name="moe_router",
    )(h, gain.reshape(1, d), w12, b)


def _dest_kernel(offs_ref, mt_ref, dest_ref):
    mt = mt_ref[...]
    experts = mt[0:2, :]
    base = jnp.zeros(experts.shape, I32)
    for e in range(MOE_EXPERTS):
        base = jnp.where(experts == e, offs_ref[e], base)
    row = lax.broadcasted_iota(I32, mt.shape, 0)
    dest_ref[...] = jnp.where(row < 2, jnp.concatenate([base + mt[2:4, :], mt[2:SUBLANES, :]], axis=0), 0)


def _dest_rows_all(meta_t, offs):
    rows, t = meta_t.shape
    return pl.pallas_call(
        _dest_kernel,
        grid_spec=pltpu.PrefetchScalarGridSpec(
            num_scalar_prefetch=1,
            grid=(1,),
            in_specs=[pl.BlockSpec((rows, t), lambda i, offs: (0, 0))],
            out_specs=pl.BlockSpec((rows, t), lambda i, offs: (0, 0)),
        ),
        out_shape=jax.ShapeDtypeStruct((rows, t), I32),
        compiler_params=_params("arbitrary"),
        name="moe_dest",
    )(offs, meta_t)


def _slab_copy(src, src_tok, dst, dst_tok, sem, slab):
    rows = lambda tok: pl.ds(pl.multiple_of(tok * slab, slab), slab)
    return pltpu.make_async_copy(src.at[rows(src_tok)], dst.at[rows(dst_tok)], sem)


_ROW_UNROLL = 8


def _dispatch_kernel(zs_ref, nv_ref, dest_ref, t_ref, xs_ref, zero_sc, sem, *, tm, tz, slab):
    i = pl.program_id(0)

    @pl.when(i == 0)
    def _():
        zero_sc[...] = jnp.zeros(zero_sc.shape, zero_sc.dtype)
        fill = lambda tok0: pltpu.make_async_copy(
            zero_sc, xs_ref.at[pl.ds(pl.multiple_of(tok0 * slab, tz * slab), tz * slab)], sem)
        fills = [fill(zs_ref[e]) for e in range(MOE_EXPERTS)]
        for c in fills:
            c.start()
        for c in fills:
            c.wait()

        def tail(tile, carry):
            c = fill(tile * tz)
            c.start()
            c.wait()
            return carry

        lax.fori_loop(nv_ref[0], xs_ref.shape[0] // (tz * slab), tail, 0)

    def issue(blk, carry):
        for u in range(_ROW_UNROLL):
            r = blk * _ROW_UNROLL + u
            _slab_copy(t_ref, r, xs_ref, dest_ref[0, r], sem, slab).start(priority=0)
            _slab_copy(t_ref, r, xs_ref, dest_ref[1, r], sem, slab).start(priority=1)
        return carry

    lax.fori_loop(0, tm // _ROW_UNROLL, issue, 0)

    def drain(blk, carry):
        for u in range(2 * _ROW_UNROLL):
            _slab_copy(t_ref, 0, xs_ref, 0, sem, slab).wait()
        return carry

    lax.fori_loop(0, tm // _ROW_UNROLL, drain, 0)


def _dispatch(t_slabs, dest_t, zero_start, n_valid, n_rows):
    t = dest_t.shape[1]
    slab = t_slabs.shape[0] // t
    tm = _tile(t, MOVE_TILE)
    tz = EXPERT_TILE
    mspec = pl.BlockSpec((SUBLANES, tm), lambda i, *_: (0, i), memory_space=pltpu.SMEM)
    return pl.pallas_call(
        functools.partial(_dispatch_kernel, tm=tm, tz=tz, slab=slab),
        grid_spec=pltpu.PrefetchScalarGridSpec(
            num_scalar_prefetch=2,
            grid=(t // tm,),
            in_specs=[mspec, pl.BlockSpec((tm * slab, LANES), lambda i, *_: (i, 0))],
            out_specs=pl.BlockSpec(memory_space=pl.ANY),
            scratch_shapes=[pltpu.VMEM((tz * slab, LANES), U32), pltpu.SemaphoreType.DMA(())],
        ),
        out_shape=jax.ShapeDtypeStruct((n_rows * slab, LANES), U32),
        compiler_params=_params("arbitrary"),
        name="moe_dispatch",
    )(zero_start, n_valid, dest_t, t_slabs)


def _experts_kernel(te_ref, nv_ref, x_ref, wg_ref, wu_ref, wd_ref, y_ref, wg_sc, wu_sc, wd_sc, *, tm):
    i = pl.program_id(0)
    live = i < nv_ref[0]

    @pl.when(jnp.logical_or(i == 0, te_ref[i] != te_ref[jnp.maximum(i - 1, 0)]))
    def _():
        wg_sc[...] = wg_ref[0, 0].astype(BF16)
        wu_sc[...] = wu_ref[0, 0].astype(BF16)
        wd_sc[...] = wd_ref[0, 0].astype(BF16)

    @pl.when(jnp.logical_not(live))
    def _():
        y_ref[...] = jnp.zeros(y_ref.shape, y_ref.dtype)

    @pl.when(live)
    def _():
        hi, lo = _unpack_halves(_from_slabs(x_ref, tm))
        x = jnp.concatenate([hi, lo], axis=1).astype(BF16)
        g = _dot(x, wg_sc[...])
        u = _dot(x, wu_sc[...])
        hid = (g * jax.nn.sigmoid(g) * u).astype(BF16)
        _to_slabs(y_ref, _pack_halves(_dot(hid, wd_sc[...])))


def _experts(xs, tile_expert, n_valid, w_gate, w_up, w_down, layer, n_tiles):
    tm = EXPERT_TILE
    _, _, d, f = w_gate.shape
    rows = tm * (d // 2 // LANES)
    live = lambda i, nv: jnp.minimum(i, nv[0] - 1)
    return pl.pallas_call(
        functools.partial(_experts_kernel, tm=tm),
        grid_spec=pltpu.PrefetchScalarGridSpec(
            num_scalar_prefetch=2,
            grid=(n_tiles,),
            in_specs=[pl.BlockSpec((rows, LANES), lambda i, te, nv: (live(i, nv), 0)),
                      pl.BlockSpec((1, 1, d, f), lambda i, te, nv: (layer, te[i], 0, 0)),
                      pl.BlockSpec((1, 1, d, f), lambda i, te, nv: (layer, te[i], 0, 0)),
                      pl.BlockSpec((1, 1, f, d), lambda i, te, nv: (layer, te[i], 0, 0))],
            out_specs=pl.BlockSpec((rows, LANES), lambda i, te, nv: (i, 0)),
            scratch_shapes=[pltpu.VMEM((d, f), BF16), pltpu.VMEM((d, f), BF16), pltpu.VMEM((f, d), BF16)],
        ),
        out_shape=jax.ShapeDtypeStruct((n_tiles * rows, LANES), U32),
        compiler_params=_params("arbitrary"),
        name="moe_experts",
    )(tile_expert, n_valid, xs, w_gate, w_up, w_down)


def _combine_kernel(dest_ref, dnext_ref, meta_ref, h_ref, g_ref, ys_ref, *rest, tm, slab, final):
    if final:
        o_ref, y_sc, sem = rest
    else:
        hn_ref, a_ref, y_sc, sem = rest
    i = pl.program_id(0)
    n = pl.num_programs(0)

    def gather(dref, slot):
        def issue(blk, carry):
            for u in range(_ROW_UNROLL):
                r = blk * _ROW_UNROLL + u
                _slab_copy(ys_ref, dref[0, r], y_sc.at[slot, 0], r, sem.at[slot], slab).start(priority=0)
                _slab_copy(ys_ref, dref[1, r], y_sc.at[slot, 1], r, sem.at[slot], slab).start(priority=1)
            return carry

        lax.fori_loop(0, tm // _ROW_UNROLL, issue, 0)

    def finish(slot):
        def drain(blk, carry):
            for u in range(2 * _ROW_UNROLL):
                _slab_copy(ys_ref, 0, y_sc.at[slot, 0], 0, sem.at[slot], slab).wait()
            return carry

        lax.fori_loop(0, tm // _ROW_UNROLL, drain, 0)

        meta = meta_ref[...]
        lane = lax.broadcasted_iota(I32, meta.shape, 1)
        wbits = lax.bitcast_convert_type(meta, F32)
        wt1 = jnp.sum(jnp.where(lane == 4, wbits, 0.0), axis=1, keepdims=True)
        wt2 = jnp.sum(jnp.where(lane == 5, wbits, 0.0), axis=1, keepdims=True)
        hi1, lo1 = _unpack_halves(_from_slabs(y_sc.at[slot, 0], tm))
        hi2, lo2 = _unpack_halves(_from_slabs(y_sc.at[slot, 1], tm))
        moe = jnp.concatenate([wt1 * hi1 + wt2 * hi2, wt1 * lo1 + wt2 * lo2], axis=1)
        h_new = h_ref[...] + moe
        if final:
            o_ref[...] = _rms(h_new, g_ref[...])
        else:
            hn_ref[...] = h_new
            a_ref[...] = _rms(h_new, g_ref[...]).astype(a_ref.dtype)

    @pl.when(i == 0)
    def _():
        gather(dest_ref, 0)

    for slot in (0, 1):
        @pl.when((i & 1) == slot)
        def _():
            @pl.when(i + 1 < n)
            def _():
                gather(dnext_ref, 1 - slot)

            finish(slot)


def _combine(ys, dest_t, meta, h, gain, final):
    t, d = h.shape
    tm = _tile(t, COMBINE_TILE)
    slab = d // 2 // LANES
    n = t // tm
    mspec = pl.BlockSpec((SUBLANES, tm), lambda i: (0, i), memory_space=pltpu.SMEM)
    mnext = pl.BlockSpec((SUBLANES, tm), lambda i: (0, jnp.minimum(i + 1, n - 1)), memory_space=pltpu.SMEM)
    row = pl.BlockSpec((tm, d), lambda i: (i, 0))
    if final:
        out_specs, out_shape = row, jax.ShapeDtypeStruct((t, d), F32)
    else:
        out_specs = [row, row]
        out_shape = [jax.ShapeDtypeStruct((t, d), F32), jax.ShapeDtypeStruct((t, d), BF16)]
    return pl.pallas_call(
        functools.partial(_combine_kernel, tm=tm, slab=slab, final=final),
        grid=(n,),
        in_specs=[mspec, mnext, pl.BlockSpec((tm, LANES), lambda i: (i, 0)), row,
                  pl.BlockSpec((1, d), lambda i: (0, 0)), pl.BlockSpec(memory_space=pl.ANY)],
        out_specs=out_specs,
        out_shape=out_shape,
        scratch_shapes=[pltpu.VMEM((2, 2, tm * slab, LANES), U32), pltpu.SemaphoreType.DMA((2,))],
        compiler_params=_params("arbitrary"),
        name="moe_combine",
    )(dest_t, dest_t, meta, h, gain.reshape(1, d), ys)


def _hier_moe(h, ffn_gain, w_grp, b_grp, w_rt, b_rt, w_gate, w_up, w_down, layer, next_gain, final):
    t, d = h.shape
    tmx = EXPERT_TILE
    t_slabs, meta, meta_t, cnt = _router(h, ffn_gain, w_grp, b_grp, w_rt, b_rt)

    counts = cnt[0, :MOE_EXPERTS].astype(I32)
    padded = (counts + tmx - 1) // tmx * tmx
    ends = jnp.cumsum(padded)
    offs = ends - padded
    n_tiles = (2 * t) // tmx + MOE_EXPERTS
    n_valid = (ends[-1] // tmx).reshape(1)
    tile_start = jnp.minimum(jnp.arange(n_tiles, dtype=I32) * tmx, ends[-1] - 1)
    tile_expert = jnp.sum((ends[None, :] <= tile_start[:, None]).astype(I32), axis=1)
    zero_start = offs + counts // tmx * tmx

    dest_t = _dest_rows_all(meta_t, offs)
    xs = _dispatch(t_slabs, dest_t, zero_start, n_valid, (n_tiles + 1) * tmx)
    ys = _experts(xs, tile_expert, n_valid, w_gate, w_up, w_down, layer, n_tiles)
    return _combine(ys, dest_t, meta, h, next_gain, final)


def kernel(x, positions, attn_norm, ffn_norm, final_norm, mla_w_dq, mla_q_norm, mla_w_uq, mla_w_dkv,
           mla_kv_norm, mla_w_ukv, mla_w_o, fox_w_qkv, fox_q_norm, fox_k_norm, fox_w_f, fox_b_f,
           fox_w_og, fox_w_o, moe_w_grp, moe_b_grp, moe_w_rt, moe_b_rt, moe_w_gate, moe_w_up, moe_w_down):
    batch, seq, d = x.shape
    depth = attn_norm.shape[0]
    t = batch * seq
    cos_t, sin_t = _rope_tables(positions)
    h = x.reshape(t, d)
    a = _norm(h, attn_norm[0], BF16)
    out = None
    for i in range(depth):
        j = i // 2
        if i % 2 == 0:
            heads = mla_w_uq.shape[2] // (MLA_NOPE + MLA_ROPE)
            q, k, v_t = _mla_project(a, cos_t, sin_t, mla_w_dq[j], mla_q_norm[j], mla_w_uq[j],
                                     mla_w_dkv[j], mla_kv_norm[j], mla_w_ukv[j])
            o = _mla_attention(q, k, v_t, batch, seq, heads)
            w_o = mla_w_o[j]
        else:
            dh = FOX_HEAD_DIM
            heads = fox_w_qkv.shape[2] // (3 * dh)
            w_qkv = fox_w_qkv[j].astype(BF16)
            gain_row = jnp.concatenate([jnp.tile(fox_q_norm[j] * (dh ** -0.5 * LOG2_E), heads),
                                        jnp.tile(fox_k_norm[j], heads)])[None, :]
            qk = _fox_qk(a, w_qkv[:, :2 * heads * dh], gain_row)
            v_t = _mm_nt(w_qkv[:, 2 * heads * dh:].T, a, BF16, name="fox_v")
            gate = _mm(a, fox_w_og[j].astype(BF16), BF16, act="sigmoid", name="fox_gate")
            c_t, c_tok = _fox_forget_cumsum(h, attn_norm[i], fox_w_f[j], fox_b_f[j], batch, seq)
            c_rows = c_t[:, :heads, :].reshape(batch, heads, 1, seq)
            o = _fox_attention(qk, v_t, c_tok, c_rows, gate, batch, seq, heads)
            w_o = fox_w_o[j]
        h = _mm_residual(o, w_o.astype(BF16), h)
        final = i == depth - 1
        next_gain = final_norm if final else attn_norm[i + 1]
        res = _hier_moe(h, ffn_norm[i], moe_w_grp[i], moe_b_grp[i], moe_w_rt[i], moe_b_rt[i],
                        moe_w_gate, moe_w_up, moe_w_down, i, next_gain, final)
        if final:
            out = res
        else:
            h, a = res
    return out.reshape(batch, seq, d)
```

```python
import functools

import jax
import jax.numpy as jnp
from jax import lax
from jax.experimental import pallas as pl
from jax.experimental.pallas import tpu as pltpu

F32 = jnp.float32
BF16 = jnp.bfloat16
I32 = jnp.int32
U32 = jnp.uint32

RMS_EPS = 1e-6
NEG_INF = -1e30
CHUNK = 64
MLA_NOPE = 128
MLA_ROPE = 64
MLA_V = 128
MLA_QK_PAD = 256
ROPE_THETA = 10000.0
LOG2_E = 1.4426950408889634
FOX_HEAD_DIM = 128
MOE_GROUPS = 8
MOE_EXPERTS_PER_GROUP = 4
MOE_EXPERTS = MOE_GROUPS * MOE_EXPERTS_PER_GROUP

LANES = 128
SUBLANES = 8
V7X_VMEM_LIMIT_BYTES = 56 * 1024 * 1024

ROW_TILE = 1024
COL_TILE = 1024
ATTN_TILE = 512
ROUTE_TILE = 512
MOVE_TILE = 512
COMBINE_TILE = 256
EXPERT_TILE = 256


def _params(*sem):
    return pltpu.CompilerParams(dimension_semantics=sem, vmem_limit_bytes=V7X_VMEM_LIMIT_BYTES)


def _tile(n, t):
    if n <= t:
        return n
    step = LANES if t % LANES == 0 else SUBLANES
    for c in range(t - t % step, 0, -step):
        if n % c == 0:
            return c
    raise ValueError(f"no aligned tile for {n} under {t}")


def _rms(x, gain):
    ms = jnp.mean(x * x, axis=-1, keepdims=True)
    return x * lax.rsqrt(ms + RMS_EPS) * gain


def _dot(a, b):
    return jnp.dot(a, b, preferred_element_type=F32)


def _nt_dot(a, b):
    return lax.dot_general(a, b, (((1,), (1,)), ((), ())), preferred_element_type=F32)


def _pack_halves(x):
    n = x.shape[1] // 2
    hi = lax.bitcast_convert_type(x[:, :n].astype(BF16).astype(F32), U32)
    lo = lax.bitcast_convert_type(x[:, n:].astype(BF16).astype(F32), U32)
    return hi | lax.shift_right_logical(lo, jnp.uint32(16))


def _unpack_halves(p):
    hi = lax.bitcast_convert_type(p & jnp.uint32(0xFFFF0000), F32)
    lo = lax.bitcast_convert_type(lax.shift_left(p, jnp.uint32(16)), F32)
    return hi, lo


def _to_slabs(ref, x):
    m, width = x.shape
    c = width // LANES
    for j in range(c):
        ref[pl.ds(j, m, stride=c), :] = x[:, j * LANES:(j + 1) * LANES]


def _from_slabs(ref, m):
    c = ref.shape[0] // m
    return jnp.concatenate([ref[pl.ds(j, m, stride=c), :] for j in range(c)], axis=1)


def _split(x, terms):
    out = []
    for _ in range(terms):
        hi = x.astype(BF16)
        out.append(hi)
        x = x - hi.astype(F32)
    return out


def _split_dot(x, w12_ref):
    m = x.shape[0]
    n = w12_ref.shape[1] // 2
    prod = _dot(jnp.concatenate(_split(x, 2), axis=0), w12_ref[...])
    return (prod[:m, :n] + (prod[:m, n:] + prod[m:, :n])) + prod[m:, n:]


def _rope_lanes(seg, cos_t, sin_t):
    half = MLA_ROPE // 2
    lane = lax.broadcasted_iota(I32, seg.shape, 1)
    swapped = jnp.where(lane < half, pltpu.roll(seg, LANES - half, 1), pltpu.roll(seg, half, 1))
    return seg * cos_t + swapped * sin_t


def _rope_table_kernel(pos_ref, freq_ref, cos_ref, sin_ref):
    ang = pos_ref[...] * freq_ref[...]
    lane = lax.broadcasted_iota(I32, ang.shape, 1)
    half = MLA_ROPE // 2
    valid = lane < MLA_ROPE
    cos_ref[...] = jnp.where(valid, jnp.cos(ang), 0.0)
    sin_ref[...] = jnp.where(valid, jnp.where(lane < half, -jnp.sin(ang), jnp.sin(ang)), 0.0)


def _rope_tables(positions):
    t = positions.size
    tm = _tile(t, ROW_TILE)
    half = MLA_ROPE // 2
    inv_freq = ROPE_THETA ** (-jnp.arange(0, MLA_ROPE, 2, dtype=F32) / MLA_ROPE)
    freq_row = jnp.concatenate([inv_freq, inv_freq, jnp.zeros((LANES - 2 * half,), F32)])[None, :]
    pos = positions.reshape(t, 1).astype(F32)
    out = jax.ShapeDtypeStruct((t, LANES), F32)
    return pl.pallas_call(
        _rope_table_kernel,
        grid=(t // tm,),
        in_specs=[pl.BlockSpec((tm, 1), lambda i: (i, 0)), pl.BlockSpec((1, LANES), lambda i: (0, 0))],
        out_specs=[pl.BlockSpec((tm, LANES), lambda i: (i, 0))] * 2,
        out_shape=[out, out],
        compiler_params=_params("parallel"),
        name="rope_tables",
    )(pos, freq_row)


def _norm_kernel(h_ref, g_ref, a_ref):
    a_ref[...] = _rms(h_ref[...], g_ref[...]).astype(a_ref.dtype)


def _norm(h, gain, out_dtype):
    t, d = h.shape
    tm = _tile(t, ROUTE_TILE)
    return pl.pallas_call(
        _norm_kernel,
        grid=(t // tm,),
        in_specs=[pl.BlockSpec((tm, d), lambda i: (i, 0)), pl.BlockSpec((1, d), lambda i: (0, 0))],
        out_specs=pl.BlockSpec((tm, d), lambda i: (i, 0)),
        out_shape=jax.ShapeDtypeStruct((t, d), out_dtype),
        compiler_params=_params("parallel"),
        name="rmsnorm",
    )(h, gain.reshape(1, d))


def _mm_kernel(x_ref, w_ref, o_ref, *, act):
    y = _dot(x_ref[...], w_ref[...])
    if act == "sigmoid":
        y = jax.nn.sigmoid(y)
    o_ref[...] = y.astype(o_ref.dtype)


def _mm(x, w, out_dtype, act=None, name="mm", col_tile=COL_TILE):
    m, k = x.shape
    n = w.shape[1]
    tm, tn = _tile(m, ROW_TILE), _tile(n, col_tile)
    return pl.pallas_call(
        functools.partial(_mm_kernel, act=act),
        grid=(m // tm, n // tn),
        in_specs=[pl.BlockSpec((tm, k), lambda i, j: (i, 0)), pl.BlockSpec((k, tn), lambda i, j: (0, j))],
        out_specs=pl.BlockSpec((tm, tn), lambda i, j: (i, j)),
        out_shape=jax.ShapeDtypeStruct((m, n), out_dtype),
        compiler_params=_params("parallel", "parallel"),
        name=name,
    )(x, w)


def _mm_nt_kernel(wt_ref, x_ref, o_ref):
    o_ref[...] = _nt_dot(wt_ref[...], x_ref[...]).astype(o_ref.dtype)


def _mm_nt(w_t, x, out_dtype, name):
    n, k = w_t.shape
    m = x.shape[0]
    tm, tn = _tile(m, ROW_TILE), _tile(n, COL_TILE)
    return pl.pallas_call(
        _mm_nt_kernel,
        grid=(m // tm, n // tn),
        in_specs=[pl.BlockSpec((tn, k), lambda i, j: (j, 0)), pl.BlockSpec((tm, k), lambda i, j: (i, 0))],
        out_specs=pl.BlockSpec((tn, tm), lambda i, j: (j, i)),
        out_shape=jax.ShapeDtypeStruct((n, m), out_dtype),
        compiler_params=_params("parallel", "parallel"),
        name=name,
    )(w_t, x)


def _mm_res_kernel(x_ref, w_ref, h_ref, o_ref):
    o_ref[...] = h_ref[...] + _dot(x_ref[...], w_ref[...])


def _mm_residual(x, w, h):
    m, k = x.shape
    n = w.shape[1]
    tm, tn = _tile(m, ROW_TILE), _tile(n, COL_TILE)
    return pl.pallas_call(
        _mm_res_kernel,
        grid=(m // tm, n // tn),
        in_specs=[pl.BlockSpec((tm, k), lambda i, j: (i, 0)), pl.BlockSpec((k, tn), lambda i, j: (0, j)),
                  pl.BlockSpec((tm, tn), lambda i, j: (i, j))],
        out_specs=pl.BlockSpec((tm, tn), lambda i, j: (i, j)),
        out_shape=jax.ShapeDtypeStruct((m, n), F32),
        compiler_params=_params("parallel", "parallel"),
        name="out_proj_residual",
    )(x, w, h)


def _mla_q_kernel(c_ref, g_ref, w_ref, wsw_ref, cos_ref, sin_ref, q_ref, *, scale):
    cq = _rms(c_ref[...], g_ref[...]).astype(BF16)
    y = _dot(cq, w_ref[...]) * scale
    y_sw = _dot(cq, wsw_ref[...]) * scale
    cos_t, sin_t = cos_ref[...], sin_ref[...]
    for hd in range(y.shape[1] // MLA_QK_PAD):
        base = hd * MLA_QK_PAD
        q_ref[:, base:base + MLA_NOPE] = y[:, base:base + MLA_NOPE].astype(q_ref.dtype)
        roped = (y[:, base + MLA_NOPE:base + MLA_QK_PAD] * cos_t
                 + y_sw[:, hd * LANES:(hd + 1) * LANES] * sin_t)
        q_ref[:, base + MLA_NOPE:base + MLA_QK_PAD] = roped.astype(q_ref.dtype)


def _mla_kv_kernel(c_ref, pe_ref, g_ref, wk_ref, wvt_ref, cos_ref, sin_ref, k_ref, vt_ref):
    ckv = _rms(c_ref[...], g_ref[...]).astype(BF16)
    kn = _dot(ckv, wk_ref[...])
    vt_ref[...] = _nt_dot(wvt_ref[...], ckv).astype(vt_ref.dtype)
    k_pe = _rope_lanes(pe_ref[...], cos_ref[...], sin_ref[...]).astype(k_ref.dtype)
    for hd in range(kn.shape[1] // MLA_NOPE):
        k_ref[:, hd * MLA_QK_PAD:hd * MLA_QK_PAD + MLA_NOPE] = (
            kn[:, hd * MLA_NOPE:(hd + 1) * MLA_NOPE].astype(k_ref.dtype))
        k_ref[:, hd * MLA_QK_PAD + MLA_NOPE:(hd + 1) * MLA_QK_PAD] = k_pe


def _mla_project(a, cos_t, sin_t, w_dq, q_norm, w_uq, w_dkv, kv_norm, w_ukv):
    t, d = a.shape
    q_lora = w_dq.shape[1]
    kv_lora = w_dkv.shape[1] - MLA_ROPE
    heads = w_uq.shape[1] // (MLA_NOPE + MLA_ROPE)
    assert q_lora % LANES == 0 and kv_lora % LANES == 0

    w_down = jnp.concatenate(
        [w_dq, w_dkv, jnp.zeros((d, LANES - MLA_ROPE), w_dkv.dtype)], axis=1).astype(BF16)
    s1 = _mm(a, w_down, F32, name="mla_down", col_tile=w_down.shape[1])

    w_q = w_uq.reshape(q_lora, heads, MLA_NOPE + MLA_ROPE)
    half = MLA_ROPE // 2
    w_sw = jnp.concatenate([w_q[:, :, MLA_NOPE + half:], w_q[:, :, MLA_NOPE:MLA_NOPE + half]], axis=2)
    w_sw = jnp.pad(w_sw, ((0, 0), (0, 0), (0, LANES - MLA_ROPE))).reshape(q_lora, heads * LANES).astype(BF16)
    w_q = jnp.pad(w_q, ((0, 0), (0, 0), (0, MLA_QK_PAD - MLA_NOPE - MLA_ROPE)))
    w_q = w_q.reshape(q_lora, heads * MLA_QK_PAD).astype(BF16)
    w_kv = w_ukv.reshape(kv_lora, heads, MLA_NOPE + MLA_V)
    w_k = w_kv[:, :, :MLA_NOPE].reshape(kv_lora, heads * MLA_NOPE).astype(BF16)
    w_vt = w_kv[:, :, MLA_NOPE:].reshape(kv_lora, heads * MLA_V).T.astype(BF16)

    tm = _tile(t, ROW_TILE)
    scale = (MLA_NOPE + MLA_ROPE) ** -0.5 * LOG2_E
    tn = heads * MLA_QK_PAD
    row128 = pl.BlockSpec((tm, LANES), lambda i, j: (i, 0))
    q = pl.pallas_call(
        functools.partial(_mla_q_kernel, scale=scale),
        grid=(t // tm, heads * MLA_QK_PAD // tn),
        in_specs=[pl.BlockSpec((tm, q_lora), lambda i, j: (i, 0)),
                  pl.BlockSpec((1, q_lora), lambda i, j: (0, 0)),
                  pl.BlockSpec((q_lora, tn), lambda i, j: (0, j)),
                  pl.BlockSpec((q_lora, tn // 2), lambda i, j: (0, j)), row128, row128],
        out_specs=pl.BlockSpec((tm, tn), lambda i, j: (i, j)),
        out_shape=jax.ShapeDtypeStruct((t, heads * MLA_QK_PAD), BF16),
        compiler_params=_params("parallel", "parallel"),
        name="mla_q",
    )(s1, q_norm.reshape(1, q_lora), w_q, w_sw, cos_t, sin_t)

    hb = heads
    kv_blk = q_lora // kv_lora
    assert q_lora % kv_lora == 0
    pe_blk = (q_lora + kv_lora) // LANES
    k, v_t = pl.pallas_call(
        _mla_kv_kernel,
        grid=(t // tm, heads // hb),
        in_specs=[pl.BlockSpec((tm, kv_lora), lambda i, j: (i, kv_blk)),
                  pl.BlockSpec((tm, LANES), lambda i, j: (i, pe_blk)),
                  pl.BlockSpec((1, kv_lora), lambda i, j: (0, 0)),
                  pl.BlockSpec((kv_lora, hb * MLA_NOPE), lambda i, j: (0, j)),
                  pl.BlockSpec((hb * MLA_V, kv_lora), lambda i, j: (j, 0)), row128, row128],
        out_specs=[pl.BlockSpec((tm, hb * MLA_QK_PAD), lambda i, j: (i, j)),
                   pl.BlockSpec((hb * MLA_V, tm), lambda i, j: (j, i))],
        out_shape=[jax.ShapeDtypeStruct((t, heads * MLA_QK_PAD), BF16),
                   jax.ShapeDtypeStruct((heads * MLA_V, t), BF16)],
        compiler_params=_params("parallel", "parallel"),
        name="mla_kv",
    )(s1, s1, kv_norm.reshape(1, kv_lora), w_k, w_vt, cos_t, sin_t)
    return q, k, v_t


def _col_max(x):
    while x.shape[0] > SUBLANES and x.shape[0] % (2 * SUBLANES) == 0:
        half = x.shape[0] // 2
        x = jnp.maximum(x[:half], x[half:])
    return jnp.max(x, axis=0, keepdims=True)


def _softmax_update(s_t, m_cur, v_t, m_sc, l_sc, acc_sc, col_shift=None):
    m_prev = m_sc[...]
    if col_shift is not None:
        m_cur = m_cur + col_shift
    m_new = jnp.maximum(m_prev, m_cur)
    shift = m_new if col_shift is None else m_new - col_shift
    p_t = jnp.exp2(s_t - shift)
    alpha = jnp.exp2(m_prev - m_new)
    l_sc[...] = alpha * l_sc[...] + jnp.sum(p_t, axis=0, keepdims=True)
    acc_sc[...] = alpha * acc_sc[...] + _dot(v_t, p_t.astype(v_t.dtype))
    m_sc[...] = m_new


def _init_softmax(m_sc, l_sc, acc_sc):
    m_sc[...] = jnp.full(m_sc.shape, NEG_INF, F32)
    l_sc[...] = jnp.zeros(l_sc.shape, F32)
    acc_sc[...] = jnp.zeros(acc_sc.shape, F32)


def _row_sweep(nq, trips_ref, refresh_q, scores, consume, consume_diag, bufs):
    def produce(kt, s_buf, m_buf):
        s_t = scores(kt)
        s_buf[...] = s_t
        m_buf[...] = _col_max(s_t)

    refresh_q(0)
    produce(0, *bufs[0])
    first = 0
    for qi in range(nq):
        cur, other = bufs[first], bufs[1 - first]

        def pair(p, carry, qi=qi, cur=cur, other=other):
            produce(2 * p + 1, *other)
            consume(qi, 2 * p, *cur)
            produce(2 * p + 2, *cur)
            consume(qi, 2 * p + 1, *other)
            return carry

        lax.fori_loop(0, trips_ref[qi], pair, 0)
        if qi % 2 == 1:
            produce(qi, *other)
            consume(qi, qi - 1, *cur)
            cur, other, first = other, cur, 1 - first
        if qi + 1 < nq:
            refresh_q(qi + 1)
            produce(0, *other)
        consume_diag(qi, *cur)
        first = 1 - first


def _mla_attn_kernel(trips_ref, q_ref, k_ref, vt_ref, o_ref, m_sc, l_sc, acc_sc, sa_sc, ma_sc, sb_sc, mb_sc,
                     qt_sc, *, tq, nq, chunk_shift):
    _init_softmax(m_sc, l_sc, acc_sc)
    rows = lambda i: pl.ds(pl.multiple_of(i * tq, tq), tq)

    def refresh_q(qi):
        qt_sc[...] = q_ref[rows(qi), :].T

    def scores(kt):
        return _dot(k_ref[rows(kt), :], qt_sc[...])

    def consume(qi, kt, s_buf, m_buf):
        _softmax_update(s_buf[...], m_buf[...], vt_ref[:, rows(kt)], m_sc, l_sc, acc_sc)

    def consume_diag(qi, s_buf, m_buf):
        s_t = s_buf[...]
        key = lax.broadcasted_iota(I32, s_t.shape, 0)
        qry = lax.broadcasted_iota(I32, s_t.shape, 1)
        allowed = lax.shift_right_logical(key, chunk_shift) <= lax.shift_right_logical(qry, chunk_shift)
        s_t = jnp.where(allowed, s_t, NEG_INF)
        _softmax_update(s_t, _col_max(s_t), vt_ref[:, rows(qi)], m_sc, l_sc, acc_sc)
        o_ref[rows(qi), :] = (acc_sc[...] / l_sc[...]).T.astype(o_ref.dtype)
        _init_softmax(m_sc, l_sc, acc_sc)

    _row_sweep(nq, trips_ref, refresh_q, scores, consume, consume_diag, ((sa_sc, ma_sc), (sb_sc, mb_sc)))


def _mla_attention(q, k, v_t, batch, seq, heads):
    tq = _tile(seq, ATTN_TILE)
    assert tq % CHUNK == 0 and CHUNK & (CHUNK - 1) == 0
    nq = seq // tq
    return pl.pallas_call(
        functools.partial(_mla_attn_kernel, tq=tq, nq=nq, chunk_shift=CHUNK.bit_length() - 1),
        grid_spec=pltpu.PrefetchScalarGridSpec(
            num_scalar_prefetch=1,
            grid=(batch, heads),
            in_specs=[pl.BlockSpec((seq, MLA_QK_PAD), lambda b, h, tr: (b, h)),
                      pl.BlockSpec((seq, MLA_QK_PAD), lambda b, h, tr: (b, h)),
                      pl.BlockSpec((MLA_V, seq), lambda b, h, tr: (h, b))],
            out_specs=pl.BlockSpec((seq, MLA_V), lambda b, h, tr: (b, h)),
            scratch_shapes=[pltpu.VMEM((1, tq), F32), pltpu.VMEM((1, tq), F32), pltpu.VMEM((MLA_V, tq), F32),
                            pltpu.VMEM((tq, tq), F32), pltpu.VMEM((1, tq), F32),
                            pltpu.VMEM((tq, tq), F32), pltpu.VMEM((1, tq), F32),
                            pltpu.VMEM((MLA_QK_PAD, tq), BF16)],
        ),
        out_shape=jax.ShapeDtypeStruct((batch * seq, heads * MLA_V), BF16),
        compiler_params=_params("parallel", "parallel"),
        name="mla_attention",
    )(jnp.arange(nq, dtype=I32) // 2, q, k, v_t)


_FORGET_TERMS = 3


def _fox_attn_kernel(trips_ref, q_ref, k_ref, vt_ref, ctok_ref, crow_ref, g_ref, o_ref, m_sc, l_sc, acc_sc,
                     ka_sc, sa_sc, ma_sc, sb_sc, mb_sc, qt_sc, *, tq, nq):
    head = pl.program_id(1)
    dh = k_ref.shape[1]
    _init_softmax(m_sc, l_sc, acc_sc)
    rows = lambda i: pl.ds(pl.multiple_of(i * tq, tq), tq)

    row = lax.broadcasted_iota(I32, (LANES, tq), 0)
    qt_sc[dh:, :] = jnp.where(row < _FORGET_TERMS, -1.0, 0.0).astype(qt_sc.dtype)

    ka_sc[:, :dh] = k_ref[...]
    terms = _split(ctok_ref[...] * LOG2_E, _FORGET_TERMS)
    src = lax.broadcasted_iota(I32, (LANES, LANES), 0)
    dst = lax.broadcasted_iota(I32, (LANES, LANES), 1)
    aug = None
    for j, term in enumerate(terms):
        pick = jnp.logical_and(src == head, dst == j).astype(BF16)
        part = _dot(term, pick)
        aug = part if aug is None else aug + part
    ka_sc[:, dh:] = aug.astype(ka_sc.dtype)

    def refresh_q(qi):
        qt_sc[:dh, :] = q_ref[rows(qi), :].T

    def scores(kt):
        return _dot(ka_sc[rows(kt), :], qt_sc[...])

    def c_q(qi):
        return crow_ref[0, 0, :, rows(qi)] * LOG2_E

    def consume(qi, kt, s_buf, m_buf):
        _softmax_update(s_buf[...], m_buf[...], vt_ref[:, rows(kt)], m_sc, l_sc, acc_sc, col_shift=c_q(qi))

    def consume_diag(qi, s_buf, m_buf):
        s_t = s_buf[...]
        key = lax.broadcasted_iota(I32, s_t.shape, 0)
        qry = lax.broadcasted_iota(I32, s_t.shape, 1)
        s_t = jnp.where(key <= qry, s_t, NEG_INF)
        _softmax_update(s_t, _col_max(s_t), vt_ref[:, rows(qi)], m_sc, l_sc, acc_sc, col_shift=c_q(qi))
        gated = (acc_sc[...] / l_sc[...]).T * g_ref[rows(qi), :].astype(F32)
        o_ref[rows(qi), :] = gated.astype(o_ref.dtype)
        _init_softmax(m_sc, l_sc, acc_sc)

    _row_sweep(nq, trips_ref, refresh_q, scores, consume, consume_diag, ((sa_sc, ma_sc), (sb_sc, mb_sc)))


def _fox_attention(qk, v_t, c_tok, c_rows, gate, batch, seq, heads):
    tq = _tile(seq, ATTN_TILE)
    assert tq % LANES == 0 or tq == seq
    dh = FOX_HEAD_DIM
    nq = seq // tq
    return pl.pallas_call(
        functools.partial(_fox_attn_kernel, tq=tq, nq=nq),
        grid_spec=pltpu.PrefetchScalarGridSpec(
            num_scalar_prefetch=1,
            grid=(batch, heads),
            in_specs=[pl.BlockSpec((seq, dh), lambda b, h, tr: (b, h)),
                      pl.BlockSpec((seq, dh), lambda b, h, tr: (b, heads + h)),
                      pl.BlockSpec((dh, seq), lambda b, h, tr: (h, b)),
                      pl.BlockSpec((seq, LANES), lambda b, h, tr: (b, 0)),
                      pl.BlockSpec((1, 1, 1, seq), lambda b, h, tr: (b, h, 0, 0)),
                      pl.BlockSpec((seq, dh), lambda b, h, tr: (b, h))],
            out_specs=pl.BlockSpec((seq, dh), lambda b, h, tr: (b, h)),
            scratch_shapes=[pltpu.VMEM((1, tq), F32), pltpu.VMEM((1, tq), F32), pltpu.VMEM((dh, tq), F32),
                            pltpu.VMEM((seq, dh + LANES), BF16),
                            pltpu.VMEM((tq, tq), F32), pltpu.VMEM((1, tq), F32),
                            pltpu.VMEM((tq, tq), F32), pltpu.VMEM((1, tq), F32),
                            pltpu.VMEM((dh + LANES, tq), BF16)],
        ),
        out_shape=jax.ShapeDtypeStruct((batch * seq, heads * dh), BF16),
        compiler_params=_params("parallel", "parallel"),
        name="fox_attention",
    )(jnp.arange(nq, dtype=I32) // 2, qk, qk, v_t, c_tok, c_rows, gate)


def _fox_qk_kernel(x_ref, w_ref, g_ref, o_ref):
    y = _dot(x_ref[...], w_ref[...])
    g = g_ref[...]
    for hd in range(y.shape[1] // FOX_HEAD_DIM):
        sl = slice(hd * FOX_HEAD_DIM, (hd + 1) * FOX_HEAD_DIM)
        o_ref[:, sl] = _rms(y[:, sl], g[:, sl]).astype(o_ref.dtype)


def _fox_qk(a, w_qk, gain_row):
    t, d = a.shape
    n = w_qk.shape[1]
    tm, tn = _tile(t, ROW_TILE), _tile(n, COL_TILE)
    return pl.pallas_call(
        _fox_qk_kernel,
        grid=(t // tm, n // tn),
        in_specs=[pl.BlockSpec((tm, d), lambda i, j: (i, 0)), pl.BlockSpec((d, tn), lambda i, j: (0, j)),
                  pl.BlockSpec((1, tn), lambda i, j: (0, j))],
        out_specs=pl.BlockSpec((tm, tn), lambda i, j: (i, j)),
        out_shape=jax.ShapeDtypeStruct((t, n), BF16),
        compiler_params=_params("parallel", "parallel"),
        name="fox_qk",
    )(a, w_qk, gain_row)


def _fox_forget_kernel(h_ref, g_ref, w_ref, b_ref, c_ref, ctok_ref, carry_sc, *, tiles_per_seq):
    i = pl.program_id(0)

    @pl.when(i % tiles_per_seq == 0)
    def _():
        carry_sc[...] = jnp.zeros(carry_sc.shape, F32)

    z = _split_dot(_rms(h_ref[...], g_ref[...]), w_ref) + b_ref[...]
    log_f = jnp.minimum(z, 0.0) - jnp.log(1.0 + jnp.exp(-jnp.abs(z)))

    tm = log_f.shape[0]
    row = lax.broadcasted_iota(I32, (tm, tm), 0)
    col = lax.broadcasted_iota(I32, (tm, tm), 1)
    tri = (col <= row).astype(BF16)
    sums = _dot(tri, jnp.concatenate(_split(log_f, 3) + [jnp.zeros_like(log_f, BF16)], axis=1))
    c = (sums[:, :LANES] + (sums[:, LANES:2 * LANES] + sums[:, 2 * LANES:3 * LANES])) + carry_sc[...]
    carry_sc[...] = c[tm - 1:tm, :]
    c_ref[0] = c.T
    ctok_ref[...] = c


def _fox_forget_cumsum(h, gain, w_f, b_f, batch, seq):
    t, d = h.shape
    heads = w_f.shape[1]
    assert heads <= LANES
    tm = _tile(seq, ROUTE_TILE)
    w12 = jnp.concatenate(_split(jnp.pad(w_f, ((0, 0), (0, LANES - heads))), 2), axis=1)
    b = jnp.pad(b_f, (0, LANES - heads)).reshape(1, LANES)
    tps = seq // tm
    return pl.pallas_call(
        functools.partial(_fox_forget_kernel, tiles_per_seq=tps),
        grid=(t // tm,),
        in_specs=[pl.BlockSpec((tm, d), lambda i: (i, 0)), pl.BlockSpec((1, d), lambda i: (0, 0)),
                  pl.BlockSpec((d, 2 * LANES), lambda i: (0, 0)), pl.BlockSpec((1, LANES), lambda i: (0, 0))],
        out_specs=[pl.BlockSpec((1, LANES, tm), lambda i: (i // tps, 0, i % tps)),
                   pl.BlockSpec((tm, LANES), lambda i: (i, 0))],
        out_shape=[jax.ShapeDtypeStruct((batch, LANES, seq), F32), jax.ShapeDtypeStruct((t, LANES), F32)],
        scratch_shapes=[pltpu.VMEM((1, LANES), F32)],
        compiler_params=_params("arbitrary"),
        name="fox_forget_cumsum",
    )(h, gain.reshape(1, d), w12, b)


def _router_kernel(h_ref, g_ref, w_ref, b_ref, t_ref, meta_ref, metat_ref, cnt_ref, carry_sc):
    i = pl.program_id(0)

    @pl.when(i == 0)
    def _():
        carry_sc[...] = jnp.zeros(carry_sc.shape, F32)

    t = _rms(h_ref[...], g_ref[...])
    _to_slabs(t_ref, _pack_halves(t))

    logits = _split_dot(t, w_ref) + b_ref[...]
    tm = logits.shape[0]
    lane = lax.broadcasted_iota(I32, logits.shape, 1)
    lane_f = lane.astype(F32)
    first = lambda hit: jnp.min(jnp.where(hit, lane_f, float(LANES)), axis=1, keepdims=True).astype(I32)

    is_grp = lane < MOE_GROUPS
    gl = jnp.where(is_grp, logits, -jnp.inf)
    gmax = jnp.max(gl, axis=1, keepdims=True)
    g_sel = first(gl == gmax)
    gexp = jnp.where(is_grp, jnp.exp(logits - gmax), 0.0)
    g_w = 1.0 / jnp.sum(gexp, axis=1, keepdims=True)

    lo = MOE_GROUPS + MOE_EXPERTS_PER_GROUP * g_sel
    in_grp = jnp.logical_and(lane >= lo, lane < lo + MOE_EXPERTS_PER_GROUP)
    el = jnp.where(in_grp, logits, -jnp.inf)
    emax = jnp.max(el, axis=1, keepdims=True)
    eexp = jnp.where(in_grp, jnp.exp(logits - emax), 0.0)
    prob = eexp / jnp.sum(eexp, axis=1, keepdims=True)
    cand1 = jnp.where(in_grp, prob, -1.0)
    p1 = jnp.max(cand1, axis=1, keepdims=True)
    j1 = first(cand1 == p1)
    cand2 = jnp.where(lane == j1, -1.0, cand1)
    p2 = jnp.max(cand2, axis=1, keepdims=True)
    j2 = first(cand2 == p2)
    denom = p1 + p2
    wt1 = p1 / denom * g_w
    wt2 = p2 / denom * g_w
    e1 = j1 - MOE_GROUPS
    e2 = j2 - MOE_GROUPS

    hit1 = lane == e1
    hit2 = lane == e2
    row = lax.broadcasted_iota(I32, (tm, tm), 0)
    col = lax.broadcasted_iota(I32, (tm, tm), 1)
    before = (col < row).astype(BF16)
    pre = _dot(before, jnp.concatenate([hit1.astype(BF16), hit2.astype(BF16)], axis=1))
    pre1, pre2 = pre[:, :LANES], pre[:, LANES:]
    carry = carry_sc[...]
    cnt1 = jnp.sum(hit1.astype(F32), axis=0, keepdims=True)
    cnt2 = jnp.sum(hit2.astype(F32), axis=0, keepdims=True)
    rank1 = jnp.sum(jnp.where(hit1, pre1 + carry, 0.0), axis=1, keepdims=True)
    rank2 = jnp.sum(jnp.where(hit2, pre2 + (carry + cnt1), 0.0), axis=1, keepdims=True)
    total = carry + cnt1 + cnt2
    carry_sc[...] = total
    cnt_ref[...] = jnp.broadcast_to(total, cnt_ref.shape)

    bits = lambda x: lax.bitcast_convert_type(jnp.broadcast_to(x, logits.shape), I32)
    meta = jnp.where(lane == 0, e1, 0)
    meta = jnp.where(lane == 1, e2, meta)
    meta = jnp.where(lane == 2, rank1.astype(I32), meta)
    meta = jnp.where(lane == 3, rank2.astype(I32), meta)
    meta = jnp.where(lane == 4, bits(wt1), meta)
    meta = jnp.where(lane == 5, bits(wt2), meta)
    meta_ref[...] = meta
    metat_ref[...] = meta.T[:SUBLANES, :]


def _router(h, gain, w_grp, b_grp, w_rt, b_rt):
    t, d = h.shape
    tm = _tile(t, ROUTE_TILE)
    slab = d // 2 // LANES
    n_used = MOE_GROUPS + MOE_EXPERTS
    w = jnp.pad(jnp.concatenate([w_grp, w_rt], axis=1), ((0, 0), (0, LANES - n_used)))
    w12 = jnp.concatenate(_split(w, 2), axis=1)
    b = jnp.pad(jnp.concatenate([b_grp, b_rt]), (0, LANES - n_used)).reshape(1, LANES)
    return pl.pallas_call(
        _router_kernel,
        grid=(t // tm,),
        in_specs=[pl.BlockSpec((tm, d), lambda i: (i, 0)), pl.BlockSpec((1, d), lambda i: (0, 0)),
                  pl.BlockSpec((d, 2 * LANES), lambda i: (0, 0)), pl.BlockSpec((1, LANES), lambda i: (0, 0))],
        out_specs=[pl.BlockSpec((tm * slab, LANES), lambda i: (i, 0)),
                   pl.BlockSpec((tm, LANES), lambda i: (i, 0)),
                   pl.BlockSpec((SUBLANES, tm), lambda i: (0, i)),
                   pl.BlockSpec((SUBLANES, LANES), lambda i: (0, 0))],
        out_shape=[jax.ShapeDtypeStruct((t * slab, LANES), U32),
                   jax.ShapeDtypeStruct((t, LANES), I32),
                   jax.ShapeDtypeStruct((SUBLANES, t), I32),
                   jax.ShapeDtypeStruct((SUBLANES, LANES), F32)],
        scratch_shapes=[pltpu.VMEM((1, LANES), F32)],
        compiler_params=_params("arbitrary"),
        name="moe_router",
    )(h, gain.reshape(1, d), w12, b)


def _dest_kernel(offs_ref, mt_ref, dest_ref):
    mt = mt_ref[...]
    experts = mt[0:2, :]
    base = jnp.zeros(experts.shape, I32)
    for e in range(MOE_EXPERTS):
        base = jnp.where(experts == e, offs_ref[e], base)
    row = lax.broadcasted_iota(I32, mt.shape, 0)
    dest_ref[...] = jnp.where(row < 2, jnp.concatenate([base + mt[2:4, :], mt[2:SUBLANES, :]], axis=0), 0)


def _dest_rows_all(meta_t, offs):
    rows, t = meta_t.shape
    return pl.pallas_call(
        _dest_kernel,
        grid_spec=pltpu.PrefetchScalarGridSpec(
            num_scalar_prefetch=1,
            grid=(1,),
            in_specs=[pl.BlockSpec((rows, t), lambda i, offs: (0, 0))],
            out_specs=pl.BlockSpec((rows, t), lambda i, offs: (0, 0)),
        ),
        out_shape=jax.ShapeDtypeStruct((rows, t), I32),
        compiler_params=_params("arbitrary"),
        name="moe_dest",
    )(offs, meta_t)


def _slab_copy(src, src_tok, dst, dst_tok, sem, slab):
    rows = lambda tok: pl.ds(pl.multiple_of(tok * slab, slab), slab)
    return pltpu.make_async_copy(src.at[rows(src_tok)], dst.at[rows(dst_tok)], sem)


_ROW_UNROLL = 8


def _dispatch_kernel(zs_ref, nv_ref, dest_ref, t_ref, xs_ref, zero_sc, sem, *, tm, tz, slab):
    i = pl.program_id(0)

    @pl.when(i == 0)
    def _():
        zero_sc[...] = jnp.zeros(zero_sc.shape, zero_sc.dtype)
        fill = lambda tok0: pltpu.make_async_copy(
            zero_sc, xs_ref.at[pl.ds(pl.multiple_of(tok0 * slab, tz * slab), tz * slab)], sem)
        fills = [fill(zs_ref[e]) for e in range(MOE_EXPERTS)]
        for c in fills:
            c.start()
        for c in fills:
            c.wait()

        def tail(tile, carry):
            c = fill(tile * tz)
            c.start()
            c.wait()
            return carry

        lax.fori_loop(nv_ref[0], xs_ref.shape[0] // (tz * slab), tail, 0)

    def issue(blk, carry):
        for u in range(_ROW_UNROLL):
            r = blk * _ROW_UNROLL + u
            _slab_copy(t_ref, r, xs_ref, dest_ref[0, r], sem, slab).start(priority=0)
            _slab_copy(t_ref, r, xs_ref, dest_ref[1, r], sem, slab).start(priority=1)
        return carry

    lax.fori_loop(0, tm // _ROW_UNROLL, issue, 0)

    def drain(blk, carry):
        for u in range(2 * _ROW_UNROLL):
            _slab_copy(t_ref, 0, xs_ref, 0, sem, slab).wait()
        return carry

    lax.fori_loop(0, tm // _ROW_UNROLL, drain, 0)


def _dispatch(t_slabs, dest_t, zero_start, n_valid, n_rows):
    t = dest_t.shape[1]
    slab = t_slabs.shape[0] // t
    tm = _tile(t, MOVE_TILE)
    tz = EXPERT_TILE
    mspec = pl.BlockSpec((SUBLANES, tm), lambda i, *_: (0, i), memory_space=pltpu.SMEM)
    return pl.pallas_call(
        functools.partial(_dispatch_kernel, tm=tm, tz=tz, slab=slab),
        grid_spec=pltpu.PrefetchScalarGridSpec(
            num_scalar_prefetch=2,
            grid=(t // tm,),
            in_specs=[mspec, pl.BlockSpec((tm * slab, LANES), lambda i, *_: (i, 0))],
            out_specs=pl.BlockSpec(memory_space=pl.ANY),
            scratch_shapes=[pltpu.VMEM((tz * slab, LANES), U32), pltpu.SemaphoreType.DMA(())],
        ),
        out_shape=jax.ShapeDtypeStruct((n_rows * slab, LANES), U32),
        compiler_params=_params("arbitrary"),
        name="moe_dispatch",
    )(zero_start, n_valid, dest_t, t_slabs)


def _experts_kernel(te_ref, nv_ref, x_ref, wg_ref, wu_ref, wd_ref, y_ref, wg_sc, wu_sc, wd_sc, *, tm):
    i = pl.program_id(0)
    live = i < nv_ref[0]

    @pl.when(jnp.logical_or(i == 0, te_ref[i] != te_ref[jnp.maximum(i - 1, 0)]))
    def _():
        wg_sc[...] = wg_ref[0, 0].astype(BF16)
        wu_sc[...] = wu_ref[0, 0].astype(BF16)
        wd_sc[...] = wd_ref[0, 0].astype(BF16)

    @pl.when(jnp.logical_not(live))
    def _():
        y_ref[...] = jnp.zeros(y_ref.shape, y_ref.dtype)

    @pl.when(live)
    def _():
        hi, lo = _unpack_halves(_from_slabs(x_ref, tm))
        x = jnp.concatenate([hi, lo], axis=1).astype(BF16)
        g = _dot(x, wg_sc[...])
        u = _dot(x, wu_sc[...])
        hid = (g * jax.nn.sigmoid(g) * u).astype(BF16)
        _to_slabs(y_ref, _pack_halves(_dot(hid, wd_sc[...])))


def _experts(xs, tile_expert, n_valid, w_gate, w_up, w_down, layer, n_tiles):
    tm = EXPERT_TILE
    _, _, d, f = w_gate.shape
    rows = tm * (d // 2 // LANES)
    live = lambda i, nv: jnp.minimum(i, nv[0] - 1)
    return pl.pallas_call(
        functools.partial(_experts_kernel, tm=tm),
        grid_spec=pltpu.PrefetchScalarGridSpec(
            num_scalar_prefetch=2,
            grid=(n_tiles,),
            in_specs=[pl.BlockSpec((rows, LANES), lambda i, te, nv: (live(i, nv), 0)),
                      pl.BlockSpec((1, 1, d, f), lambda i, te, nv: (layer, te[i], 0, 0)),
                      pl.BlockSpec((1, 1, d, f), lambda i, te, nv: (layer, te[i], 0, 0)),
                      pl.BlockSpec((1, 1, f, d), lambda i, te, nv: (layer, te[i], 0, 0))],
            out_specs=pl.BlockSpec((rows, LANES), lambda i, te, nv: (i, 0)),
            scratch_shapes=[pltpu.VMEM((d, f), BF16), pltpu.VMEM((d, f), BF16), pltpu.VMEM((f, d), BF16)],
        ),
        out_shape=jax.ShapeDtypeStruct((n_tiles * rows, LANES), U32),
        compiler_params=_params("arbitrary"),
        name="moe_experts",
    )(tile_expert, n_valid, xs, w_gate, w_up, w_down)


def _combine_kernel(dest_ref, dnext_ref, meta_ref, h_ref, g_ref, ys_ref, *rest, tm, slab, final):
    if final:
        o_ref, y_sc, sem = rest
    else:
        hn_ref, a_ref, y_sc, sem = rest
    i = pl.program_id(0)
    n = pl.num_programs(0)

    def gather(dref, slot):
        def issue(blk, carry):
            for u in range(_ROW_UNROLL):
                r = blk * _ROW_UNROLL + u
                _slab_copy(ys_ref, dref[0, r], y_sc.at[slot, 0], r, sem.at[slot], slab).start(priority=0)
                _slab_copy(ys_ref, dref[1, r], y_sc.at[slot, 1], r, sem.at[slot], slab).start(priority=1)
            return carry

        lax.fori_loop(0, tm // _ROW_UNROLL, issue, 0)

    def finish(slot):
        def drain(blk, carry):
            for u in range(2 * _ROW_UNROLL):
                _slab_copy(ys_ref, 0, y_sc.at[slot, 0], 0, sem.at[slot], slab).wait()
            return carry

        lax.fori_loop(0, tm // _ROW_UNROLL, drain, 0)

        meta = meta_ref[...]
        lane = lax.broadcasted_iota(I32, meta.shape, 1)
        wbits = lax.bitcast_convert_type(meta, F32)
        wt1 = jnp.sum(jnp.where(lane == 4, wbits, 0.0), axis=1, keepdims=True)
        wt2 = jnp.sum(jnp.where(lane == 5, wbits, 0.0), axis=1, keepdims=True)
        hi1, lo1 = _unpack_halves(_from_slabs(y_sc.at[slot, 0], tm))
        hi2, lo2 = _unpack_halves(_from_slabs(y_sc.at[slot, 1], tm))
        moe = jnp.concatenate([wt1 * hi1 + wt2 * hi2, wt1 * lo1 + wt2 * lo2], axis=1)
        h_new = h_ref[...] + moe
        if final:
            o_ref[...] = _rms(h_new, g_ref[...])
        else:
            hn_ref[...] = h_new
            a_ref[...] = _rms(h_new, g_ref[...]).astype(a_ref.dtype)

    @pl.when(i == 0)
    def _():
        gather(dest_ref, 0)

    for slot in (0, 1):
        @pl.when((i & 1) == slot)
        def _():
            @pl.when(i + 1 < n)
            def _():
                gather(dnext_ref, 1 - slot)

            finish(slot)


def _combine(ys, dest_t, meta, h, gain, final):
    t, d = h.shape
    tm = _tile(t, COMBINE_TILE)
    slab = d // 2 // LANES
    n = t // tm
    mspec = pl.BlockSpec((SUBLANES, tm), lambda i: (0, i), memory_space=pltpu.SMEM)
    mnext = pl.BlockSpec((SUBLANES, tm), lambda i: (0, jnp.minimum(i + 1, n - 1)), memory_space=pltpu.SMEM)
    row = pl.BlockSpec((tm, d), lambda i: (i, 0))
    if final:
        out_specs, out_shape = row, jax.ShapeDtypeStruct((t, d), F32)
    else:
        out_specs = [row, row]
        out_shape = [jax.ShapeDtypeStruct((t, d), F32), jax.ShapeDtypeStruct((t, d), BF16)]
    return pl.pallas_call(
        functools.partial(_combine_kernel, tm=tm, slab=slab, final=final),
        grid=(n,),
        in_specs=[mspec, mnext, pl.BlockSpec((tm, LANES), lambda i: (i, 0)), row,
                  pl.BlockSpec((1, d), lambda i: (0, 0)), pl.BlockSpec(memory_space=pl.ANY)],
        out_specs=out_specs,
        out_shape=out_shape,
        scratch_shapes=[pltpu.VMEM((2, 2, tm * slab, LANES), U32), pltpu.SemaphoreType.DMA((2,))],
        compiler_params=_params("arbitrary"),
        name="moe_combine",
    )(dest_t, dest_t, meta, h, gain.reshape(1, d), ys)


def _hier_moe(h, ffn_gain, w_grp, b_grp, w_rt, b_rt, w_gate, w_up, w_down, layer, next_gain, final):
    t, d = h.shape
    tmx = EXPERT_TILE
    t_slabs, meta, meta_t, cnt = _router(h, ffn_gain, w_grp, b_grp, w_rt, b_rt)

    counts = cnt[0, :MOE_EXPERTS].astype(I32)
    padded = (counts + tmx - 1) // tmx * tmx
    ends = jnp.cumsum(padded)
    offs = ends - padded
    n_tiles = (2 * t) // tmx + MOE_EXPERTS
    n_valid = (ends[-1] // tmx).reshape(1)
    tile_start = jnp.minimum(jnp.arange(n_tiles, dtype=I32) * tmx, ends[-1] - 1)
    tile_expert = jnp.sum((ends[None, :] <= tile_start[:, None]).astype(I32), axis=1)
    zero_start = offs + counts // tmx * tmx

    dest_t = _dest_rows_all(meta_t, offs)
    xs = _dispatch(t_slabs, dest_t, zero_start, n_valid, (n_tiles + 1) * tmx)
    ys = _experts(xs, tile_expert, n_valid, w_gate, w_up, w_down, layer, n_tiles)
    return _combine(ys, dest_t, meta, h, next_gain, final)


def kernel(x, positions, attn_norm, ffn_norm, final_norm, mla_w_dq, mla_q_norm, mla_w_uq, mla_w_dkv,
           mla_kv_norm, mla_w_ukv, mla_w_o, fox_w_qkv, fox_q_norm, fox_k_norm, fox_w_f, fox_b_f,
           fox_w_og, fox_w_o, moe_w_grp, moe_b_grp, moe_w_rt, moe_b_rt, moe_w_gate, moe_w_up, moe_w_down):
    batch, seq, d = x.shape
    depth = attn_norm.shape[0]
    t = batch * seq
    cos_t, sin_t = _rope_tables(positions)
    h = x.reshape(t, d)
    a = _norm(h, attn_norm[0], BF16)
    out = None
    for i in range(depth):
        j = i // 2
        if i % 2 == 0:
            heads = mla_w_uq.shape[2] // (MLA_NOPE + MLA_ROPE)
            q, k, v_t = _mla_project(a, cos_t, sin_t, mla_w_dq[j], mla_q_norm[j], mla_w_uq[j],
                                     mla_w_dkv[j], mla_kv_norm[j], mla_w_ukv[j])
            o = _mla_attention(q, k, v_t, batch, seq, heads)
            w_o = mla_w_o[j]
        else:
            dh = FOX_HEAD_DIM
            heads = fox_w_qkv.shape[2] // (3 * dh)
            w_qkv = fox_w_qkv[j].astype(BF16)
            gain_row = jnp.concatenate([jnp.tile(fox_q_norm[j] * (dh ** -0.5 * LOG2_E), heads),
                                        jnp.tile(fox_k_norm[j], heads)])[None, :]
            qk = _fox_qk(a, w_qkv[:, :2 * heads * dh], gain_row)
            v_t = _mm_nt(w_qkv[:, 2 * heads * dh:].T, a, BF16, name="fox_v")
            gate = _mm(a, fox_w_og[j].astype(BF16), BF16, act="sigmoid", name="fox_gate")
            c_t, c_tok = _fox_forget_cumsum(h, attn_norm[i], fox_w_f[j], fox_b_f[j], batch, seq)
            c_rows = c_t[:, :heads, :].reshape(batch, heads, 1, seq)
            o = _fox_attention(qk, v_t, c_tok, c_rows, gate, batch, seq, heads)
            w_o = fox_w_o[j]
        h = _mm_residual(o, w_o.astype(BF16), h)
        final = i == depth - 1
        next_gain = final_norm if final else attn_norm[i + 1]
        res = _hier_moe(h, ffn_norm[i], moe_w_grp[i], moe_b_grp[i], moe_w_rt[i], moe_b_rt[i],
                        moe_w_gate, moe_w_up, moe_w_down, i, next_gain, final)
        if final:
            out = res
        else:
            h, a = res
    return out.reshape(batch, seq, d)
```
